```python
import math
import jax
import jax.numpy as jnp
from jax import lax
import numpy as np

D_MODEL = 2048
BATCH = 16
SEQ = 2048
DEPTH = 4
DEC_BATCH = 32
DEC_SEQ = 16
PAST_LEN = 1024

CHUNK = 64
N_MIXERS = 3
N_RET = (DEPTH + 2) // 3
N_HG = (DEPTH + 1) // 3
N_FOX = DEPTH // 3

RET_HEADS = 8
RET_DK = D_MODEL // RET_HEADS
RET_DV = 2 * RET_DK
RET_QK = RET_HEADS * RET_DK
RET_V = RET_HEADS * RET_DV
ROPE_BASE = 10000.0

HG_DK = 128
HG_HEADS = D_MODEL // HG_DK
HG_DV = D_MODEL // HG_HEADS
HG_BLOCK = 16

FOX_HEADS = 16
FOX_HD = D_MODEL // FOX_HEADS
Q_BLOCK = 128

N_EXPERTS = 16
N_GROUPS = 4
EXPERTS_PER_GROUP = N_EXPERTS // N_GROUPS
TOPK_GROUP = 1
TOP_K = 2
D_EXPERT = D_MODEL // 2
MOE_BLOCK = 128

ALPHA = (2 * DEPTH) ** 0.25
BETA = (8 * DEPTH) ** -0.25
LN_EPS = 1e-5
NORM_EPS = 1e-6

kernel_name = 'hybrid_streaming_encoder_step'

F32 = jnp.float32


def layer_norm(x, g, b):
    xf = x.astype(F32)
    mu = jnp.mean(xf, -1, keepdims=True)
    var = jnp.mean(jnp.square(xf - mu), -1, keepdims=True)
    return ((xf - mu) * lax.rsqrt(var + LN_EPS) * g.astype(F32) + b.astype(F32)).astype(x.dtype)


def modulation(c, w, b):
    m = jax.nn.silu(c) @ w + b
    return jnp.split(m[:, None, :], 6, axis=-1)


def post_norm(x, out, gate, g, b):
    return layer_norm(ALPHA * x + (1.0 + gate) * out, g, b)


def rope(x, pos):
    half = x.shape[-1] // 2
    inv = ROPE_BASE ** (-jnp.arange(half, dtype=F32) / half)
    ang = pos.astype(F32)[:, None] * inv[None, :]
    cos = jnp.cos(ang)[None, :, None, :]
    sin = jnp.sin(ang)[None, :, None, :]
    x1, x2 = x[..., :half], x[..., half:]
    return jnp.concatenate([x1 * cos - x2 * sin, x1 * sin + x2 * cos], axis=-1)


def retention_chunk(S, qkv, log_gamma):
    q, k, v = qkv
    C = q.shape[1]
    idx = jnp.arange(C, dtype=F32)
    dist = jnp.abs(idx[:, None] - idx[None, :])
    decay = jnp.exp(dist[None] * log_gamma[:, None, None])
    scores = jnp.einsum('bthd,bshd->bhts', q, k) * decay[None]
    intra = jnp.einsum('bhts,bshe->bthe', scores, v)
    q_dec = jnp.exp((idx + 1.0)[:, None] * log_gamma[None, :])
    inter = jnp.einsum('bthd,bhde->bthe', q * q_dec[None, :, :, None], S)
    k_dec = jnp.exp((C - 1.0 - idx)[:, None] * log_gamma[None, :])
    S_new = (S * jnp.exp(C * log_gamma)[None, :, None, None]
             + jnp.einsum('bshd,bshe->bhde', k * k_dec[None, :, :, None], v))
    return S_new, intra + inter


def retention_scan(q, k, v, S0):
    B, L = q.shape[:2]
    cl = min(L, CHUNK)
    n = L // cl
    log_gamma = jnp.log1p(-jnp.exp2(-5.0 - jnp.arange(RET_HEADS, dtype=F32)))
    to_chunks = lambda a: a.reshape(B, n, cl, *a.shape[2:]).swapaxes(0, 1)
    S, o = lax.scan(lambda s, c: retention_chunk(s, c, log_gamma), S0,
                    (to_chunks(q), to_chunks(k), to_chunks(v)))
    return o.swapaxes(0, 1).reshape(B, L, RET_HEADS, RET_DV), S


def retention_mixer(u, w_in, gn_w, w_out, S0, pos0):
    B, L, _ = u.shape
    proj = (u @ w_in).astype(F32)
    q, k, v, g = jnp.split(proj, [RET_QK, 2 * RET_QK, 2 * RET_QK + RET_V], axis=-1)
    pos = pos0 + jnp.arange(L)
    q = rope(q.reshape(B, L, RET_HEADS, RET_DK), pos)
    k = rope(k.reshape(B, L, RET_HEADS, RET_DK), pos) * (RET_DK ** -0.5)
    v = v.reshape(B, L, RET_HEADS, RET_DV)
    o, S = retention_scan(q, k, v, S0.astype(F32))
    mu = jnp.mean(o, -1, keepdims=True)
    var = jnp.mean(jnp.square(o - mu), -1, keepdims=True)
    y = ((o - mu) * lax.rsqrt(var + NORM_EPS)).reshape(B, L, RET_V) * gn_w.astype(F32)
    out = (jax.nn.silu(g) * y).astype(u.dtype) @ w_out
    return out, S


def hgrn2_block(S, inp):
    q, logf, k, v = inp
    T = q.shape[1]
    G = jnp.cumsum(logf, axis=1)
    causal = jnp.tril(jnp.ones((T, T), dtype=bool))
    diff = G[:, :, None] - G[:, None, :]
    decay = jnp.exp(jnp.where(causal[None, :, :, None, None], diff, -jnp.inf))
    A = jnp.sum(q[:, :, None] * k[:, None, :] * decay, axis=-1)
    intra = jnp.einsum('btsh,bshe->bthe', A, v)
    inter = jnp.einsum('bthd,bhde->bthe', q * jnp.exp(G), S)
    G_last = G[:, -1]
    S_new = (S * jnp.exp(G_last)[..., None]
             + jnp.einsum('bshd,bshe->bhde', k * jnp.exp(G_last[:, None] - G), v))
    return S_new, intra + inter


def hgrn2_scan(q, logf, k, v, S0):
    B, L = q.shape[:2]
    pad = (-L) % HG_BLOCK
    n = (L + pad) // HG_BLOCK
    to_blocks = lambda a: jnp.pad(a, ((0, 0), (0, pad), (0, 0), (0, 0))).reshape(
        B, n, HG_BLOCK, *a.shape[2:]).swapaxes(0, 1)
    S, o = lax.scan(hgrn2_block, S0, (to_blocks(q), to_blocks(logf), to_blocks(k), to_blocks(v)))
    o = o.swapaxes(0, 1).reshape(B, L + pad, HG_HEADS, HG_DV)[:, :L]
    return o, S


def hgrn2_mixer(u, w_in, b_f, lb, norm_w, w_out, S0):
    B, L, _ = u.shape
    proj = (u @ w_in).astype(F32)
    q, fz, i, g = jnp.split(proj, 4, axis=-1)
    f = lb + (1.0 - lb) * jax.nn.sigmoid(fz + b_f.astype(F32))
    logf = jnp.log(f)
    k = 1.0 - f
    heads = lambda a: a.reshape(B, L, HG_HEADS, -1)
    o, S = hgrn2_scan(heads(q), heads(logf), heads(k), heads(i), S0.astype(F32))
    o = o * lax.rsqrt(jnp.mean(jnp.square(o), -1, keepdims=True) + NORM_EPS)
    y = o.reshape(B, L, D_MODEL) * norm_w.astype(F32)
    out = (y * jax.nn.silu(g)).astype(u.dtype) @ w_out
    return out, S


def fox_project(u, w_in, b_f):
    B, L, _ = u.shape
    proj = (u @ w_in).astype(F32)
    q, k, v, fz = jnp.split(proj, [D_MODEL, 2 * D_MODEL, 3 * D_MODEL], axis=-1)
    heads = lambda a: a.reshape(B, L, FOX_HEADS, FOX_HD)
    logf = jax.nn.log_sigmoid(fz + b_f.astype(F32))
    return heads(q), heads(k), heads(v), logf


def fox_attend(q, k, v, cq, ck, q0):
    Q, K = q.shape[1], k.shape[1]
    s = jnp.einsum('bqhd,bkhd->bhqk', q, k) * (FOX_HD ** -0.5)
    bias = cq.transpose(0, 2, 1)[..., :, None] - ck.transpose(0, 2, 1)[..., None, :]
    qpos = q0 + jnp.arange(Q)
    kpos = jnp.arange(K)
    s = jnp.where((kpos[None, :] <= qpos[:, None])[None, None], s + bias, -jnp.inf)
    p = jax.nn.softmax(s, axis=-1)
    return jnp.einsum('bhqk,bkhd->bqhd', p, v)


def fox_mixer_prompt(u, w_in, b_f, w_out):
    B, L, _ = u.shape
    q, k, v, logf = fox_project(u, w_in, b_f)
    csum = jnp.cumsum(logf, axis=1)

    def block(i):
        s0 = i * Q_BLOCK
        qb = lax.dynamic_slice_in_dim(q, s0, Q_BLOCK, axis=1)
        cb = lax.dynamic_slice_in_dim(csum, s0, Q_BLOCK, axis=1)
        return fox_attend(qb, k, v, cb, csum, s0)

    o = lax.map(block, jnp.arange(L // Q_BLOCK))
    o = o.swapaxes(0, 1).reshape(B, L, D_MODEL)
    return o.astype(u.dtype) @ w_out, k, v, logf


def fox_mixer_sample(u, w_in, b_f, w_out, cache_k, cache_v, cache_logf):
    B, L, _ = u.shape
    q, k, v, logf = fox_project(u, w_in, b_f)
    P = cache_k.shape[1]
    k_all = jnp.concatenate([cache_k.astype(F32), k], axis=1)
    v_all = jnp.concatenate([cache_v.astype(F32), v], axis=1)
    csum = jnp.cumsum(jnp.concatenate([cache_logf.astype(F32), logf], axis=1), axis=1)
    o = fox_attend(q, k_all, v_all, csum[:, P:], csum, P)
    return o.reshape(B, L, D_MODEL).astype(u.dtype) @ w_out, k, v, logf


def moe_ffn(h, router_w, router_b, w_gate, w_up, w_down):
    B, L, D = h.shape
    T = B * L
    x = h.reshape(T, D)
    scores = jax.nn.sigmoid(x.astype(F32) @ router_w.astype(F32))
    sel = scores + router_b.astype(F32)
    grp_score = lax.top_k(sel.reshape(T, N_GROUPS, EXPERTS_PER_GROUP), 2)[0].sum(-1)
    _, gidx = lax.top_k(grp_score, TOPK_GROUP)
    gmask = jnp.any(gidx[:, :, None] == jnp.arange(N_GROUPS)[None, None, :], axis=1)
    emask = jnp.repeat(gmask, EXPERTS_PER_GROUP, axis=1)
    _, eidx = lax.top_k(jnp.where(emask, sel, -jnp.inf), TOP_K)
    wts = jnp.take_along_axis(scores, eidx, axis=1)
    wts = wts / jnp.sum(wts, -1, keepdims=True)
    A = T * TOP_K
    flat_e = eidx.reshape(A).astype(jnp.int32)
    flat_t = jnp.repeat(jnp.arange(T, dtype=jnp.int32), TOP_K)
    flat_w = wts.reshape(A)
    order = jnp.argsort(flat_e)
    se, st, sw = flat_e[order], flat_t[order], flat_w[order]
    counts = jnp.bincount(flat_e, length=N_EXPERTS)
    padded = (counts + MOE_BLOCK - 1) // MOE_BLOCK * MOE_BLOCK
    pad_end = jnp.cumsum(padded)
    pad_start = pad_end - padded
    cnt_start = jnp.cumsum(counts) - counts
    dest = pad_start[se] + jnp.arange(A, dtype=jnp.int32) - cnt_start[se]
    n_blocks = (A + N_EXPERTS * (MOE_BLOCK - 1) + MOE_BLOCK - 1) // MOE_BLOCK
    R = n_blocks * MOE_BLOCK
    row_tok = jnp.zeros((R,), jnp.int32).at[dest].set(st)
    row_w = jnp.zeros((R,), F32).at[dest].set(sw)
    blk_e = jnp.minimum(jnp.searchsorted(pad_end, jnp.arange(n_blocks, dtype=jnp.int32) * MOE_BLOCK,
                                         side='right'), N_EXPERTS - 1)

    def expert_block(args):
        e, tok = args
        xb = jnp.take(x, tok, axis=0)
        return (jax.nn.silu(xb @ w_gate[e]) * (xb @ w_up[e])) @ w_down[e]

    ys = lax.map(expert_block, (blk_e, row_tok.reshape(n_blocks, MOE_BLOCK)))
    out = jnp.zeros((T, D), F32).at[row_tok].add(ys.reshape(R, D).astype(F32) * row_w[:, None])
    return out.astype(h.dtype).reshape(B, L, D)


def setup_inputs(seed: int = 0) -> dict:
    key = jax.random.key(seed)
    ks = jax.random.split(key, 32)
    D = D_MODEL
    nrm = lambda k, shape, scale: jax.random.normal(k, shape, F32) * scale
    return {
        'x_prompt': nrm(ks[0], (BATCH, SEQ, D), 1.0),
        'x_sample': nrm(ks[1], (DEC_BATCH, DEC_SEQ, D), 1.0),
        'state_ret': nrm(ks[2], (N_RET, DEC_BATCH, RET_HEADS, RET_DK, RET_DV), 1.0),
        'state_hgrn': nrm(ks[3], (N_HG, DEC_BATCH, HG_HEADS, HG_DK, HG_DV), 1.0),
        'cache_fox_k': nrm(ks[4], (N_FOX, DEC_BATCH, PAST_LEN, FOX_HEADS, FOX_HD), 1.0),
        'cache_fox_v': nrm(ks[5], (N_FOX, DEC_BATCH, PAST_LEN, FOX_HEADS, FOX_HD), 1.0),
        'cache_fox_logf': jax.nn.log_sigmoid(nrm(ks[6], (N_FOX, DEC_BATCH, PAST_LEN, FOX_HEADS), 1.0) + 2.0),
        'c_prompt': nrm(ks[7], (BATCH, D), 1.0),
        'c_sample': nrm(ks[8], (DEC_BATCH, D), 1.0),
        'ada_w': nrm(ks[9], (DEPTH, D, 6 * D), 0.1 * D ** -0.5),
        'ada_b': nrm(ks[10], (DEPTH, 6 * D), 0.01),
        'ln_mix_g': 1.0 + nrm(ks[11], (DEPTH, D), 0.01),
        'ln_mix_b': nrm(ks[12], (DEPTH, D), 0.01),
        'ln_ffn_g': 1.0 + nrm(ks[13], (DEPTH, D), 0.01),
        'ln_ffn_b': nrm(ks[14], (DEPTH, D), 0.01),
        'ret_w_in': nrm(ks[15], (N_RET, D, 2 * RET_QK + 2 * RET_V), D ** -0.5),
        'ret_gn_w': 1.0 + nrm(ks[16], (N_RET, RET_V), 0.01),
        'ret_w_out': nrm(ks[17], (N_RET, RET_V, D), BETA * RET_V ** -0.5),
        'hg_w_in': nrm(ks[18], (N_HG, D, 4 * D), D ** -0.5),
        'hg_b_f': nrm(ks[19], (N_HG, D), 0.1),
        'hg_lower_bounds': nrm(ks[20], (DEPTH, D), 0.1),
        'hg_norm_w': 1.0 + nrm(ks[21], (N_HG, D), 0.01),
        'hg_w_out': nrm(ks[22], (N_HG, D, D), BETA * D ** -0.5),
        'fox_w_in': nrm(ks[23], (N_FOX, D, 3 * D + FOX_HEADS), D ** -0.5),
        'fox_b_f': 2.0 + nrm(ks[24], (N_FOX, FOX_HEADS), 0.1),
        'fox_w_out': nrm(ks[25], (N_FOX, D, D), BETA * D ** -0.5),
        'router_w': nrm(ks[26], (D, N_EXPERTS), D ** -0.5),
        'router_b': nrm(ks[27], (N_EXPERTS,), 0.01),
        'moe_w_gate': nrm(ks[28], (DEPTH, N_EXPERTS, D, D_EXPERT), D ** -0.5),
        'moe_w_up': nrm(ks[29], (DEPTH, N_EXPERTS, D, D_EXPERT), D ** -0.5),
        'moe_w_down': nrm(ks[30], (DEPTH, N_EXPERTS, D_EXPERT, D), BETA * D_EXPERT ** -0.5),
    }


def reference(x_prompt, x_sample, state_ret, state_hgrn, cache_fox_k, cache_fox_v, cache_fox_logf,
              c_prompt, c_sample, ada_w, ada_b, ln_mix_g, ln_mix_b, ln_ffn_g, ln_ffn_b,
              ret_w_in, ret_gn_w, ret_w_out, hg_w_in, hg_b_f, hg_lower_bounds, hg_norm_w, hg_w_out,
              fox_w_in, fox_b_f, fox_w_out, router_w, router_b, moe_w_gate, moe_w_up, moe_w_down):
    dt = x_prompt.dtype
    lbs = jnp.cumsum(jax.nn.softmax(hg_lower_bounds.astype(F32), axis=0), axis=0)
    lbs = lbs - lbs[0]
    xp, xs = x_prompt, x_sample
    ret_p, ret_s, hg_p, hg_s = [], [], [], []
    fk_p, fv_p, fl_p, fk_s, fv_s, fl_s = [], [], [], [], [], []
    for layer in range(DEPTH):
        mp = modulation(c_prompt, ada_w[layer], ada_b[layer])
        ms = modulation(c_sample, ada_w[layer], ada_b[layer])
        up = xp * (1.0 + mp[1]) + mp[0]
        us = xs * (1.0 + ms[1]) + ms[0]
        kind = layer % N_MIXERS
        j = layer // N_MIXERS
        if kind == 0:
            S0 = jnp.zeros((up.shape[0], RET_HEADS, RET_DK, RET_DV), F32)
            mix_p, S = retention_mixer(up, ret_w_in[j], ret_gn_w[j], ret_w_out[j], S0, 0)
            ret_p.append(S)
            mix_s, S = retention_mixer(us, ret_w_in[j], ret_gn_w[j], ret_w_out[j], state_ret[j], PAST_LEN)
            ret_s.append(S)
        elif kind == 1:
            S0 = jnp.zeros((up.shape[0], HG_HEADS, HG_DK, HG_DV), F32)
            mix_p, S = hgrn2_mixer(up, hg_w_in[j], hg_b_f[j], lbs[layer], hg_norm_w[j], hg_w_out[j], S0)
            hg_p.append(S)
            mix_s, S = hgrn2_mixer(us, hg_w_in[j], hg_b_f[j], lbs[layer], hg_norm_w[j], hg_w_out[j],
                                   state_hgrn[j])
            hg_s.append(S)
        else:
            mix_p, k, v, lf = fox_mixer_prompt(up, fox_w_in[j], fox_b_f[j], fox_w_out[j])
            fk_p.append(k)
            fv_p.append(v)
            fl_p.append(lf)
            mix_s, k, v, lf = fox_mixer_sample(us, fox_w_in[j], fox_b_f[j], fox_w_out[j],
                                               cache_fox_k[j], cache_fox_v[j], cache_fox_logf[j])
            fk_s.append(k)
            fv_s.append(v)
            fl_s.append(lf)
        xp = post_norm(xp, mix_p, mp[2], ln_mix_g[layer], ln_mix_b[layer])
        xs = post_norm(xs, mix_s, ms[2], ln_mix_g[layer], ln_mix_b[layer])
        up = xp * (1.0 + mp[4]) + mp[3]
        us = xs * (1.0 + ms[4]) + ms[3]
        ffn_p = moe_ffn(up, router_w, router_b, moe_w_gate[layer], moe_w_up[layer], moe_w_down[layer])
        ffn_s = moe_ffn(us, router_w, router_b, moe_w_gate[layer], moe_w_up[layer], moe_w_down[layer])
        xp = post_norm(xp, ffn_p, mp[5], ln_ffn_g[layer], ln_ffn_b[layer])
        xs = post_norm(xs, ffn_s, ms[5], ln_ffn_g[layer], ln_ffn_b[layer])
    ret_state_prompt = jnp.stack(ret_p).astype(dt)
    ret_state_sample = jnp.stack(ret_s).astype(dt)
    hgrn_state_prompt = jnp.stack(hg_p).astype(dt)
    hgrn_state_sample = jnp.stack(hg_s).astype(dt)
    fox_k_prompt = jnp.stack(fk_p).astype(dt)
    fox_v_prompt = jnp.stack(fv_p).astype(dt)
    fox_logf_prompt = jnp.stack(fl_p).astype(dt)
    fox_k_sample = jnp.stack(fk_s).astype(dt)
    fox_v_sample = jnp.stack(fv_s).astype(dt)
    fox_logf_sample = jnp.stack(fl_s).astype(dt)
    return (xp, xs, ret_state_prompt, ret_state_sample, hgrn_state_prompt, hgrn_state_sample,
            fox_k_prompt, fox_v_prompt, fox_logf_prompt, fox_k_sample, fox_v_sample, fox_logf_sample)
```

```python
import functools

import jax
import jax.numpy as jnp
from jax import lax
from jax.experimental import pallas as pl
from jax.experimental.pallas import tpu as pltpu

F32 = jnp.float32
BF16 = jnp.bfloat16

D_MODEL = 2048
DEPTH = 4
CHUNK = 64
N_MIXERS = 3
RET_HEADS = 8
RET_DK = D_MODEL // RET_HEADS
RET_DV = 2 * RET_DK
RET_QK = RET_HEADS * RET_DK
RET_V = RET_HEADS * RET_DV
ROPE_BASE = 10000.0
HG_DK = 128
HG_HEADS = D_MODEL // HG_DK
HG_DV = D_MODEL // HG_HEADS
HG_BLOCK = 16
FOX_HEADS = 16
FOX_HD = D_MODEL // FOX_HEADS
N_EXPERTS = 16
N_GROUPS = 4
EXPERTS_PER_GROUP = N_EXPERTS // N_GROUPS
TOPK_GROUP = 1
TOP_K = 2
D_EXPERT = D_MODEL // 2
ALPHA = (2 * DEPTH) ** 0.25
LN_EPS = 1e-5
NORM_EPS = 1e-6

LANES = 128
VMEM_LIMIT = 56 * 1024 * 1024


def _params(*sem):
    return pltpu.CompilerParams(dimension_semantics=sem, vmem_limit_bytes=VMEM_LIMIT)


def _tile(n, pref):
    t = min(n, pref)
    while n % t:
        t //= 2
    return t


def _mod_kernel(c_ref, w_ref, b_ref, o_ref):
    c = c_ref[...]
    a = (c * jax.nn.sigmoid(c)).astype(BF16)
    o_ref[0] = jnp.dot(a, w_ref[0].astype(BF16), preferred_element_type=F32) + b_ref[0]


def _modulation_all(c_all, ada_w, ada_b):
    nb = c_all.shape[0]
    depth, d, n = ada_w.shape
    tn = _tile(n, 1024)
    return pl.pallas_call(
        _mod_kernel,
        out_shape=jax.ShapeDtypeStruct((depth, nb, n), F32),
        grid=(depth, n // tn),
        in_specs=[pl.BlockSpec((nb, d), lambda l, j: (0, 0)),
                  pl.BlockSpec((1, d, tn), lambda l, j: (l, 0, j)),
                  pl.BlockSpec((1, 1, tn), lambda l, j: (l, 0, j))],
        out_specs=pl.BlockSpec((1, nb, tn), lambda l, j: (l, 0, j)),
        compiler_params=_params("parallel", "parallel"),
        name="modulation",
    )(c_all, ada_w, ada_b.reshape(depth, 1, n))


def _modulate_kernel(x_ref, sc_ref, sh_ref, u_ref):
    u_ref[...] = x_ref[...] * (1.0 + sc_ref[0]) + sh_ref[0]


def _mod_spec(mod, bps):
    return pl.BlockSpec((1,) + mod.shape[1:], lambda i, *_: (i // bps, 0, 0))


def _modulate(x, sc, sh, tm, bps):
    t, d = x.shape
    row = pl.BlockSpec((tm, d), lambda i: (i, 0))
    return pl.pallas_call(
        _modulate_kernel,
        out_shape=jax.ShapeDtypeStruct((t, d), F32),
        grid=(t // tm,),
        in_specs=[row, _mod_spec(sc, bps), _mod_spec(sh, bps)],
        out_specs=row,
        compiler_params=_params("parallel"),
        name="modulate",
    )(x, sc, sh)


def _inproj_kernel(x_ref, w_ref, o_ref, xb_ref):
    @pl.when(pl.program_id(1) == 0)
    def _():
        xb_ref[...] = x_ref[...].astype(BF16)

    o_ref[...] = jnp.dot(xb_ref[...], w_ref[...], preferred_element_type=F32)


def _inproj(u, w, tm):
    t, d = u.shape
    n = w.shape[1]
    tn = _tile(n, 1024)
    return pl.pallas_call(
        _inproj_kernel,
        out_shape=jax.ShapeDtypeStruct((t, n), F32),
        grid=(t // tm, n // tn),
        in_specs=[pl.BlockSpec((tm, d), lambda i, j: (i, 0)),
                  pl.BlockSpec((d, tn), lambda i, j: (0, j))],
        out_specs=pl.BlockSpec((tm, tn), lambda i, j: (i, j)),
        scratch_shapes=[pltpu.VMEM((tm, d), BF16)],
        compiler_params=_params("parallel", "arbitrary"),
        name="inproj",
    )(u, w)


def _fox_inproj_kernel(x_ref, w_ref, *outs, tok, heads):
    acc = jnp.dot(x_ref[...].astype(BF16), w_ref[...], preferred_element_type=F32)
    n = 0
    if tok:
        outs[n][...] = acc
        n += 1
    if heads:
        for h in range(FOX_HEADS):
            outs[n][0, h] = acc[:, h * FOX_HD:(h + 1) * FOX_HD].astype(BF16)


def _fox_inproj(u, w, b, l, tm, tok, heads):
    t, d = u.shape
    bps = l // tm
    out_shape, out_specs = [], []
    if tok:
        out_shape.append(jax.ShapeDtypeStruct((t, d), F32))
        out_specs.append(pl.BlockSpec((tm, d), lambda i: (i, 0)))
    if heads:
        out_shape.append(jax.ShapeDtypeStruct((b, FOX_HEADS, l, FOX_HD), BF16))
        out_specs.append(pl.BlockSpec((1, FOX_HEADS, tm, FOX_HD), lambda i: (i // bps, 0, i % bps, 0)))
    return pl.pallas_call(
        functools.partial(_fox_inproj_kernel, tok=tok, heads=heads),
        out_shape=out_shape,
        grid=(t // tm,),
        in_specs=[pl.BlockSpec((tm, d), lambda i: (i, 0)),
                  pl.BlockSpec((d, d), lambda i: (0, 0))],
        out_specs=out_specs,
        compiler_params=_params("parallel"),
        name="fox_inproj",
    )(u, w)


def _fox_gate_kernel(x_ref, w_ref, b_ref, o_ref):
    z = jnp.dot(x_ref[...].astype(BF16), w_ref[...], preferred_element_type=F32) + b_ref[...]
    o_ref[...] = jnp.minimum(z, 0.0) - jnp.log1p(jnp.exp(-jnp.abs(z)))


def _fox_gate(u, w, b, tm):
    t, d = u.shape
    return pl.pallas_call(
        _fox_gate_kernel,
        out_shape=jax.ShapeDtypeStruct((t, LANES), F32),
        grid=(t // tm,),
        in_specs=[pl.BlockSpec((tm, d), lambda i: (i, 0)),
                  pl.BlockSpec((d, LANES), lambda i: (0, 0)),
                  pl.BlockSpec((1, LANES), lambda i: (0, 0))],
        out_specs=pl.BlockSpec((tm, LANES), lambda i: (i, 0)),
        compiler_params=_params("parallel"),
        name="fox_gate",
    )(u, w, b)


def _retention_kernel(lg_ref, q_ref, k_ref, v_ref, g_ref, cos_ref, sin_ref, gn_ref, *rest,
                      cl, n_chunks, has_state):
    if has_state:
        s0_ref, y_ref, sout_ref, s_ref = rest
    else:
        y_ref, sout_ref, s_ref = rest
    h = pl.program_id(1)
    li = pl.program_id(2)

    @pl.when(li == 0)
    def _():
        if has_state:
            s_ref[...] = s0_ref[0, 0]
        else:
            s_ref[...] = jnp.zeros_like(s_ref)

    lg = lg_ref[h]
    half = RET_DK // 2
    ti = lax.broadcasted_iota(jnp.int32, (cl, cl), 0)
    si = lax.broadcasted_iota(jnp.int32, (cl, cl), 1)
    decay = jnp.exp(jnp.abs(ti - si).astype(F32) * lg)
    idx = lax.broadcasted_iota(jnp.int32, (cl, 1), 0).astype(F32)
    q_dec = jnp.exp((idx + 1.0) * lg)
    k_dec = jnp.exp((cl - 1.0 - idx) * lg)
    s_dec = jnp.exp(jnp.full((1, 1), cl, F32) * lg)

    def rope(x, cos, sin):
        x1, x2 = x[:, :half], x[:, half:]
        return jnp.concatenate([x1 * cos - x2 * sin, x1 * sin + x2 * cos], axis=-1)

    for c in range(n_chunks):
        rows = slice(c * cl, (c + 1) * cl)
        cos, sin = cos_ref[rows, :], sin_ref[rows, :]
        q = rope(q_ref[0, rows, :], cos, sin)
        k = rope(k_ref[0, rows, :], cos, sin) * (RET_DK ** -0.5)
        vb = v_ref[0, rows, :].astype(BF16)
        scores = lax.dot_general(q.astype(BF16), k.astype(BF16), (((1,), (1,)), ((), ())),
                                 preferred_element_type=F32) * decay
        intra = jnp.dot(scores.astype(BF16), vb, preferred_element_type=F32)
        s = s_ref[...]
        inter = jnp.dot((q * q_dec).astype(BF16), s.astype(BF16), preferred_element_type=F32)
        kd = (k * k_dec).T.astype(BF16)
        s_ref[...] = s * s_dec + jnp.dot(kd, vb, preferred_element_type=F32)
        o = intra + inter
        mu = jnp.mean(o, axis=-1, keepdims=True)
        oc = o - mu
        var = jnp.mean(oc * oc, axis=-1, keepdims=True)
        y = oc * lax.rsqrt(var + NORM_EPS) * gn_ref[...]
        g = g_ref[0, rows, :]
        y_ref[0, rows, :] = (g * jax.nn.sigmoid(g) * y).astype(BF16)

    @pl.when(li == pl.num_programs(2) - 1)
    def _():
        sout_ref[0, 0] = s_ref[...]


def _retention(proj, cos, sin, log_gamma, gn_w, s0, b, l):
    cl = min(l, CHUNK)
    lb = _tile(l, 4 * cl)
    n_chunks = lb // cl
    p3 = proj.reshape(b, l, proj.shape[1])
    nq = RET_QK // RET_DK
    nv = (2 * RET_QK) // RET_DV
    has_state = s0 is not None
    in_specs = [pl.BlockSpec(memory_space=pltpu.SMEM),
                pl.BlockSpec((1, lb, RET_DK), lambda bi, h, li: (bi, li, h)),
                pl.BlockSpec((1, lb, RET_DK), lambda bi, h, li: (bi, li, nq + h)),
                pl.BlockSpec((1, lb, RET_DV), lambda bi, h, li: (bi, li, nv + h)),
                pl.BlockSpec((1, lb, RET_DV), lambda bi, h, li: (bi, li, nv + RET_HEADS + h)),
                pl.BlockSpec((lb, RET_DK // 2), lambda bi, h, li: (li, 0)),
                pl.BlockSpec((lb, RET_DK // 2), lambda bi, h, li: (li, 0)),
                pl.BlockSpec((1, RET_DV), lambda bi, h, li: (0, h))]
    args = [log_gamma, p3, p3, p3, p3, cos, sin, gn_w.reshape(1, RET_V)]
    state_spec = pl.BlockSpec((1, 1, RET_DK, RET_DV), lambda bi, h, li: (bi, h, 0, 0))
    if has_state:
        in_specs.append(state_spec)
        args.append(s0)
    y, s = pl.pallas_call(
        functools.partial(_retention_kernel, cl=cl, n_chunks=n_chunks, has_state=has_state),
        out_shape=[jax.ShapeDtypeStruct((b, l, RET_V), BF16),
                   jax.ShapeDtypeStruct((b, RET_HEADS, RET_DK, RET_DV), F32)],
        grid=(b, RET_HEADS, l // lb),
        in_specs=in_specs,
        out_specs=[pl.BlockSpec((1, lb, RET_DV), lambda bi, h, li: (bi, li, h)), state_spec],
        scratch_shapes=[pltpu.VMEM((RET_DK, RET_DV), F32)],
        compiler_params=_params("parallel", "parallel", "arbitrary"),
        name="retention",
    )(*args)
    return y.reshape(b * l, RET_V), s


def _hgrn_kernel(q_ref, fz_ref, v_ref, g_ref, bf_ref, lb_ref, nw_ref, tri_ref, *rest, lb_rows, has_state):
    if has_state:
        s0_ref, y_ref, sout_ref, st_ref, gc_ref, k_ref, o_ref = rest
    else:
        y_ref, sout_ref, st_ref, gc_ref, k_ref, o_ref = rest
    li = pl.program_id(1)
    hb = HG_BLOCK
    half = hb // 2

    @pl.when(li == 0)
    def _():
        for h in range(HG_HEADS):
            if has_state:
                st_ref[h] = s0_ref[0, h].T
            else:
                st_ref[h] = jnp.zeros((HG_DV, HG_DK), F32)

    lbv = lb_ref[...]
    f = lbv + (1.0 - lbv) * jax.nn.sigmoid(fz_ref[0] + bf_ref[...])
    logf = jnp.log(f)
    k_ref[...] = 1.0 - f
    hi = logf.astype(BF16)
    r1 = logf - hi.astype(F32)
    mid = r1.astype(BF16)
    lo = (r1 - mid.astype(F32)).astype(BF16)
    tri = tri_ref[...]
    gc_ref[...] = (jnp.dot(tri, hi, preferred_element_type=F32)
                   + jnp.dot(tri, mid, preferred_element_type=F32)
                   + jnp.dot(tri, lo, preferred_element_type=F32))

    rt = lax.broadcasted_iota(jnp.int32, (half, 1), 0)

    def block(bi, carry):
        r0 = pl.multiple_of(bi * hb, hb)
        for h in range(HG_HEADS):
            cs = slice(h * HG_DK, (h + 1) * HG_DK)
            gb = gc_ref[pl.ds(r0, hb), cs]
            qb = q_ref[0, pl.ds(r0, hb), cs]
            kb = k_ref[pl.ds(r0, hb), cs]
            vb = v_ref[0, pl.ds(r0, hb), cs]
            q_top, q_bot = qb[:half], qb[half:]
            g_top, g_bot = gb[:half], gb[half:]
            i_top = jnp.zeros((half, HG_DV), F32)
            i_bot = jnp.zeros((half, HG_DV), F32)
            for s in range(hb):
                gs, ks, vs = gb[s:s + 1], kb[s:s + 1], vb[s:s + 1]
                if s < half:
                    e = jnp.where(rt >= s, jnp.exp(g_top - gs), 0.0)
                    a = jnp.sum(q_top * e * ks, axis=-1, keepdims=True)
                    i_top = i_top + a * vs
                    e = jnp.exp(g_bot - gs)
                else:
                    e = jnp.where(rt + half >= s, jnp.exp(g_bot - gs), 0.0)
                a = jnp.sum(q_bot * e * ks, axis=-1, keepdims=True)
                i_bot = i_bot + a * vs
            intra = jnp.concatenate([i_top, i_bot], axis=0)
            st = st_ref[h]
            qt = (qb * jnp.exp(gb)).astype(BF16)
            inter = lax.dot_general(qt, st.astype(BF16), (((1,), (1,)), ((), ())),
                                    preferred_element_type=F32)
            o_ref[pl.ds(r0, hb), cs] = intra + inter
            gl = gb[hb - 1:hb]
            kt = (kb * jnp.exp(gl - gb)).astype(BF16)
            upd = jnp.dot(vb.T.astype(BF16), kt, preferred_element_type=F32)
            st_ref[h] = st * jnp.exp(gl) + upd
        return carry

    lax.fori_loop(0, lb_rows // hb, block, 0)

    for h in range(HG_HEADS):
        cs = slice(h * HG_DK, (h + 1) * HG_DK)
        oh = o_ref[:, cs]
        on = oh * lax.rsqrt(jnp.mean(oh * oh, axis=-1, keepdims=True) + NORM_EPS)
        g = g_ref[0, :, cs]
        y_ref[0, :, cs] = (on * nw_ref[:, cs] * (g * jax.nn.sigmoid(g))).astype(BF16)

    @pl.when(li == pl.num_programs(1) - 1)
    def _():
        for h in range(HG_HEADS):
            sout_ref[0, h] = st_ref[h].T


def _hgrn(proj, b_f, lb, norm_w, s0, b, l):
    assert l % HG_BLOCK == 0
    lbr = _tile(l, 128)
    p3 = proj.reshape(b, l, 4 * D_MODEL)
    r = jnp.arange(lbr)
    tri = ((r[:, None] // HG_BLOCK == r[None, :] // HG_BLOCK) & (r[None, :] <= r[:, None])).astype(BF16)
    has_state = s0 is not None
    col = lambda j: pl.BlockSpec((1, lbr, D_MODEL), lambda bi, li: (bi, li, j))
    vec = pl.BlockSpec((1, D_MODEL), lambda bi, li: (0, 0))
    in_specs = [col(0), col(1), col(2), col(3), vec, vec, vec,
                pl.BlockSpec((lbr, lbr), lambda bi, li: (0, 0))]
    args = [p3, p3, p3, p3, b_f.reshape(1, D_MODEL), lb.reshape(1, D_MODEL), norm_w.reshape(1, D_MODEL), tri]
    state_spec = pl.BlockSpec((1, HG_HEADS, HG_DK, HG_DV), lambda bi, li: (bi, 0, 0, 0))
    if has_state:
        in_specs.append(state_spec)
        args.append(s0)
    y, s = pl.pallas_call(
        functools.partial(_hgrn_kernel, lb_rows=lbr, has_state=has_state),
        out_shape=[jax.ShapeDtypeStruct((b, l, D_MODEL), BF16),
                   jax.ShapeDtypeStruct((b, HG_HEADS, HG_DK, HG_DV), F32)],
        grid=(b, l // lbr),
        in_specs=in_specs,
        out_specs=[pl.BlockSpec((1, lbr, D_MODEL), lambda bi, li: (bi, li, 0)), state_spec],
        scratch_shapes=[pltpu.VMEM((HG_HEADS, HG_DV, HG_DK), F32),
                        pltpu.VMEM((lbr, D_MODEL), F32),
                        pltpu.VMEM((lbr, D_MODEL), F32),
                        pltpu.VMEM((lbr, D_MODEL), F32)],
        compiler_params=_params("parallel", "arbitrary"),
        name="hgrn2",
    )(*args)
    return y.reshape(b * l, D_MODEL), s


def _fox_prompt_kernel(q_ref, k_ref, v_ref, cq_ref, ck_ref, o_ref, *, l, tq, tk):
    scale = FOX_HD ** -0.5
    for qi in range(l // tq):
        q0 = qi * tq
        q = q_ref[0, 0, q0:q0 + tq, :]
        cq = cq_ref[0, 0, q0:q0 + tq, :]
        m = jnp.full((tq, 1), -jnp.inf, F32)
        den = jnp.zeros((tq, 1), F32)
        acc = jnp.zeros((tq, FOX_HD), F32)
        for kj in range(l // tk):
            k0 = kj * tk
            if k0 > q0 + tq - 1:
                continue
            s = lax.dot_general(q, k_ref[0, 0, k0:k0 + tk, :], (((1,), (1,)), ((), ())),
                                preferred_element_type=F32) * scale
            s = s + (cq - ck_ref[0, 0, :, k0:k0 + tk])
            if k0 + tk - 1 > q0:
                qpos = q0 + lax.broadcasted_iota(jnp.int32, (tq, tk), 0)
                kpos = k0 + lax.broadcasted_iota(jnp.int32, (tq, tk), 1)
                s = jnp.where(kpos <= qpos, s, -jnp.inf)
            m_new = jnp.maximum(m, jnp.max(s, axis=-1, keepdims=True))
            w = jnp.exp(m - m_new)
            p = jnp.exp(s - m_new)
            den = den * w + jnp.sum(p, axis=-1, keepdims=True)
            acc = acc * w + jnp.dot(p.astype(BF16), v_ref[0, 0, k0:k0 + tk, :], preferred_element_type=F32)
            m = m_new
        o_ref[0, q0:q0 + tq, :] = (acc / den).astype(BF16)


def _fox_prompt_attend(q, k, v, csum, b, l):
    tq = _tile(l, 256)
    tk = _tile(l, 512)
    cq = csum.transpose(0, 2, 1)[..., None]
    ck = csum.transpose(0, 2, 1)[:, :, None, :]
    head = pl.BlockSpec((1, 1, l, FOX_HD), lambda bi, h: (bi, h, 0, 0))
    o = pl.pallas_call(
        functools.partial(_fox_prompt_kernel, l=l, tq=tq, tk=tk),
        out_shape=jax.ShapeDtypeStruct((b, l, D_MODEL), BF16),
        grid=(b, FOX_HEADS),
        in_specs=[head, head, head,
                  pl.BlockSpec((1, 1, l, 1), lambda bi, h: (bi, h, 0, 0)),
                  pl.BlockSpec((1, 1, 1, l), lambda bi, h: (bi, h, 0, 0))],
        out_specs=pl.BlockSpec((1, l, FOX_HD), lambda bi, h: (bi, 0, h)),
        compiler_params=_params("parallel", "parallel"),
        name="fox_prompt_attention",
    )(q, k, v, cq, ck)
    return o.reshape(b * l, D_MODEL)


def _fox_sample_kernel(q_ref, kn_ref, vn_ref, kc_ref, vc_ref, cq_ref, ckc_ref, ckn_ref, o_ref, *, l):
    scale = FOX_HD ** -0.5
    ti = lax.broadcasted_iota(jnp.int32, (l, l), 0)
    si = lax.broadcasted_iota(jnp.int32, (l, l), 1)
    for h in range(FOX_HEADS):
        cs = slice(h * FOX_HD, (h + 1) * FOX_HD)
        q = q_ref[0, :, cs].astype(BF16)
        kc = kc_ref[0, :, h, :].astype(BF16)
        vc = vc_ref[0, :, h, :].astype(BF16)
        kn = kn_ref[0, :, cs].astype(BF16)
        vn = vn_ref[0, :, cs].astype(BF16)
        cq = cq_ref[0, h]
        s1 = lax.dot_general(q, kc, (((1,), (1,)), ((), ())), preferred_element_type=F32) * scale
        s1 = s1 + (cq - ckc_ref[0, h])
        s2 = lax.dot_general(q, kn, (((1,), (1,)), ((), ())), preferred_element_type=F32) * scale
        s2 = jnp.where(si <= ti, s2 + (cq - ckn_ref[0, h]), -jnp.inf)
        m = jnp.maximum(jnp.max(s1, axis=-1, keepdims=True), jnp.max(s2, axis=-1, keepdims=True))
        p1 = jnp.exp(s1 - m)
        p2 = jnp.exp(s2 - m)
        den = jnp.sum(p1, axis=-1, keepdims=True) + jnp.sum(p2, axis=-1, keepdims=True)
        acc = (jnp.dot(p1.astype(BF16), vc, preferred_element_type=F32)
               + jnp.dot(p2.astype(BF16), vn, preferred_element_type=F32))
        o_ref[0, :, cs] = (acc / den).astype(BF16)


def _fox_sample_attend(q, kn, vn, cache_k, cache_v, csum, b, l):
    p = cache_k.shape[1]
    ct = csum.transpose(0, 2, 1)
    cq = ct[:, :, p:, None]
    ckc = ct[:, :, None, :p]
    ckn = ct[:, :, None, p:]
    tok = pl.BlockSpec((1, l, D_MODEL), lambda bi: (bi, 0, 0))
    cache = pl.BlockSpec((1, p, FOX_HEADS, FOX_HD), lambda bi: (bi, 0, 0, 0))
    o = pl.pallas_call(
        functools.partial(_fox_sample_kernel, l=l),
        out_shape=jax.ShapeDtypeStruct((b, l, D_MODEL), BF16),
        grid=(b,),
        in_specs=[tok, tok, tok, cache, cache,
                  pl.BlockSpec((1, FOX_HEADS, l, 1), lambda bi: (bi, 0, 0, 0)),
                  pl.BlockSpec((1, FOX_HEADS, 1, p), lambda bi: (bi, 0, 0, 0)),
                  pl.BlockSpec((1, FOX_HEADS, 1, l), lambda bi: (bi, 0, 0, 0))],
        out_specs=tok,
        compiler_params=_params("parallel"),
        name="fox_sample_attention",
    )(q.reshape(b, l, D_MODEL), kn.reshape(b, l, D_MODEL), vn.reshape(b, l, D_MODEL),
      cache_k, cache_v, cq, ckc, ckn)
    return o.reshape(b * l, D_MODEL)


def _layer_norm(z, g, b):
    mu = jnp.mean(z, axis=-1, keepdims=True)
    zc = z - mu
    var = jnp.mean(zc * zc, axis=-1, keepdims=True)
    return zc * lax.rsqrt(var + LN_EPS) * g + b


def _outproj_norm_kernel(y_ref, w_ref, x_ref, gate_ref, sc_ref, sh_ref, lng_ref, lnb_ref, rw_ref,
                         xn_ref, u_ref, s_ref, acc_ref):
    kk = pl.program_id(1)

    @pl.when(kk == 0)
    def _():
        acc_ref[...] = jnp.zeros_like(acc_ref)

    acc_ref[...] += jnp.dot(y_ref[...], w_ref[...], preferred_element_type=F32)

    @pl.when(kk == pl.num_programs(1) - 1)
    def _():
        z = ALPHA * x_ref[...] + (1.0 + gate_ref[0]) * acc_ref[...]
        xn = _layer_norm(z, lng_ref[...], lnb_ref[...])
        xn_ref[...] = xn
        u = xn * (1.0 + sc_ref[0]) + sh_ref[0]
        u_ref[...] = u
        s_ref[...] = jax.nn.sigmoid(jnp.dot(u.astype(BF16), rw_ref[...], preferred_element_type=F32))


def _outproj_norm(y, w, x, gate, sc, sh, ln_g, ln_b, rw, tm, bps):
    t, kdim = y.shape
    d = w.shape[1]
    tk = _tile(kdim, 1024)
    row = pl.BlockSpec((tm, d), lambda i, kk: (i, 0))
    vec = pl.BlockSpec((1, d), lambda i, kk: (0, 0))
    return pl.pallas_call(
        _outproj_norm_kernel,
        out_shape=[jax.ShapeDtypeStruct((t, d), F32), jax.ShapeDtypeStruct((t, d), F32),
                   jax.ShapeDtypeStruct((t, LANES), F32)],
        grid=(t // tm, kdim // tk),
        in_specs=[pl.BlockSpec((tm, tk), lambda i, kk: (i, kk)),
                  pl.BlockSpec((tk, d), lambda i, kk: (kk, 0)),
                  row, _mod_spec(gate, bps), _mod_spec(sc, bps), _mod_spec(sh, bps), vec, vec,
                  pl.BlockSpec((d, LANES), lambda i, kk: (0, 0))],
        out_specs=[row, row, pl.BlockSpec((tm, LANES), lambda i, kk: (i, 0))],
        scratch_shapes=[pltpu.VMEM((tm, d), F32)],
        compiler_params=_params("parallel", "arbitrary"),
        name="outproj_norm",
    )(y, w, x, gate, sc, sh, ln_g.reshape(1, d), ln_b.reshape(1, d), rw)


def _route(scores, router_b, tm_e):
    t = scores.shape[0]
    sel = scores + router_b.astype(F32)
    grp_score = lax.top_k(sel.reshape(t, N_GROUPS, EXPERTS_PER_GROUP), 2)[0].sum(-1)
    _, gidx = lax.top_k(grp_score, TOPK_GROUP)
    gmask = jnp.any(gidx[:, :, None] == jnp.arange(N_GROUPS)[None, None, :], axis=1)
    emask = jnp.repeat(gmask, EXPERTS_PER_GROUP, axis=1)
    _, eidx = lax.top_k(jnp.where(emask, sel, -jnp.inf), TOP_K)
    wts = jnp.take_along_axis(scores, eidx, axis=1)
    wts = wts / jnp.sum(wts, -1, keepdims=True)
    a = t * TOP_K
    flat_e = eidx.reshape(a).astype(jnp.int32)
    onehot = (flat_e[:, None] == jnp.arange(N_EXPERTS, dtype=jnp.int32)[None, :]).astype(jnp.int32)
    csum = jnp.cumsum(onehot, axis=0)
    rank = jnp.sum(onehot * csum, axis=1) - 1
    counts = csum[-1]
    padded = (counts + tm_e - 1) // tm_e * tm_e
    pad_end = jnp.cumsum(padded)
    pad_start = pad_end - padded
    dest = (pad_start[flat_e] + rank).astype(jnp.int32)
    n_blocks = (a + N_EXPERTS * (tm_e - 1) + tm_e - 1) // tm_e
    blk_e = jnp.minimum(jnp.searchsorted(pad_end, jnp.arange(n_blocks, dtype=jnp.int32) * tm_e, side='right'),
                        N_EXPERTS - 1).astype(jnp.int32)
    n_used = (pad_end[-1] // tm_e).astype(jnp.int32).reshape(1)
    return dest, wts, blk_e, n_used, n_blocks


def _dispatch_kernel(dest_ref, u_ref, xs_in, xs_ref, sem, *, tb):
    del xs_in
    base = pl.program_id(0) * tb

    def row_copy(t, slot):
        d = dest_ref[TOP_K * (base + t) + slot]
        return pltpu.make_async_copy(u_ref.at[pl.ds(t, 1), :], xs_ref.at[pl.ds(d, 1), :], sem)

    def issue(t, carry):
        for slot in range(TOP_K):
            row_copy(t, slot).start()
        return carry

    def drain(t, carry):
        for slot in range(TOP_K):
            row_copy(t, slot).wait()
        return carry

    lax.fori_loop(0, tb, issue, 0)
    lax.fori_loop(0, tb, drain, 0)


def _dispatch(u, dest, n_rows, tb):
    t, d = u.shape
    return pl.pallas_call(
        functools.partial(_dispatch_kernel, tb=tb),
        out_shape=jax.ShapeDtypeStruct((n_rows, d), F32),
        grid_spec=pltpu.PrefetchScalarGridSpec(
            num_scalar_prefetch=1,
            grid=(t // tb,),
            in_specs=[pl.BlockSpec((tb, d), lambda i, dest_ref: (i, 0)),
                      pl.BlockSpec(memory_space=pl.ANY)],
            out_specs=pl.BlockSpec(memory_space=pl.ANY),
            scratch_shapes=[pltpu.SemaphoreType.DMA],
        ),
        input_output_aliases={2: 0},
        compiler_params=_params("arbitrary"),
        name="moe_dispatch",
    )(dest, u, jnp.zeros((n_rows, d), F32))


def _expert_kernel(blk_e_ref, n_used_ref, xs_ref, wg_ref, wu_ref, wd_ref, ys_ref):
    del blk_e_ref
    i = pl.program_id(0)

    @pl.when(i < n_used_ref[0])
    def _():
        x = xs_ref[...].astype(BF16)
        g = jnp.dot(x, wg_ref[0], preferred_element_type=F32)
        up = jnp.dot(x, wu_ref[0], preferred_element_type=F32)
        hid = (g * jax.nn.sigmoid(g) * up).astype(BF16)
        ys_ref[...] = jnp.dot(hid, wd_ref[0], preferred_element_type=F32)

    @pl.when(i >= n_used_ref[0])
    def _():
        ys_ref[...] = jnp.zeros_like(ys_ref)


def _experts(xs, blk_e, n_used, wg, wu, wd, tm_e):
    r, d = xs.shape
    de = wg.shape[2]
    return pl.pallas_call(
        _expert_kernel,
        out_shape=jax.ShapeDtypeStruct((r, d), F32),
        grid_spec=pltpu.PrefetchScalarGridSpec(
            num_scalar_prefetch=2,
            grid=(r // tm_e,),
            in_specs=[pl.BlockSpec((tm_e, d), lambda i, be, nu: (i, 0)),
                      pl.BlockSpec((1, d, de), lambda i, be, nu: (be[i], 0, 0)),
                      pl.BlockSpec((1, d, de), lambda i, be, nu: (be[i], 0, 0)),
                      pl.BlockSpec((1, de, d), lambda i, be, nu: (be[i], 0, 0))],
            out_specs=pl.BlockSpec((tm_e, d), lambda i, be, nu: (i, 0)),
        ),
        compiler_params=_params("arbitrary"),
        name="moe_experts",
    )(blk_e, n_used, xs, wg, wu, wd)


def _combine_norm_kernel(dest_ref, ys_ref, wts_ref, x_ref, gate_ref, sc_ref, sh_ref, lng_ref, lnb_ref,
                         xn_ref, u_ref, y0_ref, y1_ref, sem, *, tb):
    base = pl.program_id(0) * tb
    bufs = (y0_ref, y1_ref)

    def row_copy(t, slot):
        d = dest_ref[TOP_K * (base + t) + slot]
        return pltpu.make_async_copy(ys_ref.at[pl.ds(d, 1), :], bufs[slot].at[pl.ds(t, 1), :], sem)

    def issue(t, carry):
        for slot in range(TOP_K):
            row_copy(t, slot).start()
        return carry

    def drain(t, carry):
        for slot in range(TOP_K):
            row_copy(t, slot).wait()
        return carry

    lax.fori_loop(0, tb, issue, 0)
    lax.fori_loop(0, tb, drain, 0)
    w = wts_ref[...]
    ffn = w[:, 0:1] * y0_ref[...] + w[:, 1:2] * y1_ref[...]
    z = ALPHA * x_ref[...] + (1.0 + gate_ref[0]) * ffn
    xn = _layer_norm(z, lng_ref[...], lnb_ref[...])
    xn_ref[...] = xn
    u_ref[...] = xn * (1.0 + sc_ref[0]) + sh_ref[0]


def _combine_norm(ys, dest, wts, x, gate, sc, sh, ln_g, ln_b, tb, bps):
    t, d = x.shape
    row = pl.BlockSpec((tb, d), lambda i, dr: (i, 0))
    vec = pl.BlockSpec((1, d), lambda i, dr: (0, 0))
    return pl.pallas_call(
        functools.partial(_combine_norm_kernel, tb=tb),
        out_shape=[jax.ShapeDtypeStruct((t, d), F32), jax.ShapeDtypeStruct((t, d), F32)],
        grid_spec=pltpu.PrefetchScalarGridSpec(
            num_scalar_prefetch=1,
            grid=(t // tb,),
            in_specs=[pl.BlockSpec(memory_space=pl.ANY),
                      pl.BlockSpec((tb, TOP_K), lambda i, dr: (i, 0)),
                      row, _mod_spec(gate, bps), _mod_spec(sc, bps), _mod_spec(sh, bps), vec, vec],
            out_specs=[row, row],
            scratch_shapes=[pltpu.VMEM((tb, d), F32), pltpu.VMEM((tb, d), F32), pltpu.SemaphoreType.DMA],
        ),
        compiler_params=_params("arbitrary"),
        name="moe_combine_norm",
    )(dest, ys, wts, x, gate, sc, sh, ln_g.reshape(1, d), ln_b.reshape(1, d))


def _moe_block(x, u, scores, router_b, wg, wu, wd, gate, sc, sh, ln_g, ln_b, tm, bps, tm_e):
    dest, wts, blk_e, n_used, n_blocks = _route(scores[:, :N_EXPERTS], router_b, tm_e)
    xs = _dispatch(u, dest, n_blocks * tm_e, tm)
    ys = _experts(xs, blk_e, n_used, wg, wu, wd, tm_e)
    tb = _tile(tm, 256) if gate.shape[1] == 1 else tm
    return _combine_norm(ys, dest, wts, x, gate, sc, sh, ln_g, ln_b, tb, bps * (tm // tb))


def _rope_tables(l, pos0):
    half = RET_DK // 2
    inv = ROPE_BASE ** (-jnp.arange(half, dtype=F32) / half)
    ang = (pos0 + jnp.arange(l)).astype(F32)[:, None] * inv[None, :]
    return jnp.cos(ang), jnp.sin(ang)


def _stream(x3, mods, wts, state_ret, state_hgrn, cache_k, cache_v, cache_logf, pos0, tm_e):
    b, l, d = x3.shape
    t = b * l
    fresh = state_ret is None
    if fresh:
        tm = _tile(l, 512)
        bps = l // tm
        expand = lambda m: m[:, None, :]
    else:
        tm = _tile(t, 512)
        bps = 1
        expand = lambda m: jnp.repeat(m, l, axis=0).reshape(t // tm, tm, d)
    x = x3.reshape(t, d)
    outs = dict(ret=[], hg=[], fk=[], fv=[], fl=[])
    log_gamma = jnp.log1p(-jnp.exp2(-5.0 - jnp.arange(RET_HEADS, dtype=F32)))
    cos, sin = _rope_tables(l, pos0)
    u = None
    for layer in range(DEPTH):
        m = [expand(a) for a in mods[layer]]
        if layer == 0:
            u = _modulate(x, m[1], m[0], tm, bps)
        kind, j = layer % N_MIXERS, layer // N_MIXERS
        if kind == 0:
            proj = _inproj(u, wts['ret_w_in'][j], tm)
            s0 = None if fresh else state_ret[j]
            y, s = _retention(proj, cos, sin, log_gamma, wts['ret_gn_w'][j], s0, b, l)
            outs['ret'].append(s)
            w_out = wts['ret_w_out'][j]
        elif kind == 1:
            proj = _inproj(u, wts['hg_w_in'][j], tm)
            s0 = None if fresh else state_hgrn[j]
            y, s = _hgrn(proj, wts['hg_b_f'][j], wts['lbs'][layer], wts['hg_norm_w'][j], s0, b, l)
            outs['hg'].append(s)
            w_out = wts['hg_w_out'][j]
        else:
            wq, wk, wv, wf, bf = wts['fox_in'][j]
            logf = _fox_gate(u, wf, bf, tm)[:, :FOX_HEADS].reshape(b, l, FOX_HEADS)
            if fresh:
                (qh,) = _fox_inproj(u, wq, b, l, tm, False, True)
                kt, kh = _fox_inproj(u, wk, b, l, tm, True, True)
                vt, vh = _fox_inproj(u, wv, b, l, tm, True, True)
                y = _fox_prompt_attend(qh, kh, vh, jnp.cumsum(logf, axis=1), b, l)
            else:
                qt = _inproj(u, wq, tm)
                kt = _inproj(u, wk, tm)
                vt = _inproj(u, wv, tm)
                csum = jnp.cumsum(jnp.concatenate([cache_logf[j].astype(F32), logf], axis=1), axis=1)
                y = _fox_sample_attend(qt, kt, vt, cache_k[j], cache_v[j], csum, b, l)
            outs['fk'].append(kt.reshape(b, l, FOX_HEADS, FOX_HD))
            outs['fv'].append(vt.reshape(b, l, FOX_HEADS, FOX_HD))
            outs['fl'].append(logf)
            w_out = wts['fox_w_out'][j]
        x, u, scores = _outproj_norm(y, w_out, x, m[2], m[4], m[3], wts['ln_mix_g'][layer],
                                     wts['ln_mix_b'][layer], wts['router_w'], tm, bps)
        nxt = [expand(a) for a in mods[min(layer + 1, DEPTH - 1)]]
        x, u = _moe_block(x, u, scores, wts['router_b'], wts['moe_w_gate'][layer], wts['moe_w_up'][layer],
                          wts['moe_w_down'][layer], m[5], nxt[1], nxt[0], wts['ln_ffn_g'][layer],
                          wts['ln_ffn_b'][layer], tm, bps, tm_e)
    return x.reshape(b, l, d), outs


def kernel(x_prompt, x_sample, state_ret, state_hgrn, cache_fox_k, cache_fox_v, cache_fox_logf, c_prompt, c_sample, ada_w, ada_b, ln_mix_g, ln_mix_b, ln_ffn_g, ln_ffn_b, ret_w_in, ret_gn_w, ret_w_out, hg_w_in, hg_b_f, hg_lower_bounds, hg_norm_w, hg_w_out, fox_w_in, fox_b_f, fox_w_out, router_w, router_b, moe_w_gate, moe_w_up, moe_w_down):
    dt = x_prompt.dtype
    d = D_MODEL
    nbp = c_prompt.shape[0]
    lbs = jnp.cumsum(jax.nn.softmax(hg_lower_bounds.astype(F32), axis=0), axis=0)
    lbs = lbs - lbs[0]
    mod_all = _modulation_all(jnp.concatenate([c_prompt, c_sample], axis=0).astype(F32), ada_w, ada_b)
    split6 = lambda m: [m[:, i * d:(i + 1) * d] for i in range(6)]
    mods_p = [split6(mod_all[layer, :nbp]) for layer in range(DEPTH)]
    mods_s = [split6(mod_all[layer, nbp:]) for layer in range(DEPTH)]
    pad = LANES - FOX_HEADS
    fox_in = [(fox_w_in[j, :, :d].astype(BF16), fox_w_in[j, :, d:2 * d].astype(BF16),
               fox_w_in[j, :, 2 * d:3 * d].astype(BF16),
               jnp.pad(fox_w_in[j, :, 3 * d:], ((0, 0), (0, pad))).astype(BF16),
               jnp.pad(fox_b_f[j].astype(F32), (0, pad)).reshape(1, LANES))
              for j in range(fox_w_in.shape[0])]
    wts = dict(
        ret_w_in=ret_w_in.astype(BF16), ret_gn_w=ret_gn_w.astype(F32), ret_w_out=ret_w_out.astype(BF16),
        hg_w_in=hg_w_in.astype(BF16), hg_b_f=hg_b_f.astype(F32), lbs=lbs, hg_norm_w=hg_norm_w.astype(F32),
        hg_w_out=hg_w_out.astype(BF16), fox_in=fox_in, fox_w_out=fox_w_out.astype(BF16),
        router_w=jnp.pad(router_w, ((0, 0), (0, LANES - N_EXPERTS))).astype(BF16), router_b=router_b,
        moe_w_gate=moe_w_gate.astype(BF16), moe_w_up=moe_w_up.astype(BF16), moe_w_down=moe_w_down.astype(BF16),
        ln_mix_g=ln_mix_g.astype(F32), ln_mix_b=ln_mix_b.astype(F32),
        ln_ffn_g=ln_ffn_g.astype(F32), ln_ffn_b=ln_ffn_b.astype(F32))
    past_len = cache_fox_k.shape[2]
    yp, op = _stream(x_prompt, mods_p, wts, None, None, None, None, None, 0, 256)
    ys, os_ = _stream(x_sample, mods_s, wts, state_ret, state_hgrn, cache_fox_k, cache_fox_v,
                      cache_fox_logf, past_len, 128)
    st = lambda xs: jnp.stack(xs).astype(dt)
    return (yp, ys, st(op['ret']), st(os_['ret']), st(op['hg']), st(os_['hg']),
            st(op['fk']), st(op['fv']), st(op['fl']), st(os_['fk']), st(os_['fv']), st(os_['fl']))
```

```python
import functools

import jax
import jax.numpy as jnp
from jax import lax
from jax.experimental import pallas as pl
from jax.experimental.pallas import tpu as pltpu

F32 = jnp.float32
BF16 = jnp.bfloat16

D_MODEL = 2048
DEPTH = 4
CHUNK = 64
N_MIXERS = 3
RET_HEADS = 8
RET_DK = D_MODEL // RET_HEADS
RET_DV = 2 * RET_DK
RET_QK = RET_HEADS * RET_DK
RET_V = RET_HEADS * RET_DV
ROPE_BASE = 10000.0
HG_DK = 128
HG_HEADS = D_MODEL // HG_DK
HG_DV = D_MODEL // HG_HEADS
HG_BLOCK = 16
FOX_HEADS = 16
FOX_HD = D_MODEL // FOX_HEADS
N_EXPERTS = 16
N_GROUPS = 4
EXPERTS_PER_GROUP = N_EXPERTS // N_GROUPS
TOPK_GROUP = 1
TOP_K = 2
D_EXPERT = D_MODEL // 2
ALPHA = (2 * DEPTH) ** 0.25
LN_EPS = 1e-5
NORM_EPS = 1e-6

LANES = 128
VMEM_LIMIT = 56 * 1024 * 1024


def _params(*sem):
    return pltpu.CompilerParams(dimension_semantics=sem, vmem_limit_bytes=VMEM_LIMIT)


def _tile(n, pref):
    t = min(n, pref)
    while n % t:
        t //= 2
    return t


def _mod_kernel(c_ref, w_ref, b_ref, o_ref):
    c = c_ref[...]
    a = (c * jax.nn.sigmoid(c)).astype(BF16)
    o_ref[0] = jnp.dot(a, w_ref[0].astype(BF16), preferred_element_type=F32) + b_ref[0]


def _modulation_all(c_all, ada_w, ada_b):
    nb = c_all.shape[0]
    depth, d, n = ada_w.shape
    tn = _tile(n, 1024)
    return pl.pallas_call(
        _mod_kernel,
        out_shape=jax.ShapeDtypeStruct((depth, nb, n), F32),
        grid=(depth, n // tn),
        in_specs=[pl.BlockSpec((nb, d), lambda l, j: (0, 0)),
                  pl.BlockSpec((1, d, tn), lambda l, j: (l, 0, j)),
                  pl.BlockSpec((1, 1, tn), lambda l, j: (l, 0, j))],
        out_specs=pl.BlockSpec((1, nb, tn), lambda l, j: (l, 0, j)),
        compiler_params=_params("parallel", "parallel"),
        name="modulation",
    )(c_all, ada_w, ada_b.reshape(depth, 1, n))


def _modulate_kernel(x_ref, sc_ref, sh_ref, u_ref):
    u_ref[...] = x_ref[...] * (1.0 + sc_ref[0]) + sh_ref[0]


def _mod_spec(mod, bps):
    return pl.BlockSpec((1,) + mod.shape[1:], lambda i, *_: (i // bps, 0, 0))


def _modulate(x, sc, sh, tm, bps):
    t, d = x.shape
    row = pl.BlockSpec((tm, d), lambda i: (i, 0))
    return pl.pallas_call(
        _modulate_kernel,
        out_shape=jax.ShapeDtypeStruct((t, d), F32),
        grid=(t // tm,),
        in_specs=[row, _mod_spec(sc, bps), _mod_spec(sh, bps)],
        out_specs=row,
        compiler_params=_params("parallel"),
        name="modulate",
    )(x, sc, sh)


def _inproj_kernel(x_ref, w_ref, o_ref, xb_ref):
    @pl.when(pl.program_id(1) == 0)
    def _():
        xb_ref[...] = x_ref[...].astype(BF16)

    o_ref[...] = jnp.dot(xb_ref[...], w_ref[...], preferred_element_type=F32)


def _inproj(u, w, tm):
    t, d = u.shape
    n = w.shape[1]
    tn = _tile(n, 1024)
    return pl.pallas_call(
        _inproj_kernel,
        out_shape=jax.ShapeDtypeStruct((t, n), F32),
        grid=(t // tm, n // tn),
        in_specs=[pl.BlockSpec((tm, d), lambda i, j: (i, 0)),
                  pl.BlockSpec((d, tn), lambda i, j: (0, j))],
        out_specs=pl.BlockSpec((tm, tn), lambda i, j: (i, j)),
        scratch_shapes=[pltpu.VMEM((tm, d), BF16)],
        compiler_params=_params("parallel", "arbitrary"),
        name="inproj",
    )(u, w)


def _fox_inproj_kernel(x_ref, w_ref, *outs, tok, heads):
    acc = jnp.dot(x_ref[...].astype(BF16), w_ref[...], preferred_element_type=F32)
    n = 0
    if tok:
        outs[n][...] = acc
        n += 1
    if heads:
        for h in range(FOX_HEADS):
            outs[n][0, h] = acc[:, h * FOX_HD:(h + 1) * FOX_HD].astype(BF16)


def _fox_inproj(u, w, b, l, tm, tok, heads):
    t, d = u.shape
    bps = l // tm
    out_shape, out_specs = [], []
    if tok:
        out_shape.append(jax.ShapeDtypeStruct((t, d), F32))
        out_specs.append(pl.BlockSpec((tm, d), lambda i: (i, 0)))
    if heads:
        out_shape.append(jax.ShapeDtypeStruct((b, FOX_HEADS, l, FOX_HD), BF16))
        out_specs.append(pl.BlockSpec((1, FOX_HEADS, tm, FOX_HD), lambda i: (i // bps, 0, i % bps, 0)))
    return pl.pallas_call(
        functools.partial(_fox_inproj_kernel, tok=tok, heads=heads),
        out_shape=out_shape,
        grid=(t // tm,),
        in_specs=[pl.BlockSpec((tm, d), lambda i: (i, 0)),
                  pl.BlockSpec((d, d), lambda i: (0, 0))],
        out_specs=out_specs,
        compiler_params=_params("parallel"),
        name="fox_inproj",
    )(u, w)


def _fox_gate_kernel(x_ref, w_ref, b_ref, o_ref):
    z = jnp.dot(x_ref[...].astype(BF16), w_ref[...], preferred_element_type=F32) + b_ref[...]
    o_ref[...] = jnp.minimum(z, 0.0) - jnp.log1p(jnp.exp(-jnp.abs(z)))


def _fox_gate(u, w, b, tm):
    t, d = u.shape
    return pl.pallas_call(
        _fox_gate_kernel,
        out_shape=jax.ShapeDtypeStruct((t, LANES), F32),
        grid=(t // tm,),
        in_specs=[pl.BlockSpec((tm, d), lambda i: (i, 0)),
                  pl.BlockSpec((d, LANES), lambda i: (0, 0)),
                  pl.BlockSpec((1, LANES), lambda i: (0, 0))],
        out_specs=pl.BlockSpec((tm, LANES), lambda i: (i, 0)),
        compiler_params=_params("parallel"),
        name="fox_gate",
    )(u, w, b)


def _retention_kernel(lg_ref, q_ref, k_ref, v_ref, g_ref, cos_ref, sin_ref, gn_ref, *rest,
                      cl, n_chunks, has_state):
    if has_state:
        s0_ref, y_ref, sout_ref, s_ref = rest
    else:
        y_ref, sout_ref, s_ref = rest
    h = pl.program_id(1)
    li = pl.program_id(2)

    @pl.when(li == 0)
    def _():
        if has_state:
            s_ref[...] = s0_ref[0, 0]
        else:
            s_ref[...] = jnp.zeros_like(s_ref)

    lg = lg_ref[h]
    half = RET_DK // 2
    ti = lax.broadcasted_iota(jnp.int32, (cl, cl), 0)
    si = lax.broadcasted_iota(jnp.int32, (cl, cl), 1)
    decay = jnp.exp(jnp.abs(ti - si).astype(F32) * lg)
    idx = lax.broadcasted_iota(jnp.int32, (cl, 1), 0).astype(F32)
    q_dec = jnp.exp((idx + 1.0) * lg)
    k_dec = jnp.exp((cl - 1.0 - idx) * lg)
    s_dec = jnp.exp(jnp.full((1, 1), cl, F32) * lg)

    def rope(x, cos, sin):
        x1, x2 = x[:, :half], x[:, half:]
        return jnp.concatenate([x1 * cos - x2 * sin, x1 * sin + x2 * cos], axis=-1)

    for c in range(n_chunks):
        rows = slice(c * cl, (c + 1) * cl)
        cos, sin = cos_ref[rows, :], sin_ref[rows, :]
        q = rope(q_ref[0, rows, :], cos, sin)
        k = rope(k_ref[0, rows, :], cos, sin) * (RET_DK ** -0.5)
        vb = v_ref[0, rows, :].astype(BF16)
        scores = lax.dot_general(q.astype(BF16), k.astype(BF16), (((1,), (1,)), ((), ())),
                                 preferred_element_type=F32) * decay
        intra = jnp.dot(scores.astype(BF16), vb, preferred_element_type=F32)
        s = s_ref[...]
        inter = jnp.dot((q * q_dec).astype(BF16), s.astype(BF16), preferred_element_type=F32)
        kd = (k * k_dec).T.astype(BF16)
        s_ref[...] = s * s_dec + jnp.dot(kd, vb, preferred_element_type=F32)
        o = intra + inter
        mu = jnp.mean(o, axis=-1, keepdims=True)
        oc = o - mu
        var = jnp.mean(oc * oc, axis=-1, keepdims=True)
        y = oc * lax.rsqrt(var + NORM_EPS) * gn_ref[...]
        g = g_ref[0, rows, :]
        y_ref[0, rows, :] = (g * jax.nn.sigmoid(g) * y).astype(BF16)

    @pl.when(li == pl.num_programs(2) - 1)
    def _():
        sout_ref[0, 0] = s_ref[...]


def _retention(proj, cos, sin, log_gamma, gn_w, s0, b, l):
    cl = min(l, CHUNK)
    lb = _tile(l, 4 * cl)
    n_chunks = lb // cl
    p3 = proj.reshape(b, l, proj.shape[1])
    nq = RET_QK // RET_DK
    nv = (2 * RET_QK) // RET_DV
    has_state = s0 is not None
    in_specs = [pl.BlockSpec(memory_space=pltpu.SMEM),
                pl.BlockSpec((1, lb, RET_DK), lambda bi, h, li: (bi, li, h)),
                pl.BlockSpec((1, lb, RET_DK), lambda bi, h, li: (bi, li, nq + h)),
                pl.BlockSpec((1, lb, RET_DV), lambda bi, h, li: (bi, li, nv + h)),
                pl.BlockSpec((1, lb, RET_DV), lambda bi, h, li: (bi, li, nv + RET_HEADS + h)),
                pl.BlockSpec((lb, RET_DK // 2), lambda bi, h, li: (li, 0)),
                pl.BlockSpec((lb, RET_DK // 2), lambda bi, h, li: (li, 0)),
                pl.BlockSpec((1, RET_DV), lambda bi, h, li: (0, h))]
    args = [log_gamma, p3, p3, p3, p3, cos, sin, gn_w.reshape(1, RET_V)]
    state_spec = pl.BlockSpec((1, 1, RET_DK, RET_DV), lambda bi, h, li: (bi, h, 0, 0))
    if has_state:
        in_specs.append(state_spec)
        args.append(s0)
    y, s = pl.pallas_call(
        functools.partial(_retention_kernel, cl=cl, n_chunks=n_chunks, has_state=has_state),
        out_shape=[jax.ShapeDtypeStruct((b, l, RET_V), BF16),
                   jax.ShapeDtypeStruct((b, RET_HEADS, RET_DK, RET_DV), F32)],
        grid=(b, RET_HEADS, l // lb),
        in_specs=in_specs,
        out_specs=[pl.BlockSpec((1, lb, RET_DV), lambda bi, h, li: (bi, li, h)), state_spec],
        scratch_shapes=[pltpu.VMEM((RET_DK, RET_DV), F32)],
        compiler_params=_params("parallel", "parallel", "arbitrary"),
        name="retention",
    )(*args)
    return y.reshape(b * l, RET_V), s


def _hgrn_kernel(q_ref, fz_ref, v_ref, g_ref, bf_ref, lb_ref, nw_ref, tri_ref, *rest, lb_rows, has_state):
    if has_state:
        s0_ref, y_ref, sout_ref, st_ref, gc_ref, k_ref, o_ref = rest
    else:
        y_ref, sout_ref, st_ref, gc_ref, k_ref, o_ref = rest
    li = pl.program_id(1)
    hb = HG_BLOCK
    half = hb // 2

    @pl.when(li == 0)
    def _():
        for h in range(HG_HEADS):
            if has_state:
                st_ref[h] = s0_ref[0, h].T
            else:
                st_ref[h] = jnp.zeros((HG_DV, HG_DK), F32)

    lbv = lb_ref[...]
    f = lbv + (1.0 - lbv) * jax.nn.sigmoid(fz_ref[0] + bf_ref[...])
    logf = jnp.log(f)
    k_ref[...] = 1.0 - f
    hi = logf.astype(BF16)
    r1 = logf - hi.astype(F32)
    mid = r1.astype(BF16)
    lo = (r1 - mid.astype(F32)).astype(BF16)
    tri = tri_ref[...]
    gc_ref[...] = (jnp.dot(tri, hi, preferred_element_type=F32)
                   + jnp.dot(tri, mid, preferred_element_type=F32)
                   + jnp.dot(tri, lo, preferred_element_type=F32))

    rt = lax.broadcasted_iota(jnp.int32, (half, 1), 0)

    def block(bi, carry):
        r0 = pl.multiple_of(bi * hb, hb)
        for h in range(HG_HEADS):
            cs = slice(h * HG_DK, (h + 1) * HG_DK)
            gb = gc_ref[pl.ds(r0, hb), cs]
            qb = q_ref[0, pl.ds(r0, hb), cs]
            kb = k_ref[pl.ds(r0, hb), cs]
            vb = v_ref[0, pl.ds(r0, hb), cs]
            q_top, q_bot = qb[:half], qb[half:]
            g_top, g_bot = gb[:half], gb[half:]
            i_top = jnp.zeros((half, HG_DV), F32)
            i_bot = jnp.zeros((half, HG_DV), F32)
            for s in range(hb):
                gs, ks, vs = gb[s:s + 1], kb[s:s + 1], vb[s:s + 1]
                if s < half:
                    e = jnp.where(rt >= s, jnp.exp(g_top - gs), 0.0)
                    a = jnp.sum(q_top * e * ks, axis=-1, keepdims=True)
                    i_top = i_top + a * vs
                    e = jnp.exp(g_bot - gs)
                else:
                    e = jnp.where(rt + half >= s, jnp.exp(g_bot - gs), 0.0)
                a = jnp.sum(q_bot * e * ks, axis=-1, keepdims=True)
                i_bot = i_bot + a * vs
            intra = jnp.concatenate([i_top, i_bot], axis=0)
            st = st_ref[h]
            qt = (qb * jnp.exp(gb)).astype(BF16)
            inter = lax.dot_general(qt, st.astype(BF16), (((1,), (1,)), ((), ())),
                                    preferred_element_type=F32)
            o_ref[pl.ds(r0, hb), cs] = intra + inter
            gl = gb[hb - 1:hb]
            kt = (kb * jnp.exp(gl - gb)).astype(BF16)
            upd = jnp.dot(vb.T.astype(BF16), kt, preferred_element_type=F32)
            st_ref[h] = st * jnp.exp(gl) + upd
        return carry

    lax.fori_loop(0, lb_rows // hb, block, 0)

    for h in range(HG_HEADS):
        cs = slice(h * HG_DK, (h + 1) * HG_DK)
        oh = o_ref[:, cs]
        on = oh * lax.rsqrt(jnp.mean(oh * oh, axis=-1, keepdims=True) + NORM_EPS)
        g = g_ref[0, :, cs]
        y_ref[0, :, cs] = (on * nw_ref[:, cs] * (g * jax.nn.sigmoid(g))).astype(BF16)

    @pl.when(li == pl.num_programs(1) - 1)
    def _():
        for h in range(HG_HEADS):
            sout_ref[0, h] = st_ref[h].T


def _hgrn(proj, b_f, lb, norm_w, s0, b, l):
    assert l % HG_BLOCK == 0
    lbr = _tile(l, 128)
    p3 = proj.reshape(b, l, 4 * D_MODEL)
    r = jnp.arange(lbr)
    tri = ((r[:, None] // HG_BLOCK == r[None, :] // HG_BLOCK) & (r[None, :] <= r[:, None])).astype(BF16)
    has_state = s0 is not None
    col = lambda j: pl.BlockSpec((1, lbr, D_MODEL), lambda bi, li: (bi, li, j))
    vec = pl.BlockSpec((1, D_MODEL), lambda bi, li: (0, 0))
    in_specs = [col(0), col(1), col(2), col(3), vec, vec, vec,
                pl.BlockSpec((lbr, lbr), lambda bi, li: (0, 0))]
    args = [p3, p3, p3, p3, b_f.reshape(1, D_MODEL), lb.reshape(1, D_MODEL), norm_w.reshape(1, D_MODEL), tri]
    state_spec = pl.BlockSpec((1, HG_HEADS, HG_DK, HG_DV), lambda bi, li: (bi, 0, 0, 0))
    if has_state:
        in_specs.append(state_spec)
        args.append(s0)
    y, s = pl.pallas_call(
        functools.partial(_hgrn_kernel, lb_rows=lbr, has_state=has_state),
        out_shape=[jax.ShapeDtypeStruct((b, l, D_MODEL), BF16),
                   jax.ShapeDtypeStruct((b, HG_HEADS, HG_DK, HG_DV), F32)],
        grid=(b, l // lbr),
        in_specs=in_specs,
        out_specs=[pl.BlockSpec((1, lbr, D_MODEL), lambda bi, li: (bi, li, 0)), state_spec],
        scratch_shapes=[pltpu.VMEM((HG_HEADS, HG_DV, HG_DK), F32),
                        pltpu.VMEM((lbr, D_MODEL), F32),
                        pltpu.VMEM((lbr, D_MODEL), F32),
                        pltpu.VMEM((lbr, D_MODEL), F32)],
        compiler_params=_params("parallel", "arbitrary"),
        name="hgrn2",
    )(*args)
    return y.reshape(b * l, D_MODEL), s


def _fox_prompt_kernel(q_ref, k_ref, v_ref, cq_ref, ck_ref, o_ref, *, l, tq, tk):
    scale = FOX_HD ** -0.5
    for qi in range(l // tq):
        q0 = qi * tq
        q = q_ref[0, 0, q0:q0 + tq, :]
        cq = cq_ref[0, 0, q0:q0 + tq, :]
        m = jnp.full((tq, 1), -jnp.inf, F32)
        den = jnp.zeros((tq, 1), F32)
        acc = jnp.zeros((tq, FOX_HD), F32)
        for kj in range(l // tk):
            k0 = kj * tk
            if k0 > q0 + tq - 1:
                continue
            s = lax.dot_general(q, k_ref[0, 0, k0:k0 + tk, :], (((1,), (1,)), ((), ())),
                                preferred_element_type=F32) * scale
            s = s + (cq - ck_ref[0, 0, :, k0:k0 + tk])
            if k0 + tk - 1 > q0:
                qpos = q0 + lax.broadcasted_iota(jnp.int32, (tq, tk), 0)
                kpos = k0 + lax.broadcasted_iota(jnp.int32, (tq, tk), 1)
                s = jnp.where(kpos <= qpos, s, -jnp.inf)
            m_new = jnp.maximum(m, jnp.max(s, axis=-1, keepdims=True))
            w = jnp.exp(m - m_new)
            p = jnp.exp(s - m_new)
            den = den * w + jnp.sum(p, axis=-1, keepdims=True)
            acc = acc * w + jnp.dot(p.astype(BF16), v_ref[0, 0, k0:k0 + tk, :], preferred_element_type=F32)
            m = m_new
        o_ref[0, q0:q0 + tq, :] = (acc / den).astype(BF16)


def _fox_prompt_attend(q, k, v, csum, b, l):
    tq = _tile(l, 256)
    tk = _tile(l, 512)
    cq = csum.transpose(0, 2, 1)[..., None]
    ck = csum.transpose(0, 2, 1)[:, :, None, :]
    head = pl.BlockSpec((1, 1, l, FOX_HD), lambda bi, h: (bi, h, 0, 0))
    o = pl.pallas_call(
        functools.partial(_fox_prompt_kernel, l=l, tq=tq, tk=tk),
        out_shape=jax.ShapeDtypeStruct((b, l, D_MODEL), BF16),
        grid=(b, FOX_HEADS),
        in_specs=[head, head, head,
                  pl.BlockSpec((1, 1, l, 1), lambda bi, h: (bi, h, 0, 0)),
                  pl.BlockSpec((1, 1, 1, l), lambda bi, h: (bi, h, 0, 0))],
        out_specs=pl.BlockSpec((1, l, FOX_HD), lambda bi, h: (bi, 0, h)),
        compiler_params=_params("parallel", "parallel"),
        name="fox_prompt_attention",
    )(q, k, v, cq, ck)
    return o.reshape(b * l, D_MODEL)


def _fox_sample_kernel(q_ref, kn_ref, vn_ref, kc_ref, vc_ref, cq_ref, ckc_ref, ckn_ref, o_ref, *, l):
    scale = FOX_HD ** -0.5
    ti = lax.broadcasted_iota(jnp.int32, (l, l), 0)
    si = lax.broadcasted_iota(jnp.int32, (l, l), 1)
    for h in range(FOX_HEADS):
        cs = slice(h * FOX_HD, (h + 1) * FOX_HD)
        q = q_ref[0, :, cs].astype(BF16)
        kc = kc_ref[0, :, h, :].astype(BF16)
        vc = vc_ref[0, :, h, :].astype(BF16)
        kn = kn_ref[0, :, cs].astype(BF16)
        vn = vn_ref[0, :, cs].astype(BF16)
        cq = cq_ref[0, h]
        s1 = lax.dot_general(q, kc, (((1,), (1,)), ((), ())), preferred_element_type=F32) * scale
        s1 = s1 + (cq - ckc_ref[0, h])
        s2 = lax.dot_general(q, kn, (((1,), (1,)), ((), ())), preferred_element_type=F32) * scale
        s2 = jnp.where(si <= ti, s2 + (cq - ckn_ref[0, h]), -jnp.inf)
        m = jnp.maximum(jnp.max(s1, axis=-1, keepdims=True), jnp.max(s2, axis=-1, keepdims=True))
        p1 = jnp.exp(s1 - m)
        p2 = jnp.exp(s2 - m)
        den = jnp.sum(p1, axis=-1, keepdims=True) + jnp.sum(p2, axis=-1, keepdims=True)
        acc = (jnp.dot(p1.astype(BF16), vc, preferred_element_type=F32)
               + jnp.dot(p2.astype(BF16), vn, preferred_element_type=F32))
        o_ref[0, :, cs] = (acc / den).astype(BF16)


def _fox_sample_attend(q, kn, vn, cache_k, cache_v, csum, b, l):
    p = cache_k.shape[1]
    ct = csum.transpose(0, 2, 1)
    cq = ct[:, :, p:, None]
    ckc = ct[:, :, None, :p]
    ckn = ct[:, :, None, p:]
    tok = pl.BlockSpec((1, l, D_MODEL), lambda bi: (bi, 0, 0))
    cache = pl.BlockSpec((1, p, FOX_HEADS, FOX_HD), lambda bi: (bi, 0, 0, 0))
    o = pl.pallas_call(
        functools.partial(_fox_sample_kernel, l=l),
        out_shape=jax.ShapeDtypeStruct((b, l, D_MODEL), BF16),
        grid=(b,),
        in_specs=[tok, tok, tok, cache, cache,
                  pl.BlockSpec((1, FOX_HEADS, l, 1), lambda bi: (bi, 0, 0, 0)),
                  pl.BlockSpec((1, FOX_HEADS, 1, p), lambda bi: (bi, 0, 0, 0)),
                  pl.BlockSpec((1, FOX_HEADS, 1, l), lambda bi: (bi, 0, 0, 0))],
        out_specs=tok,
        compiler_params=_params("parallel"),
        name="fox_sample_attention",
    )(q.reshape(b, l, D_MODEL), kn.reshape(b, l, D_MODEL), vn.reshape(b, l, D_MODEL),
      cache_k, cache_v, cq, ckc, ckn)
    return o.reshape(b * l, D_MODEL)


def _layer_norm(z, g, b):
    mu = jnp.mean(z, axis=-1, keepdims=True)
    zc = z - mu
    var = jnp.mean(zc * zc, axis=-1, keepdims=True)
    return zc * lax.rsqrt(var + LN_EPS) * g + b


def _outproj_norm_kernel(y_ref, w_ref, x_ref, gate_ref, sc_ref, sh_ref, lng_ref, lnb_ref, rw_ref,
                         xn_ref, u_ref, s_ref, acc_ref):
    kk = pl.program_id(1)

    @pl.when(kk == 0)
    def _():
        acc_ref[...] = jnp.zeros_like(acc_ref)

    acc_ref[...] += jnp.dot(y_ref[...], w_ref[...], preferred_element_type=F32)

    @pl.when(kk == pl.num_programs(1) - 1)
    def _():
        z = ALPHA * x_ref[...] + (1.0 + gate_ref[0]) * acc_ref[...]
        xn = _layer_norm(z, lng_ref[...], lnb_ref[...])
        xn_ref[...] = xn
        u = xn * (1.0 + sc_ref[0]) + sh_ref[0]
        u_ref[...] = u
        s_ref[...] = jax.nn.sigmoid(jnp.dot(u.astype(BF16), rw_ref[...], preferred_element_type=F32))


def _outproj_norm(y, w, x, gate, sc, sh, ln_g, ln_b, rw, tm, bps):
    t, kdim = y.shape
    d = w.shape[1]
    tk = _tile(kdim, 1024)
    row = pl.BlockSpec((tm, d), lambda i, kk: (i, 0))
    vec = pl.BlockSpec((1, d), lambda i, kk: (0, 0))
    return pl.pallas_call(
        _outproj_norm_kernel,
        out_shape=[jax.ShapeDtypeStruct((t, d), F32), jax.ShapeDtypeStruct((t, d), F32),
                   jax.ShapeDtypeStruct((t, LANES), F32)],
        grid=(t // tm, kdim // tk),
        in_specs=[pl.BlockSpec((tm, tk), lambda i, kk: (i, kk)),
                  pl.BlockSpec((tk, d), lambda i, kk: (kk, 0)),
                  row, _mod_spec(gate, bps), _mod_spec(sc, bps), _mod_spec(sh, bps), vec, vec,
                  pl.BlockSpec((d, LANES), lambda i, kk: (0, 0))],
        out_specs=[row, row, pl.BlockSpec((tm, LANES), lambda i, kk: (i, 0))],
        scratch_shapes=[pltpu.VMEM((tm, d), F32)],
        compiler_params=_params("parallel", "arbitrary"),
        name="outproj_norm",
    )(y, w, x, gate, sc, sh, ln_g.reshape(1, d), ln_b.reshape(1, d), rw)


def _argmax_first(vals):
    best, idx = vals[0], jnp.zeros(vals[0].shape, jnp.int32)
    for j in range(1, len(vals)):
        gt = vals[j] > best
        best = jnp.where(gt, vals[j], best)
        idx = jnp.where(gt, j, idx)
    return best, idx


def _pick(rows, idx):
    out = rows[0]
    for j in range(1, len(rows)):
        out = jnp.where(idx == j, rows[j], out)
    return out


def _route_kernel(s_ref, b_ref, tri_ref, e_ref, r_ref, w_ref, cnt_ref):
    i = pl.program_id(0)

    @pl.when(i == 0)
    def _():
        cnt_ref[...] = jnp.zeros_like(cnt_ref)

    sc = s_ref[...].T[:N_EXPERTS, :]
    sel = sc + b_ref[...]
    row = lambda a, e: a[e:e + 1, :]
    grp = []
    for g in range(N_GROUPS):
        a, b, c, d = (row(sel, g * EXPERTS_PER_GROUP + j) for j in range(EXPERTS_PER_GROUP))
        hi1, lo1, hi2, lo2 = jnp.maximum(a, b), jnp.minimum(a, b), jnp.maximum(c, d), jnp.minimum(c, d)
        grp.append(jnp.maximum(hi1, hi2) + jnp.maximum(jnp.minimum(hi1, hi2), jnp.maximum(lo1, lo2)))
    _, gidx = _argmax_first(grp)
    member = lambda a: [_pick([row(a, g * EXPERTS_PER_GROUP + j) for g in range(N_GROUPS)], gidx)
                        for j in range(EXPERTS_PER_GROUP)]
    v, c = member(sel), member(sc)
    _, j0 = _argmax_first(v)
    _, j1 = _argmax_first([jnp.where(j0 == j, -jnp.inf, v[j]) for j in range(EXPERTS_PER_GROUP)])
    c0, c1 = _pick(c, j0), _pick(c, j1)
    den = c0 + c1
    e0 = gidx * EXPERTS_PER_GROUP + j0
    e1 = gidx * EXPERTS_PER_GROUP + j1
    eio = lax.broadcasted_iota(jnp.int32, sc.shape, 0)
    oh0, oh1 = eio == e0, eio == e1
    member_f = jnp.where(oh0 | oh1, 1.0, 0.0)
    before = jnp.dot(member_f.astype(BF16), tri_ref[...], preferred_element_type=F32) + cnt_ref[...]
    r0 = jnp.sum(jnp.where(oh0, before, 0.0), axis=0, keepdims=True)
    r1 = jnp.sum(jnp.where(oh1, before, 0.0), axis=0, keepdims=True)
    cnt_ref[...] += jnp.sum(member_f, axis=1, keepdims=True)
    e_ref[0:1, :] = e0
    e_ref[1:2, :] = e1
    r_ref[0:1, :] = r0.astype(jnp.int32)
    r_ref[1:2, :] = r1.astype(jnp.int32)
    w_ref[0:1, :] = c0 / den
    w_ref[1:2, :] = c1 / den


def _route(scores, router_b, tm_e):
    t = scores.shape[0]
    tm = _tile(t, 512)
    r = jnp.arange(tm)
    tri = (r[:, None] < r[None, :]).astype(BF16)
    slot = pl.BlockSpec((TOP_K, tm), lambda i: (0, i))
    eidx, rank, wts, counts = pl.pallas_call(
        _route_kernel,
        out_shape=[jax.ShapeDtypeStruct((TOP_K, t), jnp.int32), jax.ShapeDtypeStruct((TOP_K, t), jnp.int32),
                   jax.ShapeDtypeStruct((TOP_K, t), F32), jax.ShapeDtypeStruct((N_EXPERTS, 1), F32)],
        grid=(t // tm,),
        in_specs=[pl.BlockSpec((tm, LANES), lambda i: (i, 0)),
                  pl.BlockSpec((N_EXPERTS, 1), lambda i: (0, 0)),
                  pl.BlockSpec((tm, tm), lambda i: (0, 0))],
        out_specs=[slot, slot, slot, pl.BlockSpec((N_EXPERTS, 1), lambda i: (0, 0))],
        compiler_params=_params("arbitrary"),
        name="moe_route",
    )(scores, router_b.astype(F32).reshape(N_EXPERTS, 1), tri)
    counts = counts[:, 0].astype(jnp.int32)
    padded = (counts + tm_e - 1) // tm_e * tm_e
    pad_end = jnp.cumsum(padded)
    pad_start = pad_end - padded
    onehot = eidx[:, :, None] == jnp.arange(N_EXPERTS, dtype=jnp.int32)[None, None, :]
    dest = (jnp.sum(jnp.where(onehot, pad_start[None, None, :], 0), axis=-1) + rank).reshape(TOP_K * t)
    n_blocks = (t * TOP_K + N_EXPERTS * (tm_e - 1) + tm_e - 1) // tm_e
    blk_e = jnp.minimum(jnp.searchsorted(pad_end, jnp.arange(n_blocks, dtype=jnp.int32) * tm_e, side='right'),
                        N_EXPERTS - 1).astype(jnp.int32)
    n_used = (pad_end[-1] // tm_e).astype(jnp.int32).reshape(1)
    return dest.astype(jnp.int32), wts.T, blk_e, n_used, n_blocks


def _dispatch_kernel(dest_ref, u_ref, xs_in, xs_ref, sem, *, tb, t_total):
    del xs_in
    base = pl.program_id(0) * tb

    def row_copy(t, slot):
        d = dest_ref[slot * t_total + base + t]
        return pltpu.make_async_copy(u_ref.at[pl.ds(t, 1), :], xs_ref.at[pl.ds(d, 1), :], sem)

    def issue(t, carry):
        for slot in range(TOP_K):
            row_copy(t, slot).start()
        return carry

    def drain(t, carry):
        for slot in range(TOP_K):
            row_copy(t, slot).wait()
        return carry

    lax.fori_loop(0, tb, issue, 0)
    lax.fori_loop(0, tb, drain, 0)


def _dispatch(u, dest, n_rows, tb):
    t, d = u.shape
    return pl.pallas_call(
        functools.partial(_dispatch_kernel, tb=tb, t_total=t),
        out_shape=jax.ShapeDtypeStruct((n_rows, d), F32),
        grid_spec=pltpu.PrefetchScalarGridSpec(
            num_scalar_prefetch=1,
            grid=(t // tb,),
            in_specs=[pl.BlockSpec((tb, d), lambda i, dest_ref: (i, 0)),
                      pl.BlockSpec(memory_space=pl.ANY)],
            out_specs=pl.BlockSpec(memory_space=pl.ANY),
            scratch_shapes=[pltpu.SemaphoreType.DMA],
        ),
        input_output_aliases={2: 0},
        compiler_params=_params("arbitrary"),
        name="moe_dispatch",
    )(dest, u, jnp.zeros((n_rows, d), F32))


def _expert_kernel(blk_e_ref, n_used_ref, xs_ref, wg_ref, wu_ref, wd_ref, ys_ref):
    del blk_e_ref
    i = pl.program_id(0)

    @pl.when(i < n_used_ref[0])
    def _():
        x = xs_ref[...].astype(BF16)
        g = jnp.dot(x, wg_ref[0], preferred_element_type=F32)
        up = jnp.dot(x, wu_ref[0], preferred_element_type=F32)
        hid = (g * jax.nn.sigmoid(g) * up).astype(BF16)
        ys_ref[...] = jnp.dot(hid, wd_ref[0], preferred_element_type=F32)

    @pl.when(i >= n_used_ref[0])
    def _():
        ys_ref[...] = jnp.zeros_like(ys_ref)


def _experts(xs, blk_e, n_used, wg, wu, wd, tm_e):
    r, d = xs.shape
    de = wg.shape[2]
    return pl.pallas_call(
        _expert_kernel,
        out_shape=jax.ShapeDtypeStruct((r, d), F32),
        grid_spec=pltpu.PrefetchScalarGridSpec(
            num_scalar_prefetch=2,
            grid=(r // tm_e,),
            in_specs=[pl.BlockSpec((tm_e, d), lambda i, be, nu: (i, 0)),
                      pl.BlockSpec((1, d, de), lambda i, be, nu: (be[i], 0, 0)),
                      pl.BlockSpec((1, d, de), lambda i, be, nu: (be[i], 0, 0)),
                      pl.BlockSpec((1, de, d), lambda i, be, nu: (be[i], 0, 0))],
            out_specs=pl.BlockSpec((tm_e, d), lambda i, be, nu: (i, 0)),
        ),
        compiler_params=_params("arbitrary"),
        name="moe_experts",
    )(blk_e, n_used, xs, wg, wu, wd)


def _combine_norm_kernel(dest_ref, ys_ref, wts_ref, x_ref, gate_ref, sc_ref, sh_ref, lng_ref, lnb_ref,
                         xn_ref, u_ref, y0_ref, y1_ref, sem, *, tb, t_total):
    base = pl.program_id(0) * tb
    bufs = (y0_ref, y1_ref)

    def row_copy(t, slot):
        d = dest_ref[slot * t_total + base + t]
        return pltpu.make_async_copy(ys_ref.at[pl.ds(d, 1), :], bufs[slot].at[pl.ds(t, 1), :], sem)

    def issue(t, carry):
        for slot in range(TOP_K):
            row_copy(t, slot).start()
        return carry

    def drain(t, carry):
        for slot in range(TOP_K):
            row_copy(t, slot).wait()
        return carry

    lax.fori_loop(0, tb, issue, 0)
    lax.fori_loop(0, tb, drain, 0)
    w = wts_ref[...]
    ffn = w[:, 0:1] * y0_ref[...] + w[:, 1:2] * y1_ref[...]
    z = ALPHA * x_ref[...] + (1.0 + gate_ref[0]) * ffn
    xn = _layer_norm(z, lng_ref[...], lnb_ref[...])
    xn_ref[...] = xn
    u_ref[...] = xn * (1.0 + sc_ref[0]) + sh_ref[0]


def _combine_norm(ys, dest, wts, x, gate, sc, sh, ln_g, ln_b, tb, bps):
    t, d = x.shape
    row = pl.BlockSpec((tb, d), lambda i, dr: (i, 0))
    vec = pl.BlockSpec((1, d), lambda i, dr: (0, 0))
    return pl.pallas_call(
        functools.partial(_combine_norm_kernel, tb=tb, t_total=t),
        out_shape=[jax.ShapeDtypeStruct((t, d), F32), jax.ShapeDtypeStruct((t, d), F32)],
        grid_spec=pltpu.PrefetchScalarGridSpec(
            num_scalar_prefetch=1,
            grid=(t // tb,),
            in_specs=[pl.BlockSpec(memory_space=pl.ANY),
                      pl.BlockSpec((tb, TOP_K), lambda i, dr: (i, 0)),
                      row, _mod_spec(gate, bps), _mod_spec(sc, bps), _mod_spec(sh, bps), vec, vec],
            out_specs=[row, row],
            scratch_shapes=[pltpu.VMEM((tb, d), F32), pltpu.VMEM((tb, d), F32), pltpu.SemaphoreType.DMA],
        ),
        compiler_params=_params("arbitrary"),
        name="moe_combine_norm",
    )(dest, ys, wts, x, gate, sc, sh, ln_g.reshape(1, d), ln_b.reshape(1, d))


def _moe_block(x, u, scores, router_b, wg, wu, wd, gate, sc, sh, ln_g, ln_b, tm, bps, tm_e):
    dest, wts, blk_e, n_used, n_blocks = _route(scores, router_b, tm_e)
    xs = _dispatch(u, dest, n_blocks * tm_e, tm)
    ys = _experts(xs, blk_e, n_used, wg, wu, wd, tm_e)
    tb = _tile(tm, 256) if gate.shape[1] == 1 else tm
    return _combine_norm(ys, dest, wts, x, gate, sc, sh, ln_g, ln_b, tb, bps * (tm // tb))


def _rope_tables(l, pos0):
    half = RET_DK // 2
    inv = ROPE_BASE ** (-jnp.arange(half, dtype=F32) / half)
    ang = (pos0 + jnp.arange(l)).astype(F32)[:, None] * inv[None, :]
    return jnp.cos(ang), jnp.sin(ang)


def _stream(x3, mods, wts, state_ret, state_hgrn, cache_k, cache_v, cache_logf, pos0, tm_e):
    b, l, d = x3.shape
    t = b * l
    fresh = state_ret is None
    if fresh:
        tm = _tile(l, 512)
        bps = l // tm
        expand = lambda m: m[:, None, :]
    else:
        tm = _tile(t, 512)
        bps = 1
        expand = lambda m: jnp.repeat(m, l, axis=0).reshape(t // tm, tm, d)
    x = x3.reshape(t, d)
    outs = dict(ret=[], hg=[], fk=[], fv=[], fl=[])
    log_gamma = jnp.log1p(-jnp.exp2(-5.0 - jnp.arange(RET_HEADS, dtype=F32)))
    cos, sin = _rope_tables(l, pos0)
    u = None
    for layer in range(DEPTH):
        m = [expand(a) for a in mods[layer]]
        if layer == 0:
            u = _modulate(x, m[1], m[0], tm, bps)
        kind, j = layer % N_MIXERS, layer // N_MIXERS
        if kind == 0:
            proj = _inproj(u, wts['ret_w_in'][j], tm)
            s0 = None if fresh else state_ret[j]
            y, s = _retention(proj, cos, sin, log_gamma, wts['ret_gn_w'][j], s0, b, l)
            outs['ret'].append(s)
            w_out = wts['ret_w_out'][j]
        elif kind == 1:
            proj = _inproj(u, wts['hg_w_in'][j], tm)
            s0 = None if fresh else state_hgrn[j]
            y, s = _hgrn(proj, wts['hg_b_f'][j], wts['lbs'][layer], wts['hg_norm_w'][j], s0, b, l)
            outs['hg'].append(s)
            w_out = wts['hg_w_out'][j]
        else:
            wq, wk, wv, wf, bf = wts['fox_in'][j]
            logf = _fox_gate(u, wf, bf, tm)[:, :FOX_HEADS].reshape(b, l, FOX_HEADS)
            if fresh:
                (qh,) = _fox_inproj(u, wq, b, l, tm, False, True)
                kt, kh = _fox_inproj(u, wk, b, l, tm, True, True)
                vt, vh = _fox_inproj(u, wv, b, l, tm, True, True)
                y = _fox_prompt_attend(qh, kh, vh, jnp.cumsum(logf, axis=1), b, l)
            else:
                qt = _inproj(u, wq, tm)
                kt = _inproj(u, wk, tm)
                vt = _inproj(u, wv, tm)
                csum = jnp.cumsum(jnp.concatenate([cache_logf[j].astype(F32), logf], axis=1), axis=1)
                y = _fox_sample_attend(qt, kt, vt, cache_k[j], cache_v[j], csum, b, l)
            outs['fk'].append(kt.reshape(b, l, FOX_HEADS, FOX_HD))
            outs['fv'].append(vt.reshape(b, l, FOX_HEADS, FOX_HD))
            outs['fl'].append(logf)
            w_out = wts['fox_w_out'][j]
        x, u, scores = _outproj_norm(y, w_out, x, m[2], m[4], m[3], wts['ln_mix_g'][layer],
                                     wts['ln_mix_b'][layer], wts['router_w'], tm, bps)
        nxt = [expand(a) for a in mods[min(layer + 1, DEPTH - 1)]]
        x, u = _moe_block(x, u, scores, wts['router_b'], wts['moe_w_gate'][layer], wts['moe_w_up'][layer],
                          wts['moe_w_down'][layer], m[5], nxt[1], nxt[0], wts['ln_ffn_g'][layer],
                          wts['ln_ffn_b'][layer], tm, bps, tm_e)
    return x.reshape(b, l, d), outs


def kernel(x_prompt, x_sample, state_ret, state_hgrn, cache_fox_k, cache_fox_v, cache_fox_logf, c_prompt, c_sample, ada_w, ada_b, ln_mix_g, ln_mix_b, ln_ffn_g, ln_ffn_b, ret_w_in, ret_gn_w, ret_w_out, hg_w_in, hg_b_f, hg_lower_bounds, hg_norm_w, hg_w_out, fox_w_in, fox_b_f, fox_w_out, router_w, router_b, moe_w_gate, moe_w_up, moe_w_down):
    dt = x_prompt.dtype
    d = D_MODEL
    nbp = c_prompt.shape[0]
    lbs = jnp.cumsum(jax.nn.softmax(hg_lower_bounds.astype(F32), axis=0), axis=0)
    lbs = lbs - lbs[0]
    mod_all = _modulation_all(jnp.concatenate([c_prompt, c_sample], axis=0).astype(F32), ada_w, ada_b)
    split6 = lambda m: [m[:, i * d:(i + 1) * d] for i in range(6)]
    mods_p = [split6(mod_all[layer, :nbp]) for layer in range(DEPTH)]
    mods_s = [split6(mod_all[layer, nbp:]) for layer in range(DEPTH)]
    pad = LANES - FOX_HEADS
    fox_in = [(fox_w_in[j, :, :d].astype(BF16), fox_w_in[j, :, d:2 * d].astype(BF16),
               fox_w_in[j, :, 2 * d:3 * d].astype(BF16),
               jnp.pad(fox_w_in[j, :, 3 * d:], ((0, 0), (0, pad))).astype(BF16),
               jnp.pad(fox_b_f[j].astype(F32), (0, pad)).reshape(1, LANES))
              for j in range(fox_w_in.shape[0])]
    wts = dict(
        ret_w_in=ret_w_in.astype(BF16), ret_gn_w=ret_gn_w.astype(F32), ret_w_out=ret_w_out.astype(BF16),
        hg_w_in=hg_w_in.astype(BF16), hg_b_f=hg_b_f.astype(F32), lbs=lbs, hg_norm_w=hg_norm_w.astype(F32),
        hg_w_out=hg_w_out.astype(BF16), fox_in=fox_in, fox_w_out=fox_w_out.astype(BF16),
        router_w=jnp.pad(router_w, ((0, 0), (0, LANES - N_EXPERTS))).astype(BF16), router_b=router_b,
        moe_w_gate=moe_w_gate.astype(BF16), moe_w_up=moe_w_up.astype(BF16), moe_w_down=moe_w_down.astype(BF16),
        ln_mix_g=ln_mix_g.astype(F32), ln_mix_b=ln_mix_b.astype(F32),
        ln_ffn_g=ln_ffn_g.astype(F32), ln_ffn_b=ln_ffn_b.astype(F32))
    past_len = cache_fox_k.shape[2]
    yp, op = _stream(x_prompt, mods_p, wts, None, None, None, None, None, 0, 256)
    ys, os_ = _stream(x_sample, mods_s, wts, state_ret, state_hgrn, cache_fox_k, cache_fox_v,
                      cache_fox_logf, past_len, 128)
    st = lambda xs: jnp.stack(xs).astype(dt)
    return (yp, ys, st(op['ret']), st(os_['ret']), st(op['hg']), st(os_['hg']),
            st(op['fk']), st(op['fv']), st(op['fl']), st(os_['fk']), st(os_['fv']), st(os_['fl']))
```

```python
import functools

import jax
import jax.numpy as jnp
from jax import lax
from jax.experimental import pallas as pl
from jax.experimental.pallas import tpu as pltpu

F32 = jnp.float32
BF16 = jnp.bfloat16

D_MODEL = 2048
DEPTH = 4
CHUNK = 64
N_MIXERS = 3
RET_HEADS = 8
RET_DK = D_MODEL // RET_HEADS
RET_DV = 2 * RET_DK
RET_QK = RET_HEADS * RET_DK
RET_V = RET_HEADS * RET_DV
ROPE_BASE = 10000.0
HG_DK = 128
HG_HEADS = D_MODEL // HG_DK
HG_DV = D_MODEL // HG_HEADS
HG_BLOCK = 16
FOX_HEADS = 16
FOX_HD = D_MODEL // FOX_HEADS
N_EXPERTS = 16
N_GROUPS = 4
EXPERTS_PER_GROUP = N_EXPERTS // N_GROUPS
TOPK_GROUP = 1
TOP_K = 2
D_EXPERT = D_MODEL // 2
ALPHA = (2 * DEPTH) ** 0.25
LN_EPS = 1e-5
NORM_EPS = 1e-6

LANES = 128
VMEM_LIMIT = 56 * 1024 * 1024
DMA_UNROLL = 8


def _params(*sem):
    return pltpu.CompilerParams(dimension_semantics=sem, vmem_limit_bytes=VMEM_LIMIT)


def _tile(n, pref):
    t = min(n, pref)
    while n % t:
        t //= 2
    return t


def _mod_kernel(c_ref, w_ref, b_ref, o_ref):
    c = c_ref[...]
    a = (c * jax.nn.sigmoid(c)).astype(BF16)
    o_ref[0] = jnp.dot(a, w_ref[0].astype(BF16), preferred_element_type=F32) + b_ref[0]


def _modulation_all(c_all, ada_w, ada_b):
    nb = c_all.shape[0]
    depth, d, n = ada_w.shape
    tn = _tile(n, 1024)
    return pl.pallas_call(
        _mod_kernel,
        out_shape=jax.ShapeDtypeStruct((depth, nb, n), F32),
        grid=(depth, n // tn),
        in_specs=[pl.BlockSpec((nb, d), lambda l, j: (0, 0)),
                  pl.BlockSpec((1, d, tn), lambda l, j: (l, 0, j)),
                  pl.BlockSpec((1, 1, tn), lambda l, j: (l, 0, j))],
        out_specs=pl.BlockSpec((1, nb, tn), lambda l, j: (l, 0, j)),
        compiler_params=_params("parallel", "parallel"),
        name="modulation",
    )(c_all, ada_w, ada_b.reshape(depth, 1, n))


def _modulate_kernel(x_ref, sc_ref, sh_ref, u_ref):
    u_ref[...] = x_ref[...] * (1.0 + sc_ref[0]) + sh_ref[0]


def _mod_spec(mod, bps):
    return pl.BlockSpec((1,) + mod.shape[1:], lambda i, *_: (i // bps, 0, 0))


def _modulate(x, sc, sh, tm, bps):
    t, d = x.shape
    row = pl.BlockSpec((tm, d), lambda i: (i, 0))
    return pl.pallas_call(
        _modulate_kernel,
        out_shape=jax.ShapeDtypeStruct((t, d), F32),
        grid=(t // tm,),
        in_specs=[row, _mod_spec(sc, bps), _mod_spec(sh, bps)],
        out_specs=row,
        compiler_params=_params("parallel"),
        name="modulate",
    )(x, sc, sh)


def _inproj_kernel(x_ref, w_ref, o_ref, xb_ref):
    @pl.when(pl.program_id(1) == 0)
    def _():
        xb_ref[...] = x_ref[...].astype(BF16)

    o_ref[...] = jnp.dot(xb_ref[...], w_ref[...], preferred_element_type=F32)


def _inproj(u, w):
    t, d = u.shape
    n = w.shape[1]
    tm = _tile(t, 1024)
    tn = _tile(n, 1024)
    return pl.pallas_call(
        _inproj_kernel,
        out_shape=jax.ShapeDtypeStruct((t, n), F32),
        grid=(t // tm, n // tn),
        in_specs=[pl.BlockSpec((tm, d), lambda i, j: (i, 0)),
                  pl.BlockSpec((d, tn), lambda i, j: (0, j))],
        out_specs=pl.BlockSpec((tm, tn), lambda i, j: (i, j)),
        scratch_shapes=[pltpu.VMEM((tm, d), BF16)],
        compiler_params=_params("parallel", "arbitrary"),
        name="inproj",
    )(u, w)


def _fox_inproj_kernel(x_ref, w_ref, *outs, tok, heads):
    acc = jnp.dot(x_ref[...].astype(BF16), w_ref[...], preferred_element_type=F32)
    n = 0
    if tok:
        outs[n][...] = acc
        n += 1
    if heads:
        for h in range(FOX_HEADS):
            outs[n][0, h] = acc[:, h * FOX_HD:(h + 1) * FOX_HD].astype(BF16)


def _fox_inproj(u, w, b, l, tm, tok, heads):
    t, d = u.shape
    bps = l // tm
    out_shape, out_specs = [], []
    if tok:
        out_shape.append(jax.ShapeDtypeStruct((t, d), F32))
        out_specs.append(pl.BlockSpec((tm, d), lambda i: (i, 0)))
    if heads:
        out_shape.append(jax.ShapeDtypeStruct((b, FOX_HEADS, l, FOX_HD), BF16))
        out_specs.append(pl.BlockSpec((1, FOX_HEADS, tm, FOX_HD), lambda i: (i // bps, 0, i % bps, 0)))
    return pl.pallas_call(
        functools.partial(_fox_inproj_kernel, tok=tok, heads=heads),
        out_shape=out_shape,
        grid=(t // tm,),
        in_specs=[pl.BlockSpec((tm, d), lambda i: (i, 0)),
                  pl.BlockSpec((d, d), lambda i: (0, 0))],
        out_specs=out_specs,
        compiler_params=_params("parallel"),
        name="fox_inproj",
    )(u, w)


def _split3(x):
    hi = x.astype(BF16)
    r1 = x - hi.astype(F32)
    mid = r1.astype(BF16)
    lo = (r1 - mid.astype(F32)).astype(BF16)
    return hi, mid, lo


def _tri_cumsum(tri, x):
    hi, mid, lo = _split3(x)
    return (jnp.dot(tri, hi, preferred_element_type=F32) + jnp.dot(tri, mid, preferred_element_type=F32)
            + jnp.dot(tri, lo, preferred_element_type=F32))


def _fox_gate_kernel(x_ref, w_ref, b_ref, tri_ref, o_ref, c_ref, carry_ref):
    @pl.when(pl.program_id(1) == 0)
    def _():
        carry_ref[...] = jnp.zeros_like(carry_ref)

    z = jnp.dot(x_ref[...].astype(BF16), w_ref[...], preferred_element_type=F32) + b_ref[...]
    logf = jnp.minimum(z, 0.0) - jnp.log1p(jnp.exp(-jnp.abs(z)))
    o_ref[...] = logf
    csum = _tri_cumsum(tri_ref[...], logf) + carry_ref[...]
    c_ref[...] = csum
    carry_ref[...] = csum[csum.shape[0] - 1:, :]


def _fox_gate(u, w, b, nb, l):
    t, d = u.shape
    tm = _tile(l, 256)
    bps = l // tm
    r = jnp.arange(tm)
    tri = (r[None, :] <= r[:, None]).astype(BF16)
    row = pl.BlockSpec((tm, LANES), lambda bi, li: (bi * bps + li, 0))
    return pl.pallas_call(
        _fox_gate_kernel,
        out_shape=[jax.ShapeDtypeStruct((t, LANES), F32), jax.ShapeDtypeStruct((t, LANES), F32)],
        grid=(nb, bps),
        in_specs=[pl.BlockSpec((tm, d), lambda bi, li: (bi * bps + li, 0)),
                  pl.BlockSpec((d, LANES), lambda bi, li: (0, 0)),
                  pl.BlockSpec((1, LANES), lambda bi, li: (0, 0)),
                  pl.BlockSpec((tm, tm), lambda bi, li: (0, 0))],
        out_specs=[row, row],
        scratch_shapes=[pltpu.VMEM((1, LANES), F32)],
        compiler_params=_params("parallel", "arbitrary"),
        name="fox_gate",
    )(u, w, b, tri)


def _retention_kernel(lg_ref, q_ref, k_ref, v_ref, g_ref, cos_ref, sin_ref, gn_ref, *rest,
                      cl, lb, has_state):
    if has_state:
        s0_ref, y_ref, sout_ref, s_ref = rest
    else:
        y_ref, sout_ref, s_ref = rest
    h = pl.program_id(1)
    li = pl.program_id(2)

    @pl.when(li == 0)
    def _():
        if has_state:
            s_ref[...] = s0_ref[0, 0]
        else:
            s_ref[...] = jnp.zeros_like(s_ref)

    lg = lg_ref[h]
    half = RET_DK // 2
    ti = lax.broadcasted_iota(jnp.int32, (lb, lb), 0)
    si = lax.broadcasted_iota(jnp.int32, (lb, lb), 1)
    shift = cl.bit_length() - 1
    same = jnp.right_shift(ti, shift) == jnp.right_shift(si, shift)
    dist = jnp.where(same, jnp.abs(ti - si), ti - si)
    decay = jnp.where(same | (si < ti), jnp.exp(dist.astype(F32) * lg), 0.0)
    idx = lax.broadcasted_iota(jnp.int32, (lb, 1), 0).astype(F32)
    q_dec = jnp.exp((idx + 1.0) * lg)
    k_dec = jnp.exp((lb - 1.0 - idx) * lg)
    s_dec = jnp.exp(jnp.full((1, 1), lb, F32) * lg)

    def rope(x, cos, sin):
        x1, x2 = x[:, :half], x[:, half:]
        return jnp.concatenate([x1 * cos - x2 * sin, x1 * sin + x2 * cos], axis=-1)

    cos, sin = cos_ref[...], sin_ref[...]
    q = rope(q_ref[0], cos, sin)
    k = rope(k_ref[0], cos, sin) * (RET_DK ** -0.5)
    vb = v_ref[0].astype(BF16)
    scores = lax.dot_general(q.astype(BF16), k.astype(BF16), (((1,), (1,)), ((), ())),
                             preferred_element_type=F32) * decay
    s = s_ref[...]
    o = (jnp.dot(scores.astype(BF16), vb, preferred_element_type=F32)
         + jnp.dot((q * q_dec).astype(BF16), s.astype(BF16), preferred_element_type=F32))
    kd = (k * k_dec).T.astype(BF16)
    s_ref[...] = s * s_dec + jnp.dot(kd, vb, preferred_element_type=F32)
    mu = jnp.mean(o, axis=-1, keepdims=True)
    oc = o - mu
    var = jnp.mean(oc * oc, axis=-1, keepdims=True)
    y = oc * lax.rsqrt(var + NORM_EPS) * gn_ref[...]
    g = g_ref[0]
    y_ref[0] = (g * jax.nn.sigmoid(g) * y).astype(BF16)

    @pl.when(li == pl.num_programs(2) - 1)
    def _():
        sout_ref[0, 0] = s_ref[...]


def _retention(proj, cos, sin, log_gamma, gn_w, s0, b, l):
    cl = min(l, CHUNK)
    lb = _tile(l, 4 * cl)
    p3 = proj.reshape(b, l, proj.shape[1])
    nq = RET_QK // RET_DK
    nv = (2 * RET_QK) // RET_DV
    has_state = s0 is not None
    in_specs = [pl.BlockSpec(memory_space=pltpu.SMEM),
                pl.BlockSpec((1, lb, RET_DK), lambda bi, h, li: (bi, li, h)),
                pl.BlockSpec((1, lb, RET_DK), lambda bi, h, li: (bi, li, nq + h)),
                pl.BlockSpec((1, lb, RET_DV), lambda bi, h, li: (bi, li, nv + h)),
                pl.BlockSpec((1, lb, RET_DV), lambda bi, h, li: (bi, li, nv + RET_HEADS + h)),
                pl.BlockSpec((lb, RET_DK // 2), lambda bi, h, li: (li, 0)),
                pl.BlockSpec((lb, RET_DK // 2), lambda bi, h, li: (li, 0)),
                pl.BlockSpec((1, RET_DV), lambda bi, h, li: (0, h))]
    args = [log_gamma, p3, p3, p3, p3, cos, sin, gn_w.reshape(1, RET_V)]
    state_spec = pl.BlockSpec((1, 1, RET_DK, RET_DV), lambda bi, h, li: (bi, h, 0, 0))
    if has_state:
        in_specs.append(state_spec)
        args.append(s0)
    y, s = pl.pallas_call(
        functools.partial(_retention_kernel, cl=cl, lb=lb, has_state=has_state),
        out_shape=[jax.ShapeDtypeStruct((b, l, RET_V), BF16),
                   jax.ShapeDtypeStruct((b, RET_HEADS, RET_DK, RET_DV), F32)],
        grid=(b, RET_HEADS, l // lb),
        in_specs=in_specs,
        out_specs=[pl.BlockSpec((1, lb, RET_DV), lambda bi, h, li: (bi, li, h)), state_spec],
        scratch_shapes=[pltpu.VMEM((RET_DK, RET_DV), F32)],
        compiler_params=_params("parallel", "parallel", "arbitrary"),
        name="retention",
    )(*args)
    return y.reshape(b * l, RET_V), s


def _hgrn_kernel(q_ref, fz_ref, v_ref, g_ref, bf_ref, lb_ref, nw_ref, tri_ref, *rest, lb_rows, has_state):
    if has_state:
        s0_ref, y_ref, sout_ref, st_ref, gc_ref, k_ref, o_ref = rest
    else:
        y_ref, sout_ref, st_ref, gc_ref, k_ref, o_ref = rest
    li = pl.program_id(1)
    hb = HG_BLOCK
    half = hb // 2

    @pl.when(li == 0)
    def _():
        for h in range(HG_HEADS):
            if has_state:
                st_ref[h] = s0_ref[0, h].T
            else:
                st_ref[h] = jnp.zeros((HG_DV, HG_DK), F32)

    lbv = lb_ref[...]
    f = lbv + (1.0 - lbv) * jax.nn.sigmoid(fz_ref[0] + bf_ref[...])
    logf = jnp.log(f)
    k_ref[...] = 1.0 - f
    gc_ref[...] = _tri_cumsum(tri_ref[...], logf)

    rt = lax.broadcasted_iota(jnp.int32, (half, 1), 0)

    def block(bi, carry):
        r0 = pl.multiple_of(bi * hb, hb)
        for h in range(HG_HEADS):
            cs = slice(h * HG_DK, (h + 1) * HG_DK)
            gb = gc_ref[pl.ds(r0, hb), cs]
            qb = q_ref[0, pl.ds(r0, hb), cs]
            kb = k_ref[pl.ds(r0, hb), cs]
            vb = v_ref[0, pl.ds(r0, hb), cs]
            q_top, q_bot = qb[:half], qb[half:]
            g_top, g_bot = gb[:half], gb[half:]
            i_top = jnp.zeros((half, HG_DV), F32)
            i_bot = jnp.zeros((half, HG_DV), F32)
            for s in range(hb):
                gs, ks, vs = gb[s:s + 1], kb[s:s + 1], vb[s:s + 1]
                if s < half:
                    e = jnp.where(rt >= s, jnp.exp(g_top - gs), 0.0)
                    a = jnp.sum(q_top * e * ks, axis=-1, keepdims=True)
                    i_top = i_top + a * vs
                    e = jnp.exp(g_bot - gs)
                else:
                    e = jnp.where(rt + half >= s, jnp.exp(g_bot - gs), 0.0)
                a = jnp.sum(q_bot * e * ks, axis=-1, keepdims=True)
                i_bot = i_bot + a * vs
            intra = jnp.concatenate([i_top, i_bot], axis=0)
            st = st_ref[h]
            qt = (qb * jnp.exp(gb)).astype(BF16)
            inter = lax.dot_general(qt, st.astype(BF16), (((1,), (1,)), ((), ())),
                                    preferred_element_type=F32)
            o_ref[pl.ds(r0, hb), cs] = intra + inter
            gl = gb[hb - 1:hb]
            kt = (kb * jnp.exp(gl - gb)).astype(BF16)
            upd = jnp.dot(vb.T.astype(BF16), kt, preferred_element_type=F32)
            st_ref[h] = st * jnp.exp(gl) + upd
        return carry

    lax.fori_loop(0, lb_rows // hb, block, 0)

    for h in range(HG_HEADS):
        cs = slice(h * HG_DK, (h + 1) * HG_DK)
        oh = o_ref[:, cs]
        on = oh * lax.rsqrt(jnp.mean(oh * oh, axis=-1, keepdims=True) + NORM_EPS)
        g = g_ref[0, :, cs]
        y_ref[0, :, cs] = (on * nw_ref[:, cs] * (g * jax.nn.sigmoid(g))).astype(BF16)

    @pl.when(li == pl.num_programs(1) - 1)
    def _():
        for h in range(HG_HEADS):
            sout_ref[0, h] = st_ref[h].T


def _hgrn(proj, b_f, lb, norm_w, s0, b, l):
    assert l % HG_BLOCK == 0
    lbr = _tile(l, 128)
    p3 = proj.reshape(b, l, 4 * D_MODEL)
    r = jnp.arange(lbr)
    tri = ((r[:, None] // HG_BLOCK == r[None, :] // HG_BLOCK) & (r[None, :] <= r[:, None])).astype(BF16)
    has_state = s0 is not None
    col = lambda j: pl.BlockSpec((1, lbr, D_MODEL), lambda bi, li: (bi, li, j))
    vec = pl.BlockSpec((1, D_MODEL), lambda bi, li: (0, 0))
    in_specs = [col(0), col(1), col(2), col(3), vec, vec, vec,
                pl.BlockSpec((lbr, lbr), lambda bi, li: (0, 0))]
    args = [p3, p3, p3, p3, b_f.reshape(1, D_MODEL), lb.reshape(1, D_MODEL), norm_w.reshape(1, D_MODEL), tri]
    state_spec = pl.BlockSpec((1, HG_HEADS, HG_DK, HG_DV), lambda bi, li: (bi, 0, 0, 0))
    if has_state:
        in_specs.append(state_spec)
        args.append(s0)
    y, s = pl.pallas_call(
        functools.partial(_hgrn_kernel, lb_rows=lbr, has_state=has_state),
        out_shape=[jax.ShapeDtypeStruct((b, l, D_MODEL), BF16),
                   jax.ShapeDtypeStruct((b, HG_HEADS, HG_DK, HG_DV), F32)],
        grid=(b, l // lbr),
        in_specs=in_specs,
        out_specs=[pl.BlockSpec((1, lbr, D_MODEL), lambda bi, li: (bi, li, 0)), state_spec],
        scratch_shapes=[pltpu.VMEM((HG_HEADS, HG_DV, HG_DK), F32),
                        pltpu.VMEM((lbr, D_MODEL), F32),
                        pltpu.VMEM((lbr, D_MODEL), F32),
                        pltpu.VMEM((lbr, D_MODEL), F32)],
        compiler_params=_params("parallel", "arbitrary"),
        name="hgrn2",
    )(*args)
    return y.reshape(b * l, D_MODEL), s


def _fox_prompt_kernel(q_ref, k_ref, v_ref, cq_ref, ck_ref, o_ref, *, l, tq, tk):
    scale = FOX_HD ** -0.5
    for qi in range(l // tq):
        q0 = qi * tq
        q = q_ref[0, 0, q0:q0 + tq, :]
        cq = cq_ref[0, 0, q0:q0 + tq, :]
        m = jnp.full((tq, 1), -jnp.inf, F32)
        den = jnp.zeros((tq, 1), F32)
        acc = jnp.zeros((tq, FOX_HD), F32)
        for kj in range(l // tk):
            k0 = kj * tk
            if k0 > q0 + tq - 1:
                continue
            s = lax.dot_general(q, k_ref[0, 0, k0:k0 + tk, :], (((1,), (1,)), ((), ())),
                                preferred_element_type=F32) * scale
            s = s + (cq - ck_ref[0, 0, :, k0:k0 + tk])
            if k0 + tk - 1 > q0:
                qpos = q0 + lax.broadcasted_iota(jnp.int32, (tq, tk), 0)
                kpos = k0 + lax.broadcasted_iota(jnp.int32, (tq, tk), 1)
                s = jnp.where(kpos <= qpos, s, -jnp.inf)
            m_new = jnp.maximum(m, jnp.max(s, axis=-1, keepdims=True))
            w = jnp.exp(m - m_new)
            p = jnp.exp(s - m_new)
            den = den * w + jnp.sum(p, axis=-1, keepdims=True)
            acc = acc * w + jnp.dot(p.astype(BF16), v_ref[0, 0, k0:k0 + tk, :], preferred_element_type=F32)
            m = m_new
        o_ref[0, q0:q0 + tq, :] = (acc / den).astype(BF16)


def _fox_prompt_attend(q, k, v, csum, b, l):
    tq = _tile(l, 256)
    tk = _tile(l, 512)
    cq = csum.transpose(0, 2, 1)[..., None]
    ck = csum.transpose(0, 2, 1)[:, :, None, :]
    head = pl.BlockSpec((1, 1, l, FOX_HD), lambda bi, h: (bi, h, 0, 0))
    o = pl.pallas_call(
        functools.partial(_fox_prompt_kernel, l=l, tq=tq, tk=tk),
        out_shape=jax.ShapeDtypeStruct((b, l, D_MODEL), BF16),
        grid=(b, FOX_HEADS),
        in_specs=[head, head, head,
                  pl.BlockSpec((1, 1, l, 1), lambda bi, h: (bi, h, 0, 0)),
                  pl.BlockSpec((1, 1, 1, l), lambda bi, h: (bi, h, 0, 0))],
        out_specs=pl.BlockSpec((1, l, FOX_HD), lambda bi, h: (bi, 0, h)),
        compiler_params=_params("parallel", "parallel"),
        name="fox_prompt_attention",
    )(q, k, v, cq, ck)
    return o.reshape(b * l, D_MODEL)


def _fox_sample_kernel(q_ref, kn_ref, vn_ref, kc_ref, vc_ref, cq_ref, ckc_ref, ckn_ref, o_ref, *, l):
    scale = FOX_HD ** -0.5
    ti = lax.broadcasted_iota(jnp.int32, (l, l), 0)
    si = lax.broadcasted_iota(jnp.int32, (l, l), 1)
    for h in range(FOX_HEADS):
        cs = slice(h * FOX_HD, (h + 1) * FOX_HD)
        q = q_ref[0, :, cs].astype(BF16)
        kc = kc_ref[0, :, h, :].astype(BF16)
        vc = vc_ref[0, :, h, :].astype(BF16)
        kn = kn_ref[0, :, cs].astype(BF16)
        vn = vn_ref[0, :, cs].astype(BF16)
        cq = cq_ref[0, h]
        s1 = lax.dot_general(q, kc, (((1,), (1,)), ((), ())), preferred_element_type=F32) * scale
        s1 = s1 + (cq - ckc_ref[0, h])
        s2 = lax.dot_general(q, kn, (((1,), (1,)), ((), ())), preferred_element_type=F32) * scale
        s2 = jnp.where(si <= ti, s2 + (cq - ckn_ref[0, h]), -jnp.inf)
        m = jnp.maximum(jnp.max(s1, axis=-1, keepdims=True), jnp.max(s2, axis=-1, keepdims=True))
        p1 = jnp.exp(s1 - m)
        p2 = jnp.exp(s2 - m)
        den = jnp.sum(p1, axis=-1, keepdims=True) + jnp.sum(p2, axis=-1, keepdims=True)
        acc = (jnp.dot(p1.astype(BF16), vc, preferred_element_type=F32)
               + jnp.dot(p2.astype(BF16), vn, preferred_element_type=F32))
        o_ref[0, :, cs] = (acc / den).astype(BF16)


def _fox_sample_attend(q, kn, vn, cache_k, cache_v, csum, b, l):
    p = cache_k.shape[1]
    ct = csum.transpose(0, 2, 1)
    cq = ct[:, :, p:, None]
    ckc = ct[:, :, None, :p]
    ckn = ct[:, :, None, p:]
    tok = pl.BlockSpec((1, l, D_MODEL), lambda bi: (bi, 0, 0))
    cache = pl.BlockSpec((1, p, FOX_HEADS, FOX_HD), lambda bi: (bi, 0, 0, 0))
    o = pl.pallas_call(
        functools.partial(_fox_sample_kernel, l=l),
        out_shape=jax.ShapeDtypeStruct((b, l, D_MODEL), BF16),
        grid=(b,),
        in_specs=[tok, tok, tok, cache, cache,
                  pl.BlockSpec((1, FOX_HEADS, l, 1), lambda bi: (bi, 0, 0, 0)),
                  pl.BlockSpec((1, FOX_HEADS, 1, p), lambda bi: (bi, 0, 0, 0)),
                  pl.BlockSpec((1, FOX_HEADS, 1, l), lambda bi: (bi, 0, 0, 0))],
        out_specs=tok,
        compiler_params=_params("parallel"),
        name="fox_sample_attention",
    )(q.reshape(b, l, D_MODEL), kn.reshape(b, l, D_MODEL), vn.reshape(b, l, D_MODEL),
      cache_k, cache_v, cq, ckc, ckn)
    return o.reshape(b * l, D_MODEL)


def _layer_norm(z, g, b):
    mu = jnp.mean(z, axis=-1, keepdims=True)
    zc = z - mu
    var = jnp.mean(zc * zc, axis=-1, keepdims=True)
    return zc * lax.rsqrt(var + LN_EPS) * g + b


def _outproj_norm_kernel(y_ref, w_ref, x_ref, gate_ref, sc_ref, sh_ref, lng_ref, lnb_ref, rw_ref,
                         xn_ref, u_ref, s_ref):
    out = jnp.dot(y_ref[...], w_ref[...], preferred_element_type=F32)
    z = ALPHA * x_ref[...] + (1.0 + gate_ref[0]) * out
    xn = _layer_norm(z, lng_ref[...], lnb_ref[...])
    xn_ref[...] = xn
    u = xn * (1.0 + sc_ref[0]) + sh_ref[0]
    u_ref[...] = u
    s_ref[...] = jax.nn.sigmoid(jnp.dot(u.astype(BF16), rw_ref[...], preferred_element_type=F32))


def _outproj_norm(y, w, x, gate, sc, sh, ln_g, ln_b, rw, tm, bps):
    t, kdim = y.shape
    d = w.shape[1]
    row = pl.BlockSpec((tm, d), lambda i: (i, 0))
    vec = pl.BlockSpec((1, d), lambda i: (0, 0))
    once = pl.Buffered(1)
    return pl.pallas_call(
        _outproj_norm_kernel,
        out_shape=[jax.ShapeDtypeStruct((t, d), F32), jax.ShapeDtypeStruct((t, d), F32),
                   jax.ShapeDtypeStruct((t, LANES), F32)],
        grid=(t // tm,),
        in_specs=[pl.BlockSpec((tm, kdim), lambda i: (i, 0)),
                  pl.BlockSpec((kdim, d), lambda i: (0, 0), pipeline_mode=once),
                  row, _mod_spec(gate, bps), _mod_spec(sc, bps), _mod_spec(sh, bps), vec, vec,
                  pl.BlockSpec((d, LANES), lambda i: (0, 0), pipeline_mode=once)],
        out_specs=[row, row, pl.BlockSpec((tm, LANES), lambda i: (i, 0))],
        compiler_params=_params("parallel"),
        name="outproj_norm",
    )(y, w, x, gate, sc, sh, ln_g.reshape(1, d), ln_b.reshape(1, d), rw)


def _argmax_first(vals):
    best, idx = vals[0], jnp.zeros(vals[0].shape, jnp.int32)
    for j in range(1, len(vals)):
        gt = vals[j] > best
        best = jnp.where(gt, vals[j], best)
        idx = jnp.where(gt, j, idx)
    return best, idx


def _pick(rows, idx):
    out = rows[0]
    for j in range(1, len(rows)):
        out = jnp.where(idx == j, rows[j], out)
    return out


def _route_kernel(s_ref, b_ref, tri_ref, e_ref, r_ref, w_ref, cnt_ref):
    i = pl.program_id(0)

    @pl.when(i == 0)
    def _():
        cnt_ref[...] = jnp.zeros_like(cnt_ref)

    sc = s_ref[...].T[:N_EXPERTS, :]
    sel = sc + b_ref[...]
    row = lambda a, e: a[e:e + 1, :]
    grp = []
    for g in range(N_GROUPS):
        a, b, c, d = (row(sel, g * EXPERTS_PER_GROUP + j) for j in range(EXPERTS_PER_GROUP))
        hi1, lo1, hi2, lo2 = jnp.maximum(a, b), jnp.minimum(a, b), jnp.maximum(c, d), jnp.minimum(c, d)
        grp.append(jnp.maximum(hi1, hi2) + jnp.maximum(jnp.minimum(hi1, hi2), jnp.maximum(lo1, lo2)))
    _, gidx = _argmax_first(grp)
    member = lambda a: [_pick([row(a, g * EXPERTS_PER_GROUP + j) for g in range(N_GROUPS)], gidx)
                        for j in range(EXPERTS_PER_GROUP)]
    v, c = member(sel), member(sc)
    _, j0 = _argmax_first(v)
    _, j1 = _argmax_first([jnp.where(j0 == j, -jnp.inf, v[j]) for j in range(EXPERTS_PER_GROUP)])
    c0, c1 = _pick(c, j0), _pick(c, j1)
    den = c0 + c1
    e0 = gidx * EXPERTS_PER_GROUP + j0
    e1 = gidx * EXPERTS_PER_GROUP + j1
    eio = lax.broadcasted_iota(jnp.int32, sc.shape, 0)
    oh0, oh1 = eio == e0, eio == e1
    member_f = jnp.where(oh0 | oh1, 1.0, 0.0)
    before = jnp.dot(member_f.astype(BF16), tri_ref[...], preferred_element_type=F32) + cnt_ref[...]
    r0 = jnp.sum(jnp.where(oh0, before, 0.0), axis=0, keepdims=True)
    r1 = jnp.sum(jnp.where(oh1, before, 0.0), axis=0, keepdims=True)
    cnt_ref[...] += jnp.sum(member_f, axis=1, keepdims=True)
    e_ref[0:1, :] = e0
    e_ref[1:2, :] = e1
    r_ref[0:1, :] = r0.astype(jnp.int32)
    r_ref[1:2, :] = r1.astype(jnp.int32)
    w_ref[0:1, :] = c0 / den
    w_ref[1:2, :] = c1 / den


def _route(scores, router_b, tm_e):
    t = scores.shape[0]
    tm = _tile(t, 512)
    r = jnp.arange(tm)
    tri = (r[:, None] < r[None, :]).astype(BF16)
    slot = pl.BlockSpec((TOP_K, tm), lambda i: (0, i))
    eidx, rank, wts, counts = pl.pallas_call(
        _route_kernel,
        out_shape=[jax.ShapeDtypeStruct((TOP_K, t), jnp.int32), jax.ShapeDtypeStruct((TOP_K, t), jnp.int32),
                   jax.ShapeDtypeStruct((TOP_K, t), F32), jax.ShapeDtypeStruct((N_EXPERTS, 1), F32)],
        grid=(t // tm,),
        in_specs=[pl.BlockSpec((tm, LANES), lambda i: (i, 0)),
                  pl.BlockSpec((N_EXPERTS, 1), lambda i: (0, 0)),
                  pl.BlockSpec((tm, tm), lambda i: (0, 0))],
        out_specs=[slot, slot, slot, pl.BlockSpec((N_EXPERTS, 1), lambda i: (0, 0))],
        compiler_params=_params("arbitrary"),
        name="moe_route",
    )(scores, router_b.astype(F32).reshape(N_EXPERTS, 1), tri)
    counts = counts[:, 0].astype(jnp.int32)
    padded = (counts + tm_e - 1) // tm_e * tm_e
    pad_end = jnp.cumsum(padded)
    pad_start = pad_end - padded
    onehot = eidx[:, :, None] == jnp.arange(N_EXPERTS, dtype=jnp.int32)[None, None, :]
    dest = (jnp.sum(jnp.where(onehot, pad_start[None, None, :], 0), axis=-1) + rank).reshape(TOP_K * t)
    n_blocks = (t * TOP_K + N_EXPERTS * (tm_e - 1) + tm_e - 1) // tm_e
    blk_e = jnp.minimum(jnp.searchsorted(pad_end, jnp.arange(n_blocks, dtype=jnp.int32) * tm_e, side='right'),
                        N_EXPERTS - 1).astype(jnp.int32)
    n_used = (pad_end[-1] // tm_e).astype(jnp.int32).reshape(1)
    return dest.astype(jnp.int32), wts.T, blk_e, n_used, n_blocks


def _dispatch_kernel(dest_ref, u_ref, xs_in, xs_ref, sem, *, tb, t_total):
    del xs_in
    base = pl.program_id(0) * tb

    def row_copy(t, slot):
        d = dest_ref[slot * t_total + base + t]
        return pltpu.make_async_copy(u_ref.at[pl.ds(t, 1), :], xs_ref.at[pl.ds(d, 1), :], sem)

    def issue(t, carry):
        for slot in range(TOP_K):
            row_copy(t, slot).start()
        return carry

    lax.fori_loop(0, tb, issue, 0, unroll=DMA_UNROLL)
    for slot in range(TOP_K):
        pltpu.make_async_copy(u_ref, xs_ref.at[pl.ds(0, tb), :], sem).wait()


def _dispatch(u, dest, n_rows, tb):
    t, d = u.shape
    return pl.pallas_call(
        functools.partial(_dispatch_kernel, tb=tb, t_total=t),
        out_shape=jax.ShapeDtypeStruct((n_rows, d), F32),
        grid_spec=pltpu.PrefetchScalarGridSpec(
            num_scalar_prefetch=1,
            grid=(t // tb,),
            in_specs=[pl.BlockSpec((tb, d), lambda i, dest_ref: (i, 0)),
                      pl.BlockSpec(memory_space=pl.ANY)],
            out_specs=pl.BlockSpec(memory_space=pl.ANY),
            scratch_shapes=[pltpu.SemaphoreType.DMA],
        ),
        input_output_aliases={2: 0},
        compiler_params=_params("arbitrary"),
        name="moe_dispatch",
    )(dest, u, jnp.zeros((n_rows, d), F32))


def _expert_kernel(blk_e_ref, n_used_ref, xs_ref, wg_ref, wu_ref, wd_ref, ys_ref):
    del blk_e_ref
    i = pl.program_id(0)

    @pl.when(i < n_used_ref[0])
    def _():
        x = xs_ref[...].astype(BF16)
        g = jnp.dot(x, wg_ref[0], preferred_element_type=F32)
        up = jnp.dot(x, wu_ref[0], preferred_element_type=F32)
        hid = (g * jax.nn.sigmoid(g) * up).astype(BF16)
        ys_ref[...] = jnp.dot(hid, wd_ref[0], preferred_element_type=F32)

    @pl.when(i >= n_used_ref[0])
    def _():
        ys_ref[...] = jnp.zeros_like(ys_ref)


def _experts(xs, blk_e, n_used, wg, wu, wd, tm_e):
    r, d = xs.shape
    de = wg.shape[2]
    return pl.pallas_call(
        _expert_kernel,
        out_shape=jax.ShapeDtypeStruct((r, d), F32),
        grid_spec=pltpu.PrefetchScalarGridSpec(
            num_scalar_prefetch=2,
            grid=(r // tm_e,),
            in_specs=[pl.BlockSpec((tm_e, d), lambda i, be, nu: (i, 0)),
                      pl.BlockSpec((1, d, de), lambda i, be, nu: (be[i], 0, 0)),
                      pl.BlockSpec((1, d, de), lambda i, be, nu: (be[i], 0, 0)),
                      pl.BlockSpec((1, de, d), lambda i, be, nu: (be[i], 0, 0))],
            out_specs=pl.BlockSpec((tm_e, d), lambda i, be, nu: (i, 0)),
        ),
        compiler_params=_params("arbitrary"),
        name="moe_experts",
    )(blk_e, n_used, xs, wg, wu, wd)


def _combine_norm_kernel(dest_ref, ys_ref, wts_ref, x_ref, gate_ref, sc_ref, sh_ref, lng_ref, lnb_ref,
                         xn_ref, u_ref, y0_ref, y1_ref, sem, *, tb, t_total):
    base = pl.program_id(0) * tb
    bufs = (y0_ref, y1_ref)

    def row_copy(t, slot):
        d = dest_ref[slot * t_total + base + t]
        return pltpu.make_async_copy(ys_ref.at[pl.ds(d, 1), :], bufs[slot].at[pl.ds(t, 1), :], sem)

    def issue(t, carry):
        for slot in range(TOP_K):
            row_copy(t, slot).start()
        return carry

    lax.fori_loop(0, tb, issue, 0, unroll=DMA_UNROLL)
    for buf in bufs:
        pltpu.make_async_copy(ys_ref.at[pl.ds(0, tb), :], buf, sem).wait()
    w = wts_ref[...]
    ffn = w[:, 0:1] * y0_ref[...] + w[:, 1:2] * y1_ref[...]
    z = ALPHA * x_ref[...] + (1.0 + gate_ref[0]) * ffn
    xn = _layer_norm(z, lng_ref[...], lnb_ref[...])
    xn_ref[...] = xn
    u_ref[...] = xn * (1.0 + sc_ref[0]) + sh_ref[0]


def _combine_norm(ys, dest, wts, x, gate, sc, sh, ln_g, ln_b, tb, bps):
    t, d = x.shape
    row = pl.BlockSpec((tb, d), lambda i, dr: (i, 0))
    vec = pl.BlockSpec((1, d), lambda i, dr: (0, 0))
    return pl.pallas_call(
        functools.partial(_combine_norm_kernel, tb=tb, t_total=t),
        out_shape=[jax.ShapeDtypeStruct((t, d), F32), jax.ShapeDtypeStruct((t, d), F32)],
        grid_spec=pltpu.PrefetchScalarGridSpec(
            num_scalar_prefetch=1,
            grid=(t // tb,),
            in_specs=[pl.BlockSpec(memory_space=pl.ANY),
                      pl.BlockSpec((tb, TOP_K), lambda i, dr: (i, 0)),
                      row, _mod_spec(gate, bps), _mod_spec(sc, bps), _mod_spec(sh, bps), vec, vec],
            out_specs=[row, row],
            scratch_shapes=[pltpu.VMEM((tb, d), F32), pltpu.VMEM((tb, d), F32), pltpu.SemaphoreType.DMA],
        ),
        compiler_params=_params("arbitrary"),
        name="moe_combine_norm",
    )(dest, ys, wts, x, gate, sc, sh, ln_g.reshape(1, d), ln_b.reshape(1, d))


def _moe_block(x, u, scores, router_b, wg, wu, wd, gate, sc, sh, ln_g, ln_b, tm, bps, tm_e):
    dest, wts, blk_e, n_used, n_blocks = _route(scores, router_b, tm_e)
    xs = _dispatch(u, dest, n_blocks * tm_e, tm)
    ys = _experts(xs, blk_e, n_used, wg, wu, wd, tm_e)
    tb = _tile(tm, 256) if gate.shape[1] == 1 else tm
    return _combine_norm(ys, dest, wts, x, gate, sc, sh, ln_g, ln_b, tb, bps * (tm // tb))


def _rope_tables(l, pos0):
    half = RET_DK // 2
    inv = ROPE_BASE ** (-jnp.arange(half, dtype=F32) / half)
    ang = (pos0 + jnp.arange(l)).astype(F32)[:, None] * inv[None, :]
    return jnp.cos(ang), jnp.sin(ang)


def _stream(x3, mods, wts, state_ret, state_hgrn, cache_k, cache_v, cache_logf, pos0, tm_e):
    b, l, d = x3.shape
    t = b * l
    fresh = state_ret is None
    if fresh:
        tm = _tile(l, 256)
        bps = l // tm
        expand = lambda m: m[:, None, :]
    else:
        tm = _tile(t, 256)
        bps = 1
        expand = lambda m: jnp.repeat(m, l, axis=0).reshape(t // tm, tm, d)
    x = x3.reshape(t, d)
    outs = dict(ret=[], hg=[], fk=[], fv=[], fl=[])
    log_gamma = jnp.log1p(-jnp.exp2(-5.0 - jnp.arange(RET_HEADS, dtype=F32)))
    cos, sin = _rope_tables(l, pos0)
    u = None
    for layer in range(DEPTH):
        m = [expand(a) for a in mods[layer]]
        if layer == 0:
            u = _modulate(x, m[1], m[0], tm, bps)
        kind, j = layer % N_MIXERS, layer // N_MIXERS
        if kind == 0:
            proj = _inproj(u, wts['ret_w_in'][j])
            s0 = None if fresh else state_ret[j]
            y, s = _retention(proj, cos, sin, log_gamma, wts['ret_gn_w'][j], s0, b, l)
            outs['ret'].append(s)
            w_out = wts['ret_w_out'][j]
        elif kind == 1:
            proj = _inproj(u, wts['hg_w_in'][j])
            s0 = None if fresh else state_hgrn[j]
            y, s = _hgrn(proj, wts['hg_b_f'][j], wts['lbs'][layer], wts['hg_norm_w'][j], s0, b, l)
            outs['hg'].append(s)
            w_out = wts['hg_w_out'][j]
        else:
            wq, wk, wv, wf, bf = wts['fox_in'][j]
            logf, csum = (a[:, :FOX_HEADS].reshape(b, l, FOX_HEADS) for a in _fox_gate(u, wf, bf, b, l))
            if fresh:
                tf = _tile(l, 512)
                (qh,) = _fox_inproj(u, wq, b, l, tf, False, True)
                kt, kh = _fox_inproj(u, wk, b, l, tf, True, True)
                vt, vh = _fox_inproj(u, wv, b, l, tf, True, True)
                y = _fox_prompt_attend(qh, kh, vh, csum, b, l)
            else:
                qt = _inproj(u, wq)
                kt = _inproj(u, wk)
                vt = _inproj(u, wv)
                csum = jnp.cumsum(jnp.concatenate([cache_logf[j].astype(F32), logf], axis=1), axis=1)
                y = _fox_sample_attend(qt, kt, vt, cache_k[j], cache_v[j], csum, b, l)
            outs['fk'].append(kt.reshape(b, l, FOX_HEADS, FOX_HD))
            outs['fv'].append(vt.reshape(b, l, FOX_HEADS, FOX_HD))
            outs['fl'].append(logf)
            w_out = wts['fox_w_out'][j]
        x, u, scores = _outproj_norm(y, w_out, x, m[2], m[4], m[3], wts['ln_mix_g'][layer],
                                     wts['ln_mix_b'][layer], wts['router_w'], tm, bps)
        nxt = [expand(a) for a in mods[min(layer + 1, DEPTH - 1)]]
        x, u = _moe_block(x, u, scores, wts['router_b'], wts['moe_w_gate'][layer], wts['moe_w_up'][layer],
                          wts['moe_w_down'][layer], m[5], nxt[1], nxt[0], wts['ln_ffn_g'][layer],
                          wts['ln_ffn_b'][layer], tm, bps, tm_e)
    return x.reshape(b, l, d), outs


def kernel(x_prompt, x_sample, state_ret, state_hgrn, cache_fox_k, cache_fox_v, cache_fox_logf, c_prompt, c_sample, ada_w, ada_b, ln_mix_g, ln_mix_b, ln_ffn_g, ln_ffn_b, ret_w_in, ret_gn_w, ret_w_out, hg_w_in, hg_b_f, hg_lower_bounds, hg_norm_w, hg_w_out, fox_w_in, fox_b_f, fox_w_out, router_w, router_b, moe_w_gate, moe_w_up, moe_w_down):
    dt = x_prompt.dtype
    d = D_MODEL
    nbp = c_prompt.shape[0]
    lbs = jnp.cumsum(jax.nn.softmax(hg_lower_bounds.astype(F32), axis=0), axis=0)
    lbs = lbs - lbs[0]
    mod_all = _modulation_all(jnp.concatenate([c_prompt, c_sample], axis=0).astype(F32), ada_w, ada_b)
    split6 = lambda m: [m[:, i * d:(i + 1) * d] for i in range(6)]
    mods_p = [split6(mod_all[layer, :nbp]) for layer in range(DEPTH)]
    mods_s = [split6(mod_all[layer, nbp:]) for layer in range(DEPTH)]
    pad = LANES - FOX_HEADS
    fox_in = [(fox_w_in[j, :, :d].astype(BF16), fox_w_in[j, :, d:2 * d].astype(BF16),
               fox_w_in[j, :, 2 * d:3 * d].astype(BF16),
               jnp.pad(fox_w_in[j, :, 3 * d:], ((0, 0), (0, pad))).astype(BF16),
               jnp.pad(fox_b_f[j].astype(F32), (0, pad)).reshape(1, LANES))
              for j in range(fox_w_in.shape[0])]
    wts = dict(
        ret_w_in=ret_w_in.astype(BF16), ret_gn_w=ret_gn_w.astype(F32), ret_w_out=ret_w_out.astype(BF16),
        hg_w_in=hg_w_in.astype(BF16), hg_b_f=hg_b_f.astype(F32), lbs=lbs, hg_norm_w=hg_norm_w.astype(F32),
        hg_w_out=hg_w_out.astype(BF16), fox_in=fox_in, fox_w_out=fox_w_out.astype(BF16),
        router_w=jnp.pad(router_w, ((0, 0), (0, LANES - N_EXPERTS))).astype(BF16), router_b=router_b,
        moe_w_gate=moe_w_gate.astype(BF16), moe_w_up=moe_w_up.astype(BF16), moe_w_down=moe_w_down.astype(BF16),
        ln_mix_g=ln_mix_g.astype(F32), ln_mix_b=ln_mix_b.astype(F32),
        ln_ffn_g=ln_ffn_g.astype(F32), ln_ffn_b=ln_ffn_b.astype(F32))
    past_len = cache_fox_k.shape[2]
    yp, op = _stream(x_prompt, mods_p, wts, None, None, None, None, None, 0, 256)
    ys, os_ = _stream(x_sample, mods_s, wts, state_ret, state_hgrn, cache_fox_k, cache_fox_v,
                      cache_fox_logf, past_len, 128)
    st = lambda xs: jnp.stack(xs).astype(dt)
    return (yp, ys, st(op['ret']), st(os_['ret']), st(op['hg']), st(os_['hg']),
            st(op['fk']), st(op['fv']), st(op['fl']), st(os_['fk']), st(os_['fv']), st(os_['fl']))
```

```python
import functools

import jax
import jax.numpy as jnp
from jax import lax
from jax.experimental import pallas as pl
from jax.experimental.pallas import tpu as pltpu

F32 = jnp.float32
BF16 = jnp.bfloat16

D_MODEL = 2048
DEPTH = 4
CHUNK = 64
N_MIXERS = 3
RET_HEADS = 8
RET_DK = D_MODEL // RET_HEADS
RET_DV = 2 * RET_DK
RET_QK = RET_HEADS * RET_DK
RET_V = RET_HEADS * RET_DV
ROPE_BASE = 10000.0
HG_DK = 128
HG_HEADS = D_MODEL // HG_DK
HG_DV = D_MODEL // HG_HEADS
HG_BLOCK = 16
FOX_HEADS = 16
FOX_HD = D_MODEL // FOX_HEADS
N_EXPERTS = 16
N_GROUPS = 4
EXPERTS_PER_GROUP = N_EXPERTS // N_GROUPS
TOPK_GROUP = 1
TOP_K = 2
D_EXPERT = D_MODEL // 2
ALPHA = (2 * DEPTH) ** 0.25
LN_EPS = 1e-5
NORM_EPS = 1e-6
LOG2E = 1.4426950408889634

LANES = 128
VMEM_LIMIT = 56 * 1024 * 1024
DMA_UNROLL = 8


def _params(*sem):
    return pltpu.CompilerParams(dimension_semantics=sem, vmem_limit_bytes=VMEM_LIMIT)


def _tile(n, pref):
    t = min(n, pref)
    while n % t:
        t //= 2
    return t


def _mod_kernel(c_ref, w_ref, b_ref, o_ref):
    c = c_ref[...]
    a = (c * jax.nn.sigmoid(c)).astype(BF16)
    o_ref[0] = jnp.dot(a, w_ref[0].astype(BF16), preferred_element_type=F32) + b_ref[0]


def _modulation_all(c_all, ada_w, ada_b):
    nb = c_all.shape[0]
    depth, d, n = ada_w.shape
    tn = _tile(n, 1024)
    return pl.pallas_call(
        _mod_kernel,
        out_shape=jax.ShapeDtypeStruct((depth, nb, n), F32),
        grid=(depth, n // tn),
        in_specs=[pl.BlockSpec((nb, d), lambda l, j: (0, 0)),
                  pl.BlockSpec((1, d, tn), lambda l, j: (l, 0, j)),
                  pl.BlockSpec((1, 1, tn), lambda l, j: (l, 0, j))],
        out_specs=pl.BlockSpec((1, nb, tn), lambda l, j: (l, 0, j)),
        compiler_params=_params("parallel", "parallel"),
        name="modulation",
    )(c_all, ada_w, ada_b.reshape(depth, 1, n))


def _modulate_kernel(x_ref, sc_ref, sh_ref, u_ref):
    u_ref[...] = x_ref[...] * (1.0 + sc_ref[0]) + sh_ref[0]


def _mod_spec(mod, bps):
    return pl.BlockSpec((1,) + mod.shape[1:], lambda i, *_: (i // bps, 0, 0))


def _modulate(x, sc, sh, tm, bps):
    t, d = x.shape
    row = pl.BlockSpec((tm, d), lambda i: (i, 0))
    return pl.pallas_call(
        _modulate_kernel,
        out_shape=jax.ShapeDtypeStruct((t, d), F32),
        grid=(t // tm,),
        in_specs=[row, _mod_spec(sc, bps), _mod_spec(sh, bps)],
        out_specs=row,
        compiler_params=_params("parallel"),
        name="modulate",
    )(x, sc, sh)


def _inproj_kernel(x_ref, w_ref, o_ref, xb_ref):
    @pl.when(pl.program_id(1) == 0)
    def _():
        xb_ref[...] = x_ref[...].astype(BF16)

    o_ref[...] = jnp.dot(xb_ref[...], w_ref[...], preferred_element_type=F32)


def _inproj(u, w):
    t, d = u.shape
    n = w.shape[1]
    tm = _tile(t, 1024)
    tn = _tile(n, 1024)
    return pl.pallas_call(
        _inproj_kernel,
        out_shape=jax.ShapeDtypeStruct((t, n), F32),
        grid=(t // tm, n // tn),
        in_specs=[pl.BlockSpec((tm, d), lambda i, j: (i, 0)),
                  pl.BlockSpec((d, tn), lambda i, j: (0, j))],
        out_specs=pl.BlockSpec((tm, tn), lambda i, j: (i, j)),
        scratch_shapes=[pltpu.VMEM((tm, d), BF16)],
        compiler_params=_params("parallel", "arbitrary"),
        name="inproj",
    )(u, w)


def _fox_inproj_kernel(x_ref, w_ref, *outs, tok, heads):
    acc = jnp.dot(x_ref[...].astype(BF16), w_ref[...], preferred_element_type=F32)
    n = 0
    if tok:
        outs[n][...] = acc
        n += 1
    if heads:
        for h in range(FOX_HEADS):
            outs[n][0, h] = acc[:, h * FOX_HD:(h + 1) * FOX_HD].astype(BF16)


def _fox_inproj(u, w, b, l, tm, tok, heads):
    t, d = u.shape
    bps = l // tm
    out_shape, out_specs = [], []
    if tok:
        out_shape.append(jax.ShapeDtypeStruct((t, d), F32))
        out_specs.append(pl.BlockSpec((tm, d), lambda i: (i, 0)))
    if heads:
        out_shape.append(jax.ShapeDtypeStruct((b, FOX_HEADS, l, FOX_HD), BF16))
        out_specs.append(pl.BlockSpec((1, FOX_HEADS, tm, FOX_HD), lambda i: (i // bps, 0, i % bps, 0)))
    return pl.pallas_call(
        functools.partial(_fox_inproj_kernel, tok=tok, heads=heads),
        out_shape=out_shape,
        grid=(t // tm,),
        in_specs=[pl.BlockSpec((tm, d), lambda i: (i, 0)),
                  pl.BlockSpec((d, d), lambda i: (0, 0))],
        out_specs=out_specs,
        compiler_params=_params("parallel"),
        name="fox_inproj",
    )(u, w)


def _split3(x):
    hi = x.astype(BF16)
    r1 = x - hi.astype(F32)
    mid = r1.astype(BF16)
    lo = (r1 - mid.astype(F32)).astype(BF16)
    return hi, mid, lo


def _tri_cumsum(tri, x):
    hi, mid, lo = _split3(x)
    return (jnp.dot(tri, hi, preferred_element_type=F32) + jnp.dot(tri, mid, preferred_element_type=F32)
            + jnp.dot(tri, lo, preferred_element_type=F32))


def _fox_gate_kernel(x_ref, w_ref, b_ref, tri_ref, o_ref, c_ref, carry_ref):
    @pl.when(pl.program_id(1) == 0)
    def _():
        carry_ref[...] = jnp.zeros_like(carry_ref)

    z = jnp.dot(x_ref[...].astype(BF16), w_ref[...], preferred_element_type=F32) + b_ref[...]
    logf = jnp.minimum(z, 0.0) - jnp.log1p(jnp.exp(-jnp.abs(z)))
    o_ref[...] = logf
    csum = _tri_cumsum(tri_ref[...], logf) + carry_ref[...]
    c_ref[...] = csum
    carry_ref[...] = csum[csum.shape[0] - 1:, :]


def _fox_gate(u, w, b, nb, l):
    t, d = u.shape
    tm = _tile(l, 256)
    bps = l // tm
    r = jnp.arange(tm)
    tri = (r[None, :] <= r[:, None]).astype(BF16)
    row = pl.BlockSpec((tm, LANES), lambda bi, li: (bi * bps + li, 0))
    return pl.pallas_call(
        _fox_gate_kernel,
        out_shape=[jax.ShapeDtypeStruct((t, LANES), F32), jax.ShapeDtypeStruct((t, LANES), F32)],
        grid=(nb, bps),
        in_specs=[pl.BlockSpec((tm, d), lambda bi, li: (bi * bps + li, 0)),
                  pl.BlockSpec((d, LANES), lambda bi, li: (0, 0)),
                  pl.BlockSpec((1, LANES), lambda bi, li: (0, 0)),
                  pl.BlockSpec((tm, tm), lambda bi, li: (0, 0))],
        out_specs=[row, row],
        scratch_shapes=[pltpu.VMEM((1, LANES), F32)],
        compiler_params=_params("parallel", "arbitrary"),
        name="fox_gate",
    )(u, w, b, tri)


def _retention_kernel(lg_ref, q_ref, k_ref, v_ref, g_ref, cos_ref, sin_ref, gn_ref, *rest,
                      cl, lb, has_state):
    if has_state:
        s0_ref, y_ref, sout_ref, s_ref = rest
    else:
        y_ref, sout_ref, s_ref = rest
    h = pl.program_id(1)
    li = pl.program_id(2)

    @pl.when(li == 0)
    def _():
        if has_state:
            s_ref[...] = s0_ref[0, 0]
        else:
            s_ref[...] = jnp.zeros_like(s_ref)

    lg = lg_ref[h]
    half = RET_DK // 2
    ti = lax.broadcasted_iota(jnp.int32, (lb, lb), 0)
    si = lax.broadcasted_iota(jnp.int32, (lb, lb), 1)
    shift = cl.bit_length() - 1
    same = jnp.right_shift(ti, shift) == jnp.right_shift(si, shift)
    dist = jnp.where(same, jnp.abs(ti - si), ti - si)
    decay = jnp.where(same | (si < ti), jnp.exp(dist.astype(F32) * lg), 0.0)
    idx = lax.broadcasted_iota(jnp.int32, (lb, 1), 0).astype(F32)
    q_dec = jnp.exp((idx + 1.0) * lg)
    k_dec = jnp.exp((lb - 1.0 - idx) * lg)
    s_dec = jnp.exp(jnp.full((1, 1), lb, F32) * lg)

    def rope(x, cos, sin):
        x1, x2 = x[:, :half], x[:, half:]
        return jnp.concatenate([x1 * cos - x2 * sin, x1 * sin + x2 * cos], axis=-1)

    cos, sin = cos_ref[...], sin_ref[...]
    q = rope(q_ref[0], cos, sin)
    k = rope(k_ref[0], cos, sin) * (RET_DK ** -0.5)
    vb = v_ref[0].astype(BF16)
    scores = lax.dot_general(q.astype(BF16), k.astype(BF16), (((1,), (1,)), ((), ())),
                             preferred_element_type=F32) * decay
    s = s_ref[...]
    o = (jnp.dot(scores.astype(BF16), vb, preferred_element_type=F32)
         + jnp.dot((q * q_dec).astype(BF16), s.astype(BF16), preferred_element_type=F32))
    kd = (k * k_dec).T.astype(BF16)
    s_ref[...] = s * s_dec + jnp.dot(kd, vb, preferred_element_type=F32)
    mu = jnp.mean(o, axis=-1, keepdims=True)
    oc = o - mu
    var = jnp.mean(oc * oc, axis=-1, keepdims=True)
    y = oc * lax.rsqrt(var + NORM_EPS) * gn_ref[...]
    g = g_ref[0]
    y_ref[0] = (g * jax.nn.sigmoid(g) * y).astype(BF16)

    @pl.when(li == pl.num_programs(2) - 1)
    def _():
        sout_ref[0, 0] = s_ref[...]


def _retention(proj, cos, sin, log_gamma, gn_w, s0, b, l):
    cl = min(l, CHUNK)
    lb = _tile(l, 4 * cl)
    p3 = proj.reshape(b, l, proj.shape[1])
    nq = RET_QK // RET_DK
    nv = (2 * RET_QK) // RET_DV
    has_state = s0 is not None
    in_specs = [pl.BlockSpec(memory_space=pltpu.SMEM),
                pl.BlockSpec((1, lb, RET_DK), lambda bi, h, li: (bi, li, h)),
                pl.BlockSpec((1, lb, RET_DK), lambda bi, h, li: (bi, li, nq + h)),
                pl.BlockSpec((1, lb, RET_DV), lambda bi, h, li: (bi, li, nv + h)),
                pl.BlockSpec((1, lb, RET_DV), lambda bi, h, li: (bi, li, nv + RET_HEADS + h)),
                pl.BlockSpec((lb, RET_DK // 2), lambda bi, h, li: (li, 0)),
                pl.BlockSpec((lb, RET_DK // 2), lambda bi, h, li: (li, 0)),
                pl.BlockSpec((1, RET_DV), lambda bi, h, li: (0, h))]
    args = [log_gamma, p3, p3, p3, p3, cos, sin, gn_w.reshape(1, RET_V)]
    state_spec = pl.BlockSpec((1, 1, RET_DK, RET_DV), lambda bi, h, li: (bi, h, 0, 0))
    if has_state:
        in_specs.append(state_spec)
        args.append(s0)
    y, s = pl.pallas_call(
        functools.partial(_retention_kernel, cl=cl, lb=lb, has_state=has_state),
        out_shape=[jax.ShapeDtypeStruct((b, l, RET_V), BF16),
                   jax.ShapeDtypeStruct((b, RET_HEADS, RET_DK, RET_DV), F32)],
        grid=(b, RET_HEADS, l // lb),
        in_specs=in_specs,
        out_specs=[pl.BlockSpec((1, lb, RET_DV), lambda bi, h, li: (bi, li, h)), state_spec],
        scratch_shapes=[pltpu.VMEM((RET_DK, RET_DV), F32)],
        compiler_params=_params("parallel", "parallel", "arbitrary"),
        name="retention",
    )(*args)
    return y.reshape(b * l, RET_V), s


def _hgrn_kernel(q_ref, fz_ref, v_ref, g_ref, bf_ref, lb_ref, nw_ref, tri_ref, *rest, lb_rows, has_state):
    if has_state:
        s0_ref, y_ref, sout_ref, st_ref, gc_ref, k_ref, o_ref = rest
    else:
        y_ref, sout_ref, st_ref, gc_ref, k_ref, o_ref = rest
    li = pl.program_id(1)
    hb = HG_BLOCK
    half = hb // 2

    @pl.when(li == 0)
    def _():
        for h in range(HG_HEADS):
            if has_state:
                st_ref[h] = s0_ref[0, h].T
            else:
                st_ref[h] = jnp.zeros((HG_DV, HG_DK), F32)

    lbv = lb_ref[...]
    f = lbv + (1.0 - lbv) * jax.nn.sigmoid(fz_ref[0] + bf_ref[...])
    logf = jnp.log(f)
    k_ref[...] = 1.0 - f
    gc_ref[...] = _tri_cumsum(tri_ref[...], logf) * LOG2E

    rt = lax.broadcasted_iota(jnp.int32, (half, 1), 0)
    lower = lax.broadcasted_iota(jnp.int32, (hb, 1), 0) >= half
    contract_last = (((1,), (1,)), ((), ()))

    def block(bi, carry):
        r0 = pl.multiple_of(bi * hb, hb)
        for h in range(HG_HEADS):
            cs = slice(h * HG_DK, (h + 1) * HG_DK)
            gb = gc_ref[pl.ds(r0, hb), cs]
            qb = q_ref[0, pl.ds(r0, hb), cs]
            kb = k_ref[pl.ds(r0, hb), cs]
            vb = v_ref[0, pl.ds(r0, hb), cs]
            vb16 = vb.astype(BF16)
            gm = gb[half - 1:half]
            q_lo = jnp.where(lower, qb * jnp.exp2(gb - gm), 0.0).astype(BF16)
            k_up = jnp.where(lower, 0.0, kb * jnp.exp2(gm - gb)).astype(BF16)
            a_cross = lax.dot_general(q_lo, k_up, contract_last, preferred_element_type=F32)
            intra = jnp.dot(a_cross.astype(BF16), vb16, preferred_element_type=F32)
            parts = []
            for p in range(2):
                rows = slice(p * half, (p + 1) * half)
                q_h, g_h = qb[rows], gb[rows]
                acc = jnp.zeros((half, HG_DV), F32)
                for s in range(p * half, (p + 1) * half):
                    gs, ks, vs = gb[s:s + 1], kb[s:s + 1], vb[s:s + 1]
                    e = jnp.where(rt + p * half >= s, jnp.exp2(g_h - gs), 0.0)
                    a = jnp.sum(q_h * e * ks, axis=-1, keepdims=True)
                    acc = acc + a * vs
                parts.append(acc)
            intra = intra + jnp.concatenate(parts, axis=0)
            st = st_ref[h]
            qt = (qb * jnp.exp2(gb)).astype(BF16)
            inter = lax.dot_general(qt, st.astype(BF16), contract_last, preferred_element_type=F32)
            o_ref[pl.ds(r0, hb), cs] = intra + inter
            gl = gb[hb - 1:hb]
            kt = (kb * jnp.exp2(gl - gb)).astype(BF16)
            upd = jnp.dot(vb.T.astype(BF16), kt, preferred_element_type=F32)
            st_ref[h] = st * jnp.exp2(gl) + upd
        return carry

    lax.fori_loop(0, lb_rows // hb, block, 0)

    for h in range(HG_HEADS):
        cs = slice(h * HG_DK, (h + 1) * HG_DK)
        oh = o_ref[:, cs]
        on = oh * lax.rsqrt(jnp.mean(oh * oh, axis=-1, keepdims=True) + NORM_EPS)
        g = g_ref[0, :, cs]
        y_ref[0, :, cs] = (on * nw_ref[:, cs] * (g * jax.nn.sigmoid(g))).astype(BF16)

    @pl.when(li == pl.num_programs(1) - 1)
    def _():
        for h in range(HG_HEADS):
            sout_ref[0, h] = st_ref[h].T


def _hgrn(proj, b_f, lb, norm_w, s0, b, l):
    assert l % HG_BLOCK == 0
    lbr = _tile(l, 128)
    p3 = proj.reshape(b, l, 4 * D_MODEL)
    r = jnp.arange(lbr)
    tri = ((r[:, None] // HG_BLOCK == r[None, :] // HG_BLOCK) & (r[None, :] <= r[:, None])).astype(BF16)
    has_state = s0 is not None
    col = lambda j: pl.BlockSpec((1, lbr, D_MODEL), lambda bi, li: (bi, li, j))
    vec = pl.BlockSpec((1, D_MODEL), lambda bi, li: (0, 0))
    in_specs = [col(0), col(1), col(2), col(3), vec, vec, vec,
                pl.BlockSpec((lbr, lbr), lambda bi, li: (0, 0))]
    args = [p3, p3, p3, p3, b_f.reshape(1, D_MODEL), lb.reshape(1, D_MODEL), norm_w.reshape(1, D_MODEL), tri]
    state_spec = pl.BlockSpec((1, HG_HEADS, HG_DK, HG_DV), lambda bi, li: (bi, 0, 0, 0))
    if has_state:
        in_specs.append(state_spec)
        args.append(s0)
    y, s = pl.pallas_call(
        functools.partial(_hgrn_kernel, lb_rows=lbr, has_state=has_state),
        out_shape=[jax.ShapeDtypeStruct((b, l, D_MODEL), BF16),
                   jax.ShapeDtypeStruct((b, HG_HEADS, HG_DK, HG_DV), F32)],
        grid=(b, l // lbr),
        in_specs=in_specs,
        out_specs=[pl.BlockSpec((1, lbr, D_MODEL), lambda bi, li: (bi, li, 0)), state_spec],
        scratch_shapes=[pltpu.VMEM((HG_HEADS, HG_DV, HG_DK), F32),
                        pltpu.VMEM((lbr, D_MODEL), F32),
                        pltpu.VMEM((lbr, D_MODEL), F32),
                        pltpu.VMEM((lbr, D_MODEL), F32)],
        compiler_params=_params("parallel", "arbitrary"),
        name="hgrn2",
    )(*args)
    return y.reshape(b * l, D_MODEL), s


def _fox_prompt_kernel(q_ref, k_ref, v_ref, cq_ref, ck_ref, o_ref, *, l, tq, tk):
    scale = FOX_HD ** -0.5
    for qi in range(l // tq):
        q0 = qi * tq
        q = q_ref[0, 0, q0:q0 + tq, :]
        cq = cq_ref[0, 0, q0:q0 + tq, :]
        m = jnp.full((tq, 1), -jnp.inf, F32)
        den = jnp.zeros((tq, 1), F32)
        acc = jnp.zeros((tq, FOX_HD), F32)
        for kj in range(l // tk):
            k0 = kj * tk
            if k0 > q0 + tq - 1:
                continue
            s = lax.dot_general(q, k_ref[0, 0, k0:k0 + tk, :], (((1,), (1,)), ((), ())),
                                preferred_element_type=F32) * scale
            s = s + (cq - ck_ref[0, 0, :, k0:k0 + tk])
            if k0 + tk - 1 > q0:
                qpos = q0 + lax.broadcasted_iota(jnp.int32, (tq, tk), 0)
                kpos = k0 + lax.broadcasted_iota(jnp.int32, (tq, tk), 1)
                s = jnp.where(kpos <= qpos, s, -jnp.inf)
            m_new = jnp.maximum(m, jnp.max(s, axis=-1, keepdims=True))
            w = jnp.exp(m - m_new)
            p = jnp.exp(s - m_new)
            den = den * w + jnp.sum(p, axis=-1, keepdims=True)
            acc = acc * w + jnp.dot(p.astype(BF16), v_ref[0, 0, k0:k0 + tk, :], preferred_element_type=F32)
            m = m_new
        o_ref[0, q0:q0 + tq, :] = (acc / den).astype(BF16)


def _fox_prompt_attend(q, k, v, csum, b, l):
    tq = _tile(l, 256)
    tk = _tile(l, 512)
    cq = csum.transpose(0, 2, 1)[..., None]
    ck = csum.transpose(0, 2, 1)[:, :, None, :]
    head = pl.BlockSpec((1, 1, l, FOX_HD), lambda bi, h: (bi, h, 0, 0))
    o = pl.pallas_call(
        functools.partial(_fox_prompt_kernel, l=l, tq=tq, tk=tk),
        out_shape=jax.ShapeDtypeStruct((b, l, D_MODEL), BF16),
        grid=(b, FOX_HEADS),
        in_specs=[head, head, head,
                  pl.BlockSpec((1, 1, l, 1), lambda bi, h: (bi, h, 0, 0)),
                  pl.BlockSpec((1, 1, 1, l), lambda bi, h: (bi, h, 0, 0))],
        out_specs=pl.BlockSpec((1, l, FOX_HD), lambda bi, h: (bi, 0, h)),
        compiler_params=_params("parallel", "parallel"),
        name="fox_prompt_attention",
    )(q, k, v, cq, ck)
    return o.reshape(b * l, D_MODEL)


def _fox_sample_kernel(q_ref, kn_ref, vn_ref, kc_ref, vc_ref, cq_ref, ckc_ref, ckn_ref, o_ref, *, l):
    scale = FOX_HD ** -0.5
    ti = lax.broadcasted_iota(jnp.int32, (l, l), 0)
    si = lax.broadcasted_iota(jnp.int32, (l, l), 1)
    for h in range(FOX_HEADS):
        cs = slice(h * FOX_HD, (h + 1) * FOX_HD)
        q = q_ref[0, :, cs].astype(BF16)
        kc = kc_ref[0, :, h, :].astype(BF16)
        vc = vc_ref[0, :, h, :].astype(BF16)
        kn = kn_ref[0, :, cs].astype(BF16)
        vn = vn_ref[0, :, cs].astype(BF16)
        cq = cq_ref[0, h]
        s1 = lax.dot_general(q, kc, (((1,), (1,)), ((), ())), preferred_element_type=F32) * scale
        s1 = s1 + (cq - ckc_ref[0, h])
        s2 = lax.dot_general(q, kn, (((1,), (1,)), ((), ())), preferred_element_type=F32) * scale
        s2 = jnp.where(si <= ti, s2 + (cq - ckn_ref[0, h]), -jnp.inf)
        m = jnp.maximum(jnp.max(s1, axis=-1, keepdims=True), jnp.max(s2, axis=-1, keepdims=True))
        p1 = jnp.exp(s1 - m)
        p2 = jnp.exp(s2 - m)
        den = jnp.sum(p1, axis=-1, keepdims=True) + jnp.sum(p2, axis=-1, keepdims=True)
        acc = (jnp.dot(p1.astype(BF16), vc, preferred_element_type=F32)
               + jnp.dot(p2.astype(BF16), vn, preferred_element_type=F32))
        o_ref[0, :, cs] = (acc / den).astype(BF16)


def _fox_sample_attend(q, kn, vn, cache_k, cache_v, csum, b, l):
    p = cache_k.shape[1]
    ct = csum.transpose(0, 2, 1)
    cq = ct[:, :, p:, None]
    ckc = ct[:, :, None, :p]
    ckn = ct[:, :, None, p:]
    tok = pl.BlockSpec((1, l, D_MODEL), lambda bi: (bi, 0, 0))
    cache = pl.BlockSpec((1, p, FOX_HEADS, FOX_HD), lambda bi: (bi, 0, 0, 0))
    o = pl.pallas_call(
        functools.partial(_fox_sample_kernel, l=l),
        out_shape=jax.ShapeDtypeStruct((b, l, D_MODEL), BF16),
        grid=(b,),
        in_specs=[tok, tok, tok, cache, cache,
                  pl.BlockSpec((1, FOX_HEADS, l, 1), lambda bi: (bi, 0, 0, 0)),
                  pl.BlockSpec((1, FOX_HEADS, 1, p), lambda bi: (bi, 0, 0, 0)),
                  pl.BlockSpec((1, FOX_HEADS, 1, l), lambda bi: (bi, 0, 0, 0))],
        out_specs=tok,
        compiler_params=_params("parallel"),
        name="fox_sample_attention",
    )(q.reshape(b, l, D_MODEL), kn.reshape(b, l, D_MODEL), vn.reshape(b, l, D_MODEL),
      cache_k, cache_v, cq, ckc, ckn)
    return o.reshape(b * l, D_MODEL)


def _layer_norm(z, g, b):
    mu = jnp.mean(z, axis=-1, keepdims=True)
    zc = z - mu
    var = jnp.mean(zc * zc, axis=-1, keepdims=True)
    return zc * lax.rsqrt(var + LN_EPS) * g + b


def _outproj_norm_kernel(y_ref, w_ref, x_ref, gate_ref, sc_ref, sh_ref, lng_ref, lnb_ref, rw_ref,
                         xn_ref, u_ref, s_ref):
    out = jnp.dot(y_ref[...], w_ref[...], preferred_element_type=F32)
    z = ALPHA * x_ref[...] + (1.0 + gate_ref[0]) * out
    xn = _layer_norm(z, lng_ref[...], lnb_ref[...])
    xn_ref[...] = xn
    u = xn * (1.0 + sc_ref[0]) + sh_ref[0]
    u_ref[...] = u
    s_ref[...] = jax.nn.sigmoid(jnp.dot(u.astype(BF16), rw_ref[...], preferred_element_type=F32))


def _outproj_norm(y, w, x, gate, sc, sh, ln_g, ln_b, rw, tm, bps):
    t, kdim = y.shape
    d = w.shape[1]
    row = pl.BlockSpec((tm, d), lambda i: (i, 0))
    vec = pl.BlockSpec((1, d), lambda i: (0, 0))
    once = pl.Buffered(1)
    return pl.pallas_call(
        _outproj_norm_kernel,
        out_shape=[jax.ShapeDtypeStruct((t, d), F32), jax.ShapeDtypeStruct((t, d), F32),
                   jax.ShapeDtypeStruct((t, LANES), F32)],
        grid=(t // tm,),
        in_specs=[pl.BlockSpec((tm, kdim), lambda i: (i, 0)),
                  pl.BlockSpec((kdim, d), lambda i: (0, 0), pipeline_mode=once),
                  row, _mod_spec(gate, bps), _mod_spec(sc, bps), _mod_spec(sh, bps), vec, vec,
                  pl.BlockSpec((d, LANES), lambda i: (0, 0), pipeline_mode=once)],
        out_specs=[row, row, pl.BlockSpec((tm, LANES), lambda i: (i, 0))],
        compiler_params=_params("parallel"),
        name="outproj_norm",
    )(y, w, x, gate, sc, sh, ln_g.reshape(1, d), ln_b.reshape(1, d), rw)


def _argmax_first(vals):
    best, idx = vals[0], jnp.zeros(vals[0].shape, jnp.int32)
    for j in range(1, len(vals)):
        gt = vals[j] > best
        best = jnp.where(gt, vals[j], best)
        idx = jnp.where(gt, j, idx)
    return best, idx


def _pick(rows, idx):
    out = rows[0]
    for j in range(1, len(rows)):
        out = jnp.where(idx == j, rows[j], out)
    return out


def _route_kernel(s_ref, b_ref, tri_ref, e_ref, r_ref, w_ref, cnt_ref):
    i = pl.program_id(0)

    @pl.when(i == 0)
    def _():
        cnt_ref[...] = jnp.zeros_like(cnt_ref)

    sc = s_ref[...].T[:N_EXPERTS, :]
    sel = sc + b_ref[...]
    row = lambda a, e: a[e:e + 1, :]
    grp = []
    for g in range(N_GROUPS):
        a, b, c, d = (row(sel, g * EXPERTS_PER_GROUP + j) for j in range(EXPERTS_PER_GROUP))
        hi1, lo1, hi2, lo2 = jnp.maximum(a, b), jnp.minimum(a, b), jnp.maximum(c, d), jnp.minimum(c, d)
        grp.append(jnp.maximum(hi1, hi2) + jnp.maximum(jnp.minimum(hi1, hi2), jnp.maximum(lo1, lo2)))
    _, gidx = _argmax_first(grp)
    member = lambda a: [_pick([row(a, g * EXPERTS_PER_GROUP + j) for g in range(N_GROUPS)], gidx)
                        for j in range(EXPERTS_PER_GROUP)]
    v, c = member(sel), member(sc)
    _, j0 = _argmax_first(v)
    _, j1 = _argmax_first([jnp.where(j0 == j, -jnp.inf, v[j]) for j in range(EXPERTS_PER_GROUP)])
    c0, c1 = _pick(c, j0), _pick(c, j1)
    den = c0 + c1
    e0 = gidx * EXPERTS_PER_GROUP + j0
    e1 = gidx * EXPERTS_PER_GROUP + j1
    eio = lax.broadcasted_iota(jnp.int32, sc.shape, 0)
    oh0, oh1 = eio == e0, eio == e1
    member_f = jnp.where(oh0 | oh1, 1.0, 0.0)
    before = jnp.dot(member_f.astype(BF16), tri_ref[...], preferred_element_type=F32) + cnt_ref[...]
    r0 = jnp.sum(jnp.where(oh0, before, 0.0), axis=0, keepdims=True)
    r1 = jnp.sum(jnp.where(oh1, before, 0.0), axis=0, keepdims=True)
    cnt_ref[...] += jnp.sum(member_f, axis=1, keepdims=True)
    e_ref[0:1, :] = e0
    e_ref[1:2, :] = e1
    r_ref[0:1, :] = r0.astype(jnp.int32)
    r_ref[1:2, :] = r1.astype(jnp.int32)
    w_ref[0:1, :] = c0 / den
    w_ref[1:2, :] = c1 / den


def _route(scores, router_b, tm_e):
    t = scores.shape[0]
    tm = _tile(t, 512)
    r = jnp.arange(tm)
    tri = (r[:, None] < r[None, :]).astype(BF16)
    slot = pl.BlockSpec((TOP_K, tm), lambda i: (0, i))
    eidx, rank, wts, counts = pl.pallas_call(
        _route_kernel,
        out_shape=[jax.ShapeDtypeStruct((TOP_K, t), jnp.int32), jax.ShapeDtypeStruct((TOP_K, t), jnp.int32),
                   jax.ShapeDtypeStruct((TOP_K, t), F32), jax.ShapeDtypeStruct((N_EXPERTS, 1), F32)],
        grid=(t // tm,),
        in_specs=[pl.BlockSpec((tm, LANES), lambda i: (i, 0)),
                  pl.BlockSpec((N_EXPERTS, 1), lambda i: (0, 0)),
                  pl.BlockSpec((tm, tm), lambda i: (0, 0))],
        out_specs=[slot, slot, slot, pl.BlockSpec((N_EXPERTS, 1), lambda i: (0, 0))],
        compiler_params=_params("arbitrary"),
        name="moe_route",
    )(scores, router_b.astype(F32).reshape(N_EXPERTS, 1), tri)
    counts = counts[:, 0].astype(jnp.int32)
    padded = (counts + tm_e - 1) // tm_e * tm_e
    pad_end = jnp.cumsum(padded)
    pad_start = pad_end - padded
    onehot = eidx[:, :, None] == jnp.arange(N_EXPERTS, dtype=jnp.int32)[None, None, :]
    dest = (jnp.sum(jnp.where(onehot, pad_start[None, None, :], 0), axis=-1) + rank).reshape(TOP_K * t)
    n_blocks = (t * TOP_K + N_EXPERTS * (tm_e - 1) + tm_e - 1) // tm_e
    blk_row = jnp.arange(n_blocks, dtype=jnp.int32) * tm_e
    blk_e = jnp.minimum(jnp.sum((pad_end[None, :] <= blk_row[:, None]).astype(jnp.int32), axis=1), N_EXPERTS - 1)
    n_used = (pad_end[-1] // tm_e).astype(jnp.int32).reshape(1)
    last_blk = jnp.concatenate([jnp.where(counts > 0, pad_end - tm_e, -1).astype(jnp.int32), n_used])
    return dest.astype(jnp.int32), wts.T, blk_e, n_used, n_blocks, last_blk


def _dispatch_kernel(dest_ref, last_ref, u_ref, xs_ref, zero_ref, sem, zsem, *, tb, t_total, tm_e, n_blocks):
    base = pl.program_id(0) * tb

    @pl.when(pl.program_id(0) == 0)
    def _():
        zero_ref[...] = jnp.zeros_like(zero_ref)

        def zero_copy(e):
            row = pl.multiple_of(jnp.maximum(last_ref[e], 0), tm_e)
            return pltpu.make_async_copy(zero_ref, xs_ref.at[pl.ds(row, tm_e), :], zsem)

        for e in range(N_EXPERTS):
            pl.when(last_ref[e] >= 0)(lambda e=e: zero_copy(e).start())
        for e in range(N_EXPERTS):
            pl.when(last_ref[e] >= 0)(lambda e=e: zero_copy(e).wait())

        def tail_copy(blk):
            return pltpu.make_async_copy(zero_ref, xs_ref.at[pl.ds(pl.multiple_of(blk * tm_e, tm_e), tm_e), :], zsem)

        n_used = last_ref[N_EXPERTS]
        lax.fori_loop(n_used, n_blocks, lambda blk, c: (tail_copy(blk).start(), c)[1], 0)
        lax.fori_loop(n_used, n_blocks, lambda blk, c: (tail_copy(blk).wait(), c)[1], 0)

    def row_copy(t, slot):
        d = dest_ref[slot * t_total + base + t]
        return pltpu.make_async_copy(u_ref.at[pl.ds(t, 1), :], xs_ref.at[pl.ds(d, 1), :], sem)

    def issue(t, carry):
        for slot in range(TOP_K):
            row_copy(t, slot).start()
        return carry

    lax.fori_loop(0, tb, issue, 0, unroll=DMA_UNROLL)
    for slot in range(TOP_K):
        pltpu.make_async_copy(u_ref, xs_ref.at[pl.ds(0, tb), :], sem).wait()


def _dispatch(u, dest, last_blk, n_rows, tb, tm_e):
    t, d = u.shape
    return pl.pallas_call(
        functools.partial(_dispatch_kernel, tb=tb, t_total=t, tm_e=tm_e, n_blocks=n_rows // tm_e),
        out_shape=jax.ShapeDtypeStruct((n_rows, d), F32),
        grid_spec=pltpu.PrefetchScalarGridSpec(
            num_scalar_prefetch=2,
            grid=(t // tb,),
            in_specs=[pl.BlockSpec((tb, d), lambda i, dest_ref, last_ref: (i, 0))],
            out_specs=pl.BlockSpec(memory_space=pl.ANY),
            scratch_shapes=[pltpu.VMEM((tm_e, d), F32), pltpu.SemaphoreType.DMA, pltpu.SemaphoreType.DMA],
        ),
        compiler_params=_params("arbitrary"),
        name="moe_dispatch",
    )(dest, last_blk, u)


def _expert_kernel(blk_e_ref, n_used_ref, xs_ref, wg_ref, wu_ref, wd_ref, ys_ref):
    del blk_e_ref
    i = pl.program_id(0)

    @pl.when(i < n_used_ref[0])
    def _():
        x = xs_ref[...].astype(BF16)
        g = jnp.dot(x, wg_ref[0], preferred_element_type=F32)
        up = jnp.dot(x, wu_ref[0], preferred_element_type=F32)
        hid = (g * jax.nn.sigmoid(g) * up).astype(BF16)
        ys_ref[...] = jnp.dot(hid, wd_ref[0], preferred_element_type=F32)

    @pl.when(i >= n_used_ref[0])
    def _():
        ys_ref[...] = jnp.zeros_like(ys_ref)


def _experts(xs, blk_e, n_used, wg, wu, wd, tm_e):
    r, d = xs.shape
    de = wg.shape[2]
    return pl.pallas_call(
        _expert_kernel,
        out_shape=jax.ShapeDtypeStruct((r, d), F32),
        grid_spec=pltpu.PrefetchScalarGridSpec(
            num_scalar_prefetch=2,
            grid=(r // tm_e,),
            in_specs=[pl.BlockSpec((tm_e, d), lambda i, be, nu: (i, 0)),
                      pl.BlockSpec((1, d, de), lambda i, be, nu: (be[i], 0, 0)),
                      pl.BlockSpec((1, d, de), lambda i, be, nu: (be[i], 0, 0)),
                      pl.BlockSpec((1, de, d), lambda i, be, nu: (be[i], 0, 0))],
            out_specs=pl.BlockSpec((tm_e, d), lambda i, be, nu: (i, 0)),
        ),
        compiler_params=_params("arbitrary"),
        name="moe_experts",
    )(blk_e, n_used, xs, wg, wu, wd)


def _combine_norm_kernel(dest_ref, ys_ref, wts_ref, x_ref, gate_ref, sc_ref, sh_ref, lng_ref, lnb_ref,
                         xn_ref, u_ref, y0_ref, y1_ref, sems, *, tb, t_total):
    i = pl.program_id(0)
    par = i % 2
    bufs = (y0_ref, y1_ref)

    def start_gathers(blk, p):
        base = blk * tb

        def issue(t, carry):
            for slot in range(TOP_K):
                d = dest_ref[slot * t_total + base + t]
                pltpu.make_async_copy(ys_ref.at[pl.ds(d, 1), :], bufs[slot].at[p, pl.ds(t, 1), :],
                                      sems.at[p]).start()
            return carry

        lax.fori_loop(0, tb, issue, 0, unroll=DMA_UNROLL)

    @pl.when(i == 0)
    def _():
        start_gathers(0, 0)

    @pl.when(i + 1 < pl.num_programs(0))
    def _():
        start_gathers(i + 1, 1 - par)

    for buf in bufs:
        pltpu.make_async_copy(ys_ref.at[pl.ds(0, tb), :], buf.at[par], sems.at[par]).wait()
    w = wts_ref[...]
    ffn = w[:, 0:1] * y0_ref[par] + w[:, 1:2] * y1_ref[par]
    z = ALPHA * x_ref[...] + (1.0 + gate_ref[0]) * ffn
    xn = _layer_norm(z, lng_ref[...], lnb_ref[...])
    xn_ref[...] = xn
    u_ref[...] = xn * (1.0 + sc_ref[0]) + sh_ref[0]


def _combine_norm(ys, dest, wts, x, gate, sc, sh, ln_g, ln_b, tb, bps):
    t, d = x.shape
    row = pl.BlockSpec((tb, d), lambda i, dr: (i, 0))
    vec = pl.BlockSpec((1, d), lambda i, dr: (0, 0))
    return pl.pallas_call(
        functools.partial(_combine_norm_kernel, tb=tb, t_total=t),
        out_shape=[jax.ShapeDtypeStruct((t, d), F32), jax.ShapeDtypeStruct((t, d), F32)],
        grid_spec=pltpu.PrefetchScalarGridSpec(
            num_scalar_prefetch=1,
            grid=(t // tb,),
            in_specs=[pl.BlockSpec(memory_space=pl.ANY),
                      pl.BlockSpec((tb, TOP_K), lambda i, dr: (i, 0)),
                      row, _mod_spec(gate, bps), _mod_spec(sc, bps), _mod_spec(sh, bps), vec, vec],
            out_specs=[row, row],
            scratch_shapes=[pltpu.VMEM((2, tb, d), F32), pltpu.VMEM((2, tb, d), F32),
                            pltpu.SemaphoreType.DMA((2,))],
        ),
        compiler_params=_params("arbitrary"),
        name="moe_combine_norm",
    )(dest, ys, wts, x, gate, sc, sh, ln_g.reshape(1, d), ln_b.reshape(1, d))


def _moe_block(x, u, scores, router_b, wg, wu, wd, gate, sc, sh, ln_g, ln_b, tm, bps, tm_e):
    dest, wts, blk_e, n_used, n_blocks, last_blk = _route(scores, router_b, tm_e)
    xs = _dispatch(u, dest, last_blk, n_blocks * tm_e, tm, tm_e)
    ys = _experts(xs, blk_e, n_used, wg, wu, wd, tm_e)
    tb = _tile(tm, 256) if gate.shape[1] == 1 else tm
    return _combine_norm(ys, dest, wts, x, gate, sc, sh, ln_g, ln_b, tb, bps * (tm // tb))


def _rope_tables(l, pos0):
    half = RET_DK // 2
    inv = ROPE_BASE ** (-jnp.arange(half, dtype=F32) / half)
    ang = (pos0 + jnp.arange(l)).astype(F32)[:, None] * inv[None, :]
    return jnp.cos(ang), jnp.sin(ang)


def _stream(x3, mods, wts, state_ret, state_hgrn, cache_k, cache_v, cache_logf, pos0, tm_e):
    b, l, d = x3.shape
    t = b * l
    fresh = state_ret is None
    if fresh:
        tm = _tile(l, 256)
        bps = l // tm
        expand = lambda m: m[:, None, :]
    else:
        tm = _tile(t, 256)
        bps = 1
        expand = lambda m: jnp.repeat(m, l, axis=0).reshape(t // tm, tm, d)
    x = x3.reshape(t, d)
    outs = dict(ret=[], hg=[], fk=[], fv=[], fl=[])
    log_gamma = jnp.log1p(-jnp.exp2(-5.0 - jnp.arange(RET_HEADS, dtype=F32)))
    cos, sin = _rope_tables(l, pos0)
    u = None
    for layer in range(DEPTH):
        m = [expand(a) for a in mods[layer]]
        if layer == 0:
            u = _modulate(x, m[1], m[0], tm, bps)
        kind, j = layer % N_MIXERS, layer // N_MIXERS
        if kind == 0:
            proj = _inproj(u, wts['ret_w_in'][j])
            s0 = None if fresh else state_ret[j]
            y, s = _retention(proj, cos, sin, log_gamma, wts['ret_gn_w'][j], s0, b, l)
            outs['ret'].append(s)
            w_out = wts['ret_w_out'][j]
        elif kind == 1:
            proj = _inproj(u, wts['hg_w_in'][j])
            s0 = None if fresh else state_hgrn[j]
            y, s = _hgrn(proj, wts['hg_b_f'][j], wts['lbs'][layer], wts['hg_norm_w'][j], s0, b, l)
            outs['hg'].append(s)
            w_out = wts['hg_w_out'][j]
        else:
            wq, wk, wv, wf, bf = wts['fox_in'][j]
            logf, csum = (a[:, :FOX_HEADS].reshape(b, l, FOX_HEADS) for a in _fox_gate(u, wf, bf, b, l))
            if fresh:
                tf = _tile(l, 512)
                (qh,) = _fox_inproj(u, wq, b, l, tf, False, True)
                kt, kh = _fox_inproj(u, wk, b, l, tf, True, True)
                vt, vh = _fox_inproj(u, wv, b, l, tf, True, True)
                y = _fox_prompt_attend(qh, kh, vh, csum, b, l)
            else:
                qt = _inproj(u, wq)
                kt = _inproj(u, wk)
                vt = _inproj(u, wv)
                lf_all = jnp.concatenate([cache_logf[j].astype(F32), logf], axis=1)
                pos = jnp.arange(lf_all.shape[1])
                csum = jnp.einsum('ts,bsh->bth', (pos[None, :] <= pos[:, None]).astype(F32), lf_all,
                                  precision=lax.Precision.HIGHEST)
                y = _fox_sample_attend(qt, kt, vt, cache_k[j], cache_v[j], csum, b, l)
            outs['fk'].append(kt.reshape(b, l, FOX_HEADS, FOX_HD))
            outs['fv'].append(vt.reshape(b, l, FOX_HEADS, FOX_HD))
            outs['fl'].append(logf)
            w_out = wts['fox_w_out'][j]
        x, u, scores = _outproj_norm(y, w_out, x, m[2], m[4], m[3], wts['ln_mix_g'][layer],
                                     wts['ln_mix_b'][layer], wts['router_w'], tm, bps)
        nxt = [expand(a) for a in mods[min(layer + 1, DEPTH - 1)]]
        x, u = _moe_block(x, u, scores, wts['router_b'], wts['moe_w_gate'][layer], wts['moe_w_up'][layer],
                          wts['moe_w_down'][layer], m[5], nxt[1], nxt[0], wts['ln_ffn_g'][layer],
                          wts['ln_ffn_b'][layer], tm, bps, tm_e)
    return x.reshape(b, l, d), outs


def kernel(x_prompt, x_sample, state_ret, state_hgrn, cache_fox_k, cache_fox_v, cache_fox_logf, c_prompt, c_sample, ada_w, ada_b, ln_mix_g, ln_mix_b, ln_ffn_g, ln_ffn_b, ret_w_in, ret_gn_w, ret_w_out, hg_w_in, hg_b_f, hg_lower_bounds, hg_norm_w, hg_w_out, fox_w_in, fox_b_f, fox_w_out, router_w, router_b, moe_w_gate, moe_w_up, moe_w_down):
    dt = x_prompt.dtype
    d = D_MODEL
    nbp = c_prompt.shape[0]
    lbs = jnp.cumsum(jax.nn.softmax(hg_lower_bounds.astype(F32), axis=0), axis=0)
    lbs = lbs - lbs[0]
    mod_all = _modulation_all(jnp.concatenate([c_prompt, c_sample], axis=0).astype(F32), ada_w, ada_b)
    split6 = lambda m: [m[:, i * d:(i + 1) * d] for i in range(6)]
    mods_p = [split6(mod_all[layer, :nbp]) for layer in range(DEPTH)]
    mods_s = [split6(mod_all[layer, nbp:]) for layer in range(DEPTH)]
    pad = LANES - FOX_HEADS
    fox_in = [(fox_w_in[j, :, :d].astype(BF16), fox_w_in[j, :, d:2 * d].astype(BF16),
               fox_w_in[j, :, 2 * d:3 * d].astype(BF16),
               jnp.pad(fox_w_in[j, :, 3 * d:], ((0, 0), (0, pad))).astype(BF16),
               jnp.pad(fox_b_f[j].astype(F32), (0, pad)).reshape(1, LANES))
              for j in range(fox_w_in.shape[0])]
    wts = dict(
        ret_w_in=ret_w_in.astype(BF16), ret_gn_w=ret_gn_w.astype(F32), ret_w_out=ret_w_out.astype(BF16),
        hg_w_in=hg_w_in.astype(BF16), hg_b_f=hg_b_f.astype(F32), lbs=lbs, hg_norm_w=hg_norm_w.astype(F32),
        hg_w_out=hg_w_out.astype(BF16), fox_in=fox_in, fox_w_out=fox_w_out.astype(BF16),
        router_w=jnp.pad(router_w, ((0, 0), (0, LANES - N_EXPERTS))).astype(BF16), router_b=router_b,
        moe_w_gate=moe_w_gate.astype(BF16), moe_w_up=moe_w_up.astype(BF16), moe_w_down=moe_w_down.astype(BF16),
        ln_mix_g=ln_mix_g.astype(F32), ln_mix_b=ln_mix_b.astype(F32),
        ln_ffn_g=ln_ffn_g.astype(F32), ln_ffn_b=ln_ffn_b.astype(F32))
    past_len = cache_fox_k.shape[2]
    yp, op = _stream(x_prompt, mods_p, wts, None, None, None, None, None, 0, 256)
    ys, os_ = _stream(x_sample, mods_s, wts, state_ret, state_hgrn, cache_fox_k, cache_fox_v,
                      cache_fox_logf, past_len, 128)
    st = lambda xs: jnp.stack(xs).astype(dt)
    return (yp, ys, st(op['ret']), st(os_['ret']), st(op['hg']), st(os_['hg']),
            st(op['fk']), st(op['fv']), st(op['fl']), st(os_['fk']), st(os_['fv']), st(os_['fl']))
```

```python
import functools

import jax
import jax.numpy as jnp
from jax import lax
from jax.experimental import pallas as pl
from jax.experimental.pallas import tpu as pltpu

F32 = jnp.float32
BF16 = jnp.bfloat16

D_MODEL = 2048
DEPTH = 4
CHUNK = 64
N_MIXERS = 3
RET_HEADS = 8
RET_DK = D_MODEL // RET_HEADS
RET_DV = 2 * RET_DK
RET_QK = RET_HEADS * RET_DK
RET_V = RET_HEADS * RET_DV
ROPE_BASE = 10000.0
HG_DK = 128
HG_HEADS = D_MODEL // HG_DK
HG_DV = D_MODEL // HG_HEADS
HG_BLOCK = 16
FOX_HEADS = 16
FOX_HD = D_MODEL // FOX_HEADS
N_EXPERTS = 16
N_GROUPS = 4
EXPERTS_PER_GROUP = N_EXPERTS // N_GROUPS
TOPK_GROUP = 1
TOP_K = 2
D_EXPERT = D_MODEL // 2
ALPHA = (2 * DEPTH) ** 0.25
LN_EPS = 1e-5
NORM_EPS = 1e-6
LOG2E = 1.4426950408889634

LANES = 128
VMEM_LIMIT = 56 * 1024 * 1024
DMA_UNROLL = 8


def _params(*sem):
    return pltpu.CompilerParams(dimension_semantics=sem, vmem_limit_bytes=VMEM_LIMIT)


def _tile(n, pref):
    t = min(n, pref)
    while n % t:
        t //= 2
    return t


def _mod_kernel(c_ref, w_ref, b_ref, o_ref):
    c = c_ref[...]
    a = (c * jax.nn.sigmoid(c)).astype(BF16)
    o_ref[0] = jnp.dot(a, w_ref[0].astype(BF16), preferred_element_type=F32) + b_ref[0]


def _modulation_all(c_all, ada_w, ada_b):
    nb = c_all.shape[0]
    depth, d, n = ada_w.shape
    tn = _tile(n, 1024)
    return pl.pallas_call(
        _mod_kernel,
        out_shape=jax.ShapeDtypeStruct((depth, nb, n), F32),
        grid=(depth, n // tn),
        in_specs=[pl.BlockSpec((nb, d), lambda l, j: (0, 0)),
                  pl.BlockSpec((1, d, tn), lambda l, j: (l, 0, j)),
                  pl.BlockSpec((1, 1, tn), lambda l, j: (l, 0, j))],
        out_specs=pl.BlockSpec((1, nb, tn), lambda l, j: (l, 0, j)),
        compiler_params=_params("parallel", "parallel"),
        name="modulation",
    )(c_all, ada_w, ada_b.reshape(depth, 1, n))


def _mod_spec(mod, bps):
    return pl.BlockSpec((1,) + mod.shape[1:], lambda i, *_: (i // bps, 0, 0))


def _inproj_kernel(x_ref, w_ref, *rest, modulated):
    if modulated:
        sc_ref, sh_ref, o_ref, xb_ref = rest
    else:
        o_ref, xb_ref = rest

    @pl.when(pl.program_id(1) == 0)
    def _():
        x = x_ref[...]
        if modulated:
            x = x * (1.0 + sc_ref[0]) + sh_ref[0]
        xb_ref[...] = x.astype(BF16)

    o_ref[...] = jnp.dot(xb_ref[...], w_ref[...], preferred_element_type=F32)


def _inproj(u, w, mod=None):
    t, d = u.shape
    n = w.shape[1]
    tm = _tile(t, 1024)
    tn = _tile(n, 1024)
    in_specs = [pl.BlockSpec((tm, d), lambda i, j: (i, 0)),
                pl.BlockSpec((d, tn), lambda i, j: (0, j))]
    args = [u, w]
    if mod is not None:
        sc, sh, seq_rows = mod
        if seq_rows is None:
            sc, sh, bps = sc.reshape(t // tm, tm, d), sh.reshape(t // tm, tm, d), 1
        else:
            tm = _tile(seq_rows, tm)
            bps = seq_rows // tm
            in_specs[0] = pl.BlockSpec((tm, d), lambda i, j: (i, 0))
        in_specs += [_mod_spec(sc, bps), _mod_spec(sh, bps)]
        args += [sc, sh]
    return pl.pallas_call(
        functools.partial(_inproj_kernel, modulated=mod is not None),
        out_shape=jax.ShapeDtypeStruct((t, n), F32),
        grid=(t // tm, n // tn),
        in_specs=in_specs,
        out_specs=pl.BlockSpec((tm, tn), lambda i, j: (i, j)),
        scratch_shapes=[pltpu.VMEM((tm, d), BF16)],
        compiler_params=_params("parallel", "arbitrary"),
        name="inproj",
    )(*args)


def _fox_inproj_kernel(x_ref, w_ref, *outs, tok, heads):
    acc = jnp.dot(x_ref[...].astype(BF16), w_ref[...], preferred_element_type=F32)
    n = 0
    if tok:
        outs[n][...] = acc
        n += 1
    if heads:
        for h in range(FOX_HEADS):
            outs[n][0, h] = acc[:, h * FOX_HD:(h + 1) * FOX_HD].astype(BF16)


def _fox_inproj(u, w, b, l, tm, tok, heads):
    t, d = u.shape
    bps = l // tm
    out_shape, out_specs = [], []
    if tok:
        out_shape.append(jax.ShapeDtypeStruct((t, d), F32))
        out_specs.append(pl.BlockSpec((tm, d), lambda i: (i, 0)))
    if heads:
        out_shape.append(jax.ShapeDtypeStruct((b, FOX_HEADS, l, FOX_HD), BF16))
        out_specs.append(pl.BlockSpec((1, FOX_HEADS, tm, FOX_HD), lambda i: (i // bps, 0, i % bps, 0)))
    return pl.pallas_call(
        functools.partial(_fox_inproj_kernel, tok=tok, heads=heads),
        out_shape=out_shape,
        grid=(t // tm,),
        in_specs=[pl.BlockSpec((tm, d), lambda i: (i, 0)),
                  pl.BlockSpec((d, d), lambda i: (0, 0))],
        out_specs=out_specs,
        compiler_params=_params("parallel"),
        name="fox_inproj",
    )(u, w)


def _split3(x):
    hi = x.astype(BF16)
    r1 = x - hi.astype(F32)
    mid = r1.astype(BF16)
    lo = (r1 - mid.astype(F32)).astype(BF16)
    return hi, mid, lo


def _tri_cumsum(tri, x):
    hi, mid, lo = _split3(x)
    return (jnp.dot(tri, hi, preferred_element_type=F32) + jnp.dot(tri, mid, preferred_element_type=F32)
            + jnp.dot(tri, lo, preferred_element_type=F32))


def _fox_gate_kernel(x_ref, w_ref, b_ref, tri_ref, o_ref, c_ref, carry_ref):
    @pl.when(pl.program_id(1) == 0)
    def _():
        carry_ref[...] = jnp.zeros_like(carry_ref)

    z = jnp.dot(x_ref[...].astype(BF16), w_ref[...], preferred_element_type=F32) + b_ref[...]
    logf = jnp.minimum(z, 0.0) - jnp.log1p(jnp.exp(-jnp.abs(z)))
    o_ref[...] = logf
    csum = _tri_cumsum(tri_ref[...], logf) + carry_ref[...]
    c_ref[...] = csum
    carry_ref[...] = csum[csum.shape[0] - 1:, :]


def _fox_gate(u, w, b, nb, l):
    t, d = u.shape
    tm = _tile(l, 256)
    bps = l // tm
    r = jnp.arange(tm)
    tri = (r[None, :] <= r[:, None]).astype(BF16)
    row = pl.BlockSpec((tm, LANES), lambda bi, li: (bi * bps + li, 0))
    return pl.pallas_call(
        _fox_gate_kernel,
        out_shape=[jax.ShapeDtypeStruct((t, LANES), F32), jax.ShapeDtypeStruct((t, LANES), F32)],
        grid=(nb, bps),
        in_specs=[pl.BlockSpec((tm, d), lambda bi, li: (bi * bps + li, 0)),
                  pl.BlockSpec((d, LANES), lambda bi, li: (0, 0)),
                  pl.BlockSpec((1, LANES), lambda bi, li: (0, 0)),
                  pl.BlockSpec((tm, tm), lambda bi, li: (0, 0))],
        out_specs=[row, row],
        scratch_shapes=[pltpu.VMEM((1, LANES), F32)],
        compiler_params=_params("parallel", "arbitrary"),
        name="fox_gate",
    )(u, w, b, tri)


def _retention_kernel(lg_ref, q_ref, k_ref, v_ref, g_ref, cos_ref, sin_ref, gn_ref, dec_ref, *rest,
                      lb, has_state):
    if has_state:
        s0_ref, y_ref, sout_ref, s_ref = rest
    else:
        y_ref, sout_ref, s_ref = rest
    h = pl.program_id(1)
    li = pl.program_id(2)

    @pl.when(li == 0)
    def _():
        if has_state:
            s_ref[...] = s0_ref[0, 0]
        else:
            s_ref[...] = jnp.zeros_like(s_ref)

    lg = lg_ref[h]
    half = RET_DK // 2
    decay = dec_ref[0]
    idx = lax.broadcasted_iota(jnp.int32, (lb, 1), 0).astype(F32)
    q_dec = jnp.exp((idx + 1.0) * lg)
    k_dec = jnp.exp((lb - 1.0 - idx) * lg)
    s_dec = jnp.exp(jnp.full((1, 1), lb, F32) * lg)

    def rope(x, cos, sin):
        x1, x2 = x[:, :half], x[:, half:]
        return jnp.concatenate([x1 * cos - x2 * sin, x1 * sin + x2 * cos], axis=-1)

    cos, sin = cos_ref[...], sin_ref[...]
    q = rope(q_ref[0], cos, sin)
    k = rope(k_ref[0], cos, sin) * (RET_DK ** -0.5)
    vb = v_ref[0].astype(BF16)
    scores = lax.dot_general(q.astype(BF16), k.astype(BF16), (((1,), (1,)), ((), ())),
                             preferred_element_type=F32) * decay
    s = s_ref[...]
    o = (jnp.dot(scores.astype(BF16), vb, preferred_element_type=F32)
         + jnp.dot((q * q_dec).astype(BF16), s.astype(BF16), preferred_element_type=F32))
    kd = (k * k_dec).T.astype(BF16)
    s_ref[...] = s * s_dec + jnp.dot(kd, vb, preferred_element_type=F32)
    mu = jnp.mean(o, axis=-1, keepdims=True)
    oc = o - mu
    var = jnp.mean(oc * oc, axis=-1, keepdims=True)
    y = oc * lax.rsqrt(var + NORM_EPS) * gn_ref[...]
    g = g_ref[0]
    y_ref[0] = (g * jax.nn.sigmoid(g) * y).astype(BF16)

    @pl.when(li == pl.num_programs(2) - 1)
    def _():
        sout_ref[0, 0] = s_ref[...]


def _retention(proj, cos, sin, log_gamma, gn_w, s0, b, l):
    cl = min(l, CHUNK)
    lb = _tile(l, 4 * cl)
    p3 = proj.reshape(b, l, proj.shape[1])
    nq = RET_QK // RET_DK
    nv = (2 * RET_QK) // RET_DV
    has_state = s0 is not None
    in_specs = [pl.BlockSpec(memory_space=pltpu.SMEM),
                pl.BlockSpec((1, lb, RET_DK), lambda bi, h, li: (bi, li, h)),
                pl.BlockSpec((1, lb, RET_DK), lambda bi, h, li: (bi, li, nq + h)),
                pl.BlockSpec((1, lb, RET_DV), lambda bi, h, li: (bi, li, nv + h)),
                pl.BlockSpec((1, lb, RET_DV), lambda bi, h, li: (bi, li, nv + RET_HEADS + h)),
                pl.BlockSpec((lb, RET_DK // 2), lambda bi, h, li: (li, 0)),
                pl.BlockSpec((lb, RET_DK // 2), lambda bi, h, li: (li, 0)),
                pl.BlockSpec((1, RET_DV), lambda bi, h, li: (0, h)),
                pl.BlockSpec((1, lb, lb), lambda bi, h, li: (h, 0, 0))]
    pos = jnp.arange(lb)
    dt = pos[:, None] - pos[None, :]
    same = (pos[:, None] // cl) == (pos[None, :] // cl)
    dist = jnp.where(same, jnp.abs(dt), dt).astype(F32)
    decay = jnp.where((same | (dt > 0))[None], jnp.exp(dist[None] * log_gamma[:, None, None]), 0.0)
    args = [log_gamma, p3, p3, p3, p3, cos, sin, gn_w.reshape(1, RET_V), decay]
    state_spec = pl.BlockSpec((1, 1, RET_DK, RET_DV), lambda bi, h, li: (bi, h, 0, 0))
    if has_state:
        in_specs.append(state_spec)
        args.append(s0)
    y, s = pl.pallas_call(
        functools.partial(_retention_kernel, lb=lb, has_state=has_state),
        out_shape=[jax.ShapeDtypeStruct((b, l, RET_V), BF16),
                   jax.ShapeDtypeStruct((b, RET_HEADS, RET_DK, RET_DV), F32)],
        grid=(b, RET_HEADS, l // lb),
        in_specs=in_specs,
        out_specs=[pl.BlockSpec((1, lb, RET_DV), lambda bi, h, li: (bi, li, h)), state_spec],
        scratch_shapes=[pltpu.VMEM((RET_DK, RET_DV), F32)],
        compiler_params=_params("parallel", "parallel", "arbitrary"),
        name="retention",
    )(*args)
    return y.reshape(b * l, RET_V), s


def _hgrn_kernel(q_ref, fz_ref, v_ref, g_ref, bf_ref, lb_ref, nw_ref, tri_ref, *rest, lb_rows, has_state):
    if has_state:
        s0_ref, y_ref, sout_ref, st_ref, gc_ref, k_ref, o_ref = rest
    else:
        y_ref, sout_ref, st_ref, gc_ref, k_ref, o_ref = rest
    li = pl.program_id(1)
    hb = HG_BLOCK
    half = hb // 2

    @pl.when(li == 0)
    def _():
        for h in range(HG_HEADS):
            if has_state:
                st_ref[h] = s0_ref[0, h].T
            else:
                st_ref[h] = jnp.zeros((HG_DV, HG_DK), F32)

    lbv = lb_ref[...]
    f = lbv + (1.0 - lbv) * jax.nn.sigmoid(fz_ref[0] + bf_ref[...])
    logf = jnp.log(f)
    k_ref[...] = 1.0 - f
    gc_ref[...] = _tri_cumsum(tri_ref[...], logf) * LOG2E

    rt = lax.broadcasted_iota(jnp.int32, (half, 1), 0)
    contract_last = (((1,), (1,)), ((), ()))

    def block(bi, carry):
        r0 = pl.multiple_of(bi * hb, hb)
        for h in range(HG_HEADS):
            cs = slice(h * HG_DK, (h + 1) * HG_DK)
            gb = gc_ref[pl.ds(r0, hb), cs]
            qb = q_ref[0, pl.ds(r0, hb), cs]
            kb = k_ref[pl.ds(r0, hb), cs]
            vb = v_ref[0, pl.ds(r0, hb), cs]
            q_top, q_bot = qb[:half], qb[half:]
            g_top, g_bot = gb[:half], gb[half:]
            i_top = jnp.zeros((half, HG_DV), F32)
            i_bot = jnp.zeros((half, HG_DV), F32)
            for s in range(hb):
                gs, ks, vs = gb[s:s + 1], kb[s:s + 1], vb[s:s + 1]
                if s < half:
                    e = jnp.where(rt >= s, jnp.exp2(g_top - gs), 0.0)
                    a = jnp.sum(q_top * e * ks, axis=-1, keepdims=True)
                    i_top = i_top + a * vs
                    e = jnp.exp2(g_bot - gs)
                else:
                    e = jnp.where(rt + half >= s, jnp.exp2(g_bot - gs), 0.0)
                a = jnp.sum(q_bot * e * ks, axis=-1, keepdims=True)
                i_bot = i_bot + a * vs
            intra = jnp.concatenate([i_top, i_bot], axis=0)
            st = st_ref[h]
            qt = (qb * jnp.exp2(gb)).astype(BF16)
            inter = lax.dot_general(qt, st.astype(BF16), contract_last, preferred_element_type=F32)
            o_ref[pl.ds(r0, hb), cs] = intra + inter
            gl = gb[hb - 1:hb]
            kt = (kb * jnp.exp2(gl - gb)).astype(BF16)
            upd = jnp.dot(vb.T.astype(BF16), kt, preferred_element_type=F32)
            st_ref[h] = st * jnp.exp2(gl) + upd
        return carry

    lax.fori_loop(0, lb_rows // hb, block, 0)

    for h in range(HG_HEADS):
        cs = slice(h * HG_DK, (h + 1) * HG_DK)
        oh = o_ref[:, cs]
        on = oh * lax.rsqrt(jnp.mean(oh * oh, axis=-1, keepdims=True) + NORM_EPS)
        g = g_ref[0, :, cs]
        y_ref[0, :, cs] = (on * nw_ref[:, cs] * (g * jax.nn.sigmoid(g))).astype(BF16)

    @pl.when(li == pl.num_programs(1) - 1)
    def _():
        for h in range(HG_HEADS):
            sout_ref[0, h] = st_ref[h].T


def _hgrn(proj, b_f, lb, norm_w, s0, b, l):
    assert l % HG_BLOCK == 0
    lbr = _tile(l, 128)
    p3 = proj.reshape(b, l, 4 * D_MODEL)
    r = jnp.arange(lbr)
    tri = ((r[:, None] // HG_BLOCK == r[None, :] // HG_BLOCK) & (r[None, :] <= r[:, None])).astype(BF16)
    has_state = s0 is not None
    col = lambda j: pl.BlockSpec((1, lbr, D_MODEL), lambda bi, li: (bi, li, j))
    vec = pl.BlockSpec((1, D_MODEL), lambda bi, li: (0, 0))
    in_specs = [col(0), col(1), col(2), col(3), vec, vec, vec,
                pl.BlockSpec((lbr, lbr), lambda bi, li: (0, 0))]
    args = [p3, p3, p3, p3, b_f.reshape(1, D_MODEL), lb.reshape(1, D_MODEL), norm_w.reshape(1, D_MODEL), tri]
    state_spec = pl.BlockSpec((1, HG_HEADS, HG_DK, HG_DV), lambda bi, li: (bi, 0, 0, 0))
    if has_state:
        in_specs.append(state_spec)
        args.append(s0)
    y, s = pl.pallas_call(
        functools.partial(_hgrn_kernel, lb_rows=lbr, has_state=has_state),
        out_shape=[jax.ShapeDtypeStruct((b, l, D_MODEL), BF16),
                   jax.ShapeDtypeStruct((b, HG_HEADS, HG_DK, HG_DV), F32)],
        grid=(b, l // lbr),
        in_specs=in_specs,
        out_specs=[pl.BlockSpec((1, lbr, D_MODEL), lambda bi, li: (bi, li, 0)), state_spec],
        scratch_shapes=[pltpu.VMEM((HG_HEADS, HG_DV, HG_DK), F32),
                        pltpu.VMEM((lbr, D_MODEL), F32),
                        pltpu.VMEM((lbr, D_MODEL), F32),
                        pltpu.VMEM((lbr, D_MODEL), F32)],
        compiler_params=_params("parallel", "arbitrary"),
        name="hgrn2",
    )(*args)
    return y.reshape(b * l, D_MODEL), s


def _fox_prompt_kernel(q_ref, k_ref, v_ref, cq_ref, ck_ref, o_ref, *, l, tq, tk):
    scale = FOX_HD ** -0.5
    for qi in range(l // tq):
        q0 = qi * tq
        q = q_ref[0, 0, q0:q0 + tq, :]
        cq = cq_ref[0, 0, q0:q0 + tq, :]
        m = jnp.full((tq, 1), -jnp.inf, F32)
        den = jnp.zeros((tq, 1), F32)
        acc = jnp.zeros((tq, FOX_HD), F32)
        for kj in range(l // tk):
            k0 = kj * tk
            if k0 > q0 + tq - 1:
                continue
            s = lax.dot_general(q, k_ref[0, 0, k0:k0 + tk, :], (((1,), (1,)), ((), ())),
                                preferred_element_type=F32) * scale
            s = s + (cq - ck_ref[0, 0, :, k0:k0 + tk])
            if k0 + tk - 1 > q0:
                qpos = q0 + lax.broadcasted_iota(jnp.int32, (tq, tk), 0)
                kpos = k0 + lax.broadcasted_iota(jnp.int32, (tq, tk), 1)
                s = jnp.where(kpos <= qpos, s, -jnp.inf)
            m_new = jnp.maximum(m, jnp.max(s, axis=-1, keepdims=True))
            w = jnp.exp(m - m_new)
            p = jnp.exp(s - m_new)
            den = den * w + jnp.sum(p, axis=-1, keepdims=True)
            acc = acc * w + jnp.dot(p.astype(BF16), v_ref[0, 0, k0:k0 + tk, :], preferred_element_type=F32)
            m = m_new
        o_ref[0, q0:q0 + tq, :] = (acc / den).astype(BF16)


def _fox_prompt_attend(q, k, v, csum, b, l):
    tq = _tile(l, 256)
    tk = _tile(l, 512)
    cq = csum.transpose(0, 2, 1)[..., None]
    ck = csum.transpose(0, 2, 1)[:, :, None, :]
    head = pl.BlockSpec((1, 1, l, FOX_HD), lambda bi, h: (bi, h, 0, 0))
    o = pl.pallas_call(
        functools.partial(_fox_prompt_kernel, l=l, tq=tq, tk=tk),
        out_shape=jax.ShapeDtypeStruct((b, l, D_MODEL), BF16),
        grid=(b, FOX_HEADS),
        in_specs=[head, head, head,
                  pl.BlockSpec((1, 1, l, 1), lambda bi, h: (bi, h, 0, 0)),
                  pl.BlockSpec((1, 1, 1, l), lambda bi, h: (bi, h, 0, 0))],
        out_specs=pl.BlockSpec((1, l, FOX_HD), lambda bi, h: (bi, 0, h)),
        compiler_params=_params("parallel", "parallel"),
        name="fox_prompt_attention",
    )(q, k, v, cq, ck)
    return o.reshape(b * l, D_MODEL)


def _fox_sample_kernel(q_ref, kn_ref, vn_ref, kc_ref, vc_ref, cq_ref, ckc_ref, ckn_ref, o_ref, *, l):
    scale = FOX_HD ** -0.5
    ti = lax.broadcasted_iota(jnp.int32, (l, l), 0)
    si = lax.broadcasted_iota(jnp.int32, (l, l), 1)
    for h in range(FOX_HEADS):
        cs = slice(h * FOX_HD, (h + 1) * FOX_HD)
        q = q_ref[0, :, cs].astype(BF16)
        kc = kc_ref[0, :, h, :].astype(BF16)
        vc = vc_ref[0, :, h, :].astype(BF16)
        kn = kn_ref[0, :, cs].astype(BF16)
        vn = vn_ref[0, :, cs].astype(BF16)
        cq = cq_ref[0, h]
        s1 = lax.dot_general(q, kc, (((1,), (1,)), ((), ())), preferred_element_type=F32) * scale
        s1 = s1 + (cq - ckc_ref[0, h])
        s2 = lax.dot_general(q, kn, (((1,), (1,)), ((), ())), preferred_element_type=F32) * scale
        s2 = jnp.where(si <= ti, s2 + (cq - ckn_ref[0, h]), -jnp.inf)
        m = jnp.maximum(jnp.max(s1, axis=-1, keepdims=True), jnp.max(s2, axis=-1, keepdims=True))
        p1 = jnp.exp(s1 - m)
        p2 = jnp.exp(s2 - m)
        den = jnp.sum(p1, axis=-1, keepdims=True) + jnp.sum(p2, axis=-1, keepdims=True)
        acc = (jnp.dot(p1.astype(BF16), vc, preferred_element_type=F32)
               + jnp.dot(p2.astype(BF16), vn, preferred_element_type=F32))
        o_ref[0, :, cs] = (acc / den).astype(BF16)


def _fox_sample_attend(q, kn, vn, cache_k, cache_v, csum, b, l):
    p = cache_k.shape[1]
    ct = csum.transpose(0, 2, 1)
    cq = ct[:, :, p:, None]
    ckc = ct[:, :, None, :p]
    ckn = ct[:, :, None, p:]
    tok = pl.BlockSpec((1, l, D_MODEL), lambda bi: (bi, 0, 0))
    cache = pl.BlockSpec((1, p, FOX_HEADS, FOX_HD), lambda bi: (bi, 0, 0, 0))
    o = pl.pallas_call(
        functools.partial(_fox_sample_kernel, l=l),
        out_shape=jax.ShapeDtypeStruct((b, l, D_MODEL), BF16),
        grid=(b,),
        in_specs=[tok, tok, tok, cache, cache,
                  pl.BlockSpec((1, FOX_HEADS, l, 1), lambda bi: (bi, 0, 0, 0)),
                  pl.BlockSpec((1, FOX_HEADS, 1, p), lambda bi: (bi, 0, 0, 0)),
                  pl.BlockSpec((1, FOX_HEADS, 1, l), lambda bi: (bi, 0, 0, 0))],
        out_specs=tok,
        compiler_params=_params("parallel"),
        name="fox_sample_attention",
    )(q.reshape(b, l, D_MODEL), kn.reshape(b, l, D_MODEL), vn.reshape(b, l, D_MODEL),
      cache_k, cache_v, cq, ckc, ckn)
    return o.reshape(b * l, D_MODEL)


def _layer_norm(z, g, b):
    mu = jnp.mean(z, axis=-1, keepdims=True)
    zc = z - mu
    var = jnp.mean(zc * zc, axis=-1, keepdims=True)
    return zc * lax.rsqrt(var + LN_EPS) * g + b


def _outproj_norm_kernel(y_ref, w_ref, x_ref, gate_ref, sc_ref, sh_ref, lng_ref, lnb_ref, rw_ref,
                         xn_ref, u_ref, s_ref):
    out = jnp.dot(y_ref[...], w_ref[...], preferred_element_type=F32)
    z = ALPHA * x_ref[...] + (1.0 + gate_ref[0]) * out
    xn = _layer_norm(z, lng_ref[...], lnb_ref[...])
    xn_ref[...] = xn
    u = xn * (1.0 + sc_ref[0]) + sh_ref[0]
    u_ref[...] = u
    s_ref[...] = jax.nn.sigmoid(jnp.dot(u.astype(BF16), rw_ref[...], preferred_element_type=F32))


def _outproj_norm(y, w, x, gate, sc, sh, ln_g, ln_b, rw, tm, bps):
    t, kdim = y.shape
    d = w.shape[1]
    row = pl.BlockSpec((tm, d), lambda i: (i, 0))
    vec = pl.BlockSpec((1, d), lambda i: (0, 0))
    once = pl.Buffered(1)
    return pl.pallas_call(
        _outproj_norm_kernel,
        out_shape=[jax.ShapeDtypeStruct((t, d), F32), jax.ShapeDtypeStruct((t, d), F32),
                   jax.ShapeDtypeStruct((t, LANES), F32)],
        grid=(t // tm,),
        in_specs=[pl.BlockSpec((tm, kdim), lambda i: (i, 0)),
                  pl.BlockSpec((kdim, d), lambda i: (0, 0), pipeline_mode=once),
                  row, _mod_spec(gate, bps), _mod_spec(sc, bps), _mod_spec(sh, bps), vec, vec,
                  pl.BlockSpec((d, LANES), lambda i: (0, 0), pipeline_mode=once)],
        out_specs=[row, row, pl.BlockSpec((tm, LANES), lambda i: (i, 0))],
        compiler_params=_params("parallel"),
        name="outproj_norm",
    )(y, w, x, gate, sc, sh, ln_g.reshape(1, d), ln_b.reshape(1, d), rw)


def _argmax_first(vals):
    best, idx = vals[0], jnp.zeros(vals[0].shape, jnp.int32)
    for j in range(1, len(vals)):
        gt = vals[j] > best
        best = jnp.where(gt, vals[j], best)
        idx = jnp.where(gt, j, idx)
    return best, idx


def _pick(rows, idx):
    out = rows[0]
    for j in range(1, len(rows)):
        out = jnp.where(idx == j, rows[j], out)
    return out


def _route_kernel(s_ref, b_ref, tri_ref, e_ref, r_ref, w_ref, cnt_ref):
    i = pl.program_id(0)

    @pl.when(i == 0)
    def _():
        cnt_ref[...] = jnp.zeros_like(cnt_ref)

    sc = s_ref[...].T[:N_EXPERTS, :]
    sel = sc + b_ref[...]
    row = lambda a, e: a[e:e + 1, :]
    grp = []
    for g in range(N_GROUPS):
        a, b, c, d = (row(sel, g * EXPERTS_PER_GROUP + j) for j in range(EXPERTS_PER_GROUP))
        hi1, lo1, hi2, lo2 = jnp.maximum(a, b), jnp.minimum(a, b), jnp.maximum(c, d), jnp.minimum(c, d)
        grp.append(jnp.maximum(hi1, hi2) + jnp.maximum(jnp.minimum(hi1, hi2), jnp.maximum(lo1, lo2)))
    _, gidx = _argmax_first(grp)
    member = lambda a: [_pick([row(a, g * EXPERTS_PER_GROUP + j) for g in range(N_GROUPS)], gidx)
                        for j in range(EXPERTS_PER_GROUP)]
    v, c = member(sel), member(sc)
    _, j0 = _argmax_first(v)
    _, j1 = _argmax_first([jnp.where(j0 == j, -jnp.inf, v[j]) for j in range(EXPERTS_PER_GROUP)])
    c0, c1 = _pick(c, j0), _pick(c, j1)
    den = c0 + c1
    e0 = gidx * EXPERTS_PER_GROUP + j0
    e1 = gidx * EXPERTS_PER_GROUP + j1
    eio = lax.broadcasted_iota(jnp.int32, sc.shape, 0)
    oh0, oh1 = eio == e0, eio == e1
    member_f = jnp.where(oh0 | oh1, 1.0, 0.0)
    before = jnp.dot(member_f.astype(BF16), tri_ref[...], preferred_element_type=F32) + cnt_ref[...]
    r0 = jnp.sum(jnp.where(oh0, before, 0.0), axis=0, keepdims=True)
    r1 = jnp.sum(jnp.where(oh1, before, 0.0), axis=0, keepdims=True)
    cnt_ref[...] += jnp.sum(member_f, axis=1, keepdims=True)
    e_ref[0:1, :] = e0
    e_ref[1:2, :] = e1
    r_ref[0:1, :] = r0.astype(jnp.int32)
    r_ref[1:2, :] = r1.astype(jnp.int32)
    w_ref[0:1, :] = c0 / den
    w_ref[1:2, :] = c1 / den


def _route(scores, router_b, tm_e):
    t = scores.shape[0]
    tm = _tile(t, 512)
    r = jnp.arange(tm)
    tri = (r[:, None] < r[None, :]).astype(BF16)
    slot = pl.BlockSpec((TOP_K, tm), lambda i: (0, i))
    eidx, rank, wts, counts = pl.pallas_call(
        _route_kernel,
        out_shape=[jax.ShapeDtypeStruct((TOP_K, t), jnp.int32), jax.ShapeDtypeStruct((TOP_K, t), jnp.int32),
                   jax.ShapeDtypeStruct((TOP_K, t), F32), jax.ShapeDtypeStruct((N_EXPERTS, 1), F32)],
        grid=(t // tm,),
        in_specs=[pl.BlockSpec((tm, LANES), lambda i: (i, 0)),
                  pl.BlockSpec((N_EXPERTS, 1), lambda i: (0, 0)),
                  pl.BlockSpec((tm, tm), lambda i: (0, 0))],
        out_specs=[slot, slot, slot, pl.BlockSpec((N_EXPERTS, 1), lambda i: (0, 0))],
        compiler_params=_params("arbitrary"),
        name="moe_route",
    )(scores, router_b.astype(F32).reshape(N_EXPERTS, 1), tri)
    counts = counts[:, 0].astype(jnp.int32)
    padded = (counts + tm_e - 1) // tm_e * tm_e
    pad_end = jnp.cumsum(padded)
    pad_start = pad_end - padded
    onehot = eidx[:, :, None] == jnp.arange(N_EXPERTS, dtype=jnp.int32)[None, None, :]
    dest = (jnp.sum(jnp.where(onehot, pad_start[None, None, :], 0), axis=-1) + rank).reshape(TOP_K * t)
    n_blocks = (t * TOP_K + N_EXPERTS * (tm_e - 1) + tm_e - 1) // tm_e
    blk_row = jnp.arange(n_blocks, dtype=jnp.int32) * tm_e
    blk_e = jnp.minimum(jnp.sum((pad_end[None, :] <= blk_row[:, None]).astype(jnp.int32), axis=1), N_EXPERTS - 1)
    n_used = (pad_end[-1] // tm_e).astype(jnp.int32).reshape(1)
    last_blk = jnp.concatenate([jnp.where(counts > 0, pad_end - tm_e, -1).astype(jnp.int32), n_used])
    return dest.astype(jnp.int32), wts.T, blk_e, n_used, n_blocks, last_blk


def _dispatch_kernel(dest_ref, last_ref, u_ref, xs_ref, zero_ref, sem, zsem, *, tb, t_total, tm_e, n_blocks):
    base = pl.program_id(0) * tb

    @pl.when(pl.program_id(0) == 0)
    def _():
        zero_ref[...] = jnp.zeros_like(zero_ref)

        def zero_copy(e):
            row = pl.multiple_of(jnp.maximum(last_ref[e], 0), tm_e)
            return pltpu.make_async_copy(zero_ref, xs_ref.at[pl.ds(row, tm_e), :], zsem)

        for e in range(N_EXPERTS):
            pl.when(last_ref[e] >= 0)(lambda e=e: zero_copy(e).start())
        for e in range(N_EXPERTS):
            pl.when(last_ref[e] >= 0)(lambda e=e: zero_copy(e).wait())

        def tail_copy(blk):
            return pltpu.make_async_copy(zero_ref, xs_ref.at[pl.ds(pl.multiple_of(blk * tm_e, tm_e), tm_e), :], zsem)

        n_used = last_ref[N_EXPERTS]
        lax.fori_loop(n_used, n_blocks, lambda blk, c: (tail_copy(blk).start(), c)[1], 0)
        lax.fori_loop(n_used, n_blocks, lambda blk, c: (tail_copy(blk).wait(), c)[1], 0)

    def row_copy(t, slot):
        d = dest_ref[slot * t_total + base + t]
        return pltpu.make_async_copy(u_ref.at[pl.ds(t, 1), :], xs_ref.at[pl.ds(d, 1), :], sem)

    def issue(t, carry):
        for slot in range(TOP_K):
            row_copy(t, slot).start()
        return carry

    lax.fori_loop(0, tb, issue, 0, unroll=DMA_UNROLL)
    for slot in range(TOP_K):
        pltpu.make_async_copy(u_ref, xs_ref.at[pl.ds(0, tb), :], sem).wait()


def _dispatch(u, dest, last_blk, n_rows, tb, tm_e):
    t, d = u.shape
    return pl.pallas_call(
        functools.partial(_dispatch_kernel, tb=tb, t_total=t, tm_e=tm_e, n_blocks=n_rows // tm_e),
        out_shape=jax.ShapeDtypeStruct((n_rows, d), F32),
        grid_spec=pltpu.PrefetchScalarGridSpec(
            num_scalar_prefetch=2,
            grid=(t // tb,),
            in_specs=[pl.BlockSpec((tb, d), lambda i, dest_ref, last_ref: (i, 0))],
            out_specs=pl.BlockSpec(memory_space=pl.ANY),
            scratch_shapes=[pltpu.VMEM((tm_e, d), F32), pltpu.SemaphoreType.DMA, pltpu.SemaphoreType.DMA],
        ),
        compiler_params=_params("arbitrary"),
        name="moe_dispatch",
    )(dest, last_blk, u)


def _expert_kernel(blk_e_ref, n_used_ref, xs_ref, wg_ref, wu_ref, wd_ref, ys_ref):
    del blk_e_ref
    i = pl.program_id(0)

    @pl.when(i < n_used_ref[0])
    def _():
        x = xs_ref[...].astype(BF16)
        g = jnp.dot(x, wg_ref[0], preferred_element_type=F32)
        up = jnp.dot(x, wu_ref[0], preferred_element_type=F32)
        hid = (g * jax.nn.sigmoid(g) * up).astype(BF16)
        ys_ref[...] = jnp.dot(hid, wd_ref[0], preferred_element_type=F32)

    @pl.when(i >= n_used_ref[0])
    def _():
        ys_ref[...] = jnp.zeros_like(ys_ref)


def _experts(xs, blk_e, n_used, wg, wu, wd, tm_e):
    r, d = xs.shape
    de = wg.shape[2]
    return pl.pallas_call(
        _expert_kernel,
        out_shape=jax.ShapeDtypeStruct((r, d), F32),
        grid_spec=pltpu.PrefetchScalarGridSpec(
            num_scalar_prefetch=2,
            grid=(r // tm_e,),
            in_specs=[pl.BlockSpec((tm_e, d), lambda i, be, nu: (i, 0)),
                      pl.BlockSpec((1, d, de), lambda i, be, nu: (be[i], 0, 0)),
                      pl.BlockSpec((1, d, de), lambda i, be, nu: (be[i], 0, 0)),
                      pl.BlockSpec((1, de, d), lambda i, be, nu: (be[i], 0, 0))],
            out_specs=pl.BlockSpec((tm_e, d), lambda i, be, nu: (i, 0)),
        ),
        compiler_params=_params("arbitrary"),
        name="moe_experts",
    )(blk_e, n_used, xs, wg, wu, wd)


def _combine_norm_kernel(dest_ref, ys_ref, wts_ref, x_ref, gate_ref, sc_ref, sh_ref, lng_ref, lnb_ref,
                         xn_ref, u_ref, y0_ref, y1_ref, sems, *, tb, t_total):
    i = pl.program_id(0)
    par = i % 2
    bufs = (y0_ref, y1_ref)

    def start_gathers(blk, p):
        base = blk * tb

        def issue(t, carry):
            for slot in range(TOP_K):
                d = dest_ref[slot * t_total + base + t]
                pltpu.make_async_copy(ys_ref.at[pl.ds(d, 1), :], bufs[slot].at[p, pl.ds(t, 1), :],
                                      sems.at[p]).start()
            return carry

        lax.fori_loop(0, tb, issue, 0, unroll=DMA_UNROLL)

    @pl.when(i == 0)
    def _():
        start_gathers(0, 0)

    @pl.when(i + 1 < pl.num_programs(0))
    def _():
        start_gathers(i + 1, 1 - par)

    for buf in bufs:
        pltpu.make_async_copy(ys_ref.at[pl.ds(0, tb), :], buf.at[par], sems.at[par]).wait()
    w = wts_ref[...]
    ffn = w[:, 0:1] * y0_ref[par] + w[:, 1:2] * y1_ref[par]
    z = ALPHA * x_ref[...] + (1.0 + gate_ref[0]) * ffn
    xn = _layer_norm(z, lng_ref[...], lnb_ref[...])
    xn_ref[...] = xn
    u_ref[...] = xn * (1.0 + sc_ref[0]) + sh_ref[0]


def _combine_norm(ys, dest, wts, x, gate, sc, sh, ln_g, ln_b, tb, bps):
    t, d = x.shape
    row = pl.BlockSpec((tb, d), lambda i, dr: (i, 0))
    vec = pl.BlockSpec((1, d), lambda i, dr: (0, 0))
    return pl.pallas_call(
        functools.partial(_combine_norm_kernel, tb=tb, t_total=t),
        out_shape=[jax.ShapeDtypeStruct((t, d), F32), jax.ShapeDtypeStruct((t, d), F32)],
        grid_spec=pltpu.PrefetchScalarGridSpec(
            num_scalar_prefetch=1,
            grid=(t // tb,),
            in_specs=[pl.BlockSpec(memory_space=pl.ANY),
                      pl.BlockSpec((tb, TOP_K), lambda i, dr: (i, 0)),
                      row, _mod_spec(gate, bps), _mod_spec(sc, bps), _mod_spec(sh, bps), vec, vec],
            out_specs=[row, row],
            scratch_shapes=[pltpu.VMEM((2, tb, d), F32), pltpu.VMEM((2, tb, d), F32),
                            pltpu.SemaphoreType.DMA((2,))],
        ),
        compiler_params=_params("arbitrary"),
        name="moe_combine_norm",
    )(dest, ys, wts, x, gate, sc, sh, ln_g.reshape(1, d), ln_b.reshape(1, d))


def _moe_block(x, u, scores, router_b, wg, wu, wd, gate, sc, sh, ln_g, ln_b, tm, bps, tm_e):
    dest, wts, blk_e, n_used, n_blocks, last_blk = _route(scores, router_b, tm_e)
    xs = _dispatch(u, dest, last_blk, n_blocks * tm_e, tm, tm_e)
    ys = _experts(xs, blk_e, n_used, wg, wu, wd, tm_e)
    tb = _tile(tm, 256) if gate.shape[1] == 1 else tm
    return _combine_norm(ys, dest, wts, x, gate, sc, sh, ln_g, ln_b, tb, bps * (tm // tb))


def _rope_tables(l, pos0):
    half = RET_DK // 2
    inv = ROPE_BASE ** (-jnp.arange(half, dtype=F32) / half)
    ang = (pos0 + jnp.arange(l)).astype(F32)[:, None] * inv[None, :]
    return jnp.cos(ang), jnp.sin(ang)


def _stream(x3, mods, wts, state_ret, state_hgrn, cache_k, cache_v, cache_logf, pos0, tm_e):
    b, l, d = x3.shape
    t = b * l
    fresh = state_ret is None
    if fresh:
        tm = _tile(l, 256)
        bps = l // tm
        expand = lambda m: m[:, None, :]
    else:
        tm = _tile(t, 256)
        bps = 1
        expand = lambda m: jnp.repeat(m, l, axis=0).reshape(t // tm, tm, d)
    x = x3.reshape(t, d)
    outs = dict(ret=[], hg=[], fk=[], fv=[], fl=[])
    log_gamma = jnp.log1p(-jnp.exp2(-5.0 - jnp.arange(RET_HEADS, dtype=F32)))
    cos, sin = _rope_tables(l, pos0)
    u = None
    for layer in range(DEPTH):
        m = [expand(a) for a in mods[layer]]
        kind, j = layer % N_MIXERS, layer // N_MIXERS
        if kind == 0:
            if layer == 0:
                proj = _inproj(x, wts['ret_w_in'][j], (m[1], m[0], l if fresh else None))
            else:
                proj = _inproj(u, wts['ret_w_in'][j])
            s0 = None if fresh else state_ret[j]
            y, s = _retention(proj, cos, sin, log_gamma, wts['ret_gn_w'][j], s0, b, l)
            outs['ret'].append(s)
            w_out = wts['ret_w_out'][j]
        elif kind == 1:
            proj = _inproj(u, wts['hg_w_in'][j])
            s0 = None if fresh else state_hgrn[j]
            y, s = _hgrn(proj, wts['hg_b_f'][j], wts['lbs'][layer], wts['hg_norm_w'][j], s0, b, l)
            outs['hg'].append(s)
            w_out = wts['hg_w_out'][j]
        else:
            wq, wk, wv, wf, bf = wts['fox_in'][j]
            logf, csum = (a[:, :FOX_HEADS].reshape(b, l, FOX_HEADS) for a in _fox_gate(u, wf, bf, b, l))
            if fresh:
                tf = _tile(l, 512)
                (qh,) = _fox_inproj(u, wq, b, l, tf, False, True)
                kt, kh = _fox_inproj(u, wk, b, l, tf, True, True)
                vt, vh = _fox_inproj(u, wv, b, l, tf, True, True)
                y = _fox_prompt_attend(qh, kh, vh, csum, b, l)
            else:
                qt = _inproj(u, wq)
                kt = _inproj(u, wk)
                vt = _inproj(u, wv)
                lf_all = jnp.concatenate([cache_logf[j].astype(F32), logf], axis=1)
                pos = jnp.arange(lf_all.shape[1])
                csum = jnp.einsum('ts,bsh->bth', (pos[None, :] <= pos[:, None]).astype(F32), lf_all,
                                  precision=lax.Precision.HIGHEST)
                y = _fox_sample_attend(qt, kt, vt, cache_k[j], cache_v[j], csum, b, l)
            outs['fk'].append(kt.reshape(b, l, FOX_HEADS, FOX_HD))
            outs['fv'].append(vt.reshape(b, l, FOX_HEADS, FOX_HD))
            outs['fl'].append(logf)
            w_out = wts['fox_w_out'][j]
        x, u, scores = _outproj_norm(y, w_out, x, m[2], m[4], m[3], wts['ln_mix_g'][layer],
                                     wts['ln_mix_b'][layer], wts['router_w'], tm, bps)
        nxt = [expand(a) for a in mods[min(layer + 1, DEPTH - 1)]]
        x, u = _moe_block(x, u, scores, wts['router_b'], wts['moe_w_gate'][layer], wts['moe_w_up'][layer],
                          wts['moe_w_down'][layer], m[5], nxt[1], nxt[0], wts['ln_ffn_g'][layer],
                          wts['ln_ffn_b'][layer], tm, bps, tm_e)
    return x.reshape(b, l, d), outs


def kernel(x_prompt, x_sample, state_ret, state_hgrn, cache_fox_k, cache_fox_v, cache_fox_logf, c_prompt, c_sample, ada_w, ada_b, ln_mix_g, ln_mix_b, ln_ffn_g, ln_ffn_b, ret_w_in, ret_gn_w, ret_w_out, hg_w_in, hg_b_f, hg_lower_bounds, hg_norm_w, hg_w_out, fox_w_in, fox_b_f, fox_w_out, router_w, router_b, moe_w_gate, moe_w_up, moe_w_down):
    dt = x_prompt.dtype
    d = D_MODEL
    nbp = c_prompt.shape[0]
    lbs = jnp.cumsum(jax.nn.softmax(hg_lower_bounds.astype(F32), axis=0), axis=0)
    lbs = lbs - lbs[0]
    mod_all = _modulation_all(jnp.concatenate([c_prompt, c_sample], axis=0).astype(F32), ada_w, ada_b)
    split6 = lambda m: [m[:, i * d:(i + 1) * d] for i in range(6)]
    mods_p = [split6(mod_all[layer, :nbp]) for layer in range(DEPTH)]
    mods_s = [split6(mod_all[layer, nbp:]) for layer in range(DEPTH)]
    pad = LANES - FOX_HEADS
    fox_in = [(fox_w_in[j, :, :d].astype(BF16), fox_w_in[j, :, d:2 * d].astype(BF16),
               fox_w_in[j, :, 2 * d:3 * d].astype(BF16),
               jnp.pad(fox_w_in[j, :, 3 * d:], ((0, 0), (0, pad))).astype(BF16),
               jnp.pad(fox_b_f[j].astype(F32), (0, pad)).reshape(1, LANES))
              for j in range(fox_w_in.shape[0])]
    wts = dict(
        ret_w_in=ret_w_in.astype(BF16), ret_gn_w=ret_gn_w.astype(F32), ret_w_out=ret_w_out.astype(BF16),
        hg_w_in=hg_w_in.astype(BF16), hg_b_f=hg_b_f.astype(F32), lbs=lbs, hg_norm_w=hg_norm_w.astype(F32),
        hg_w_out=hg_w_out.astype(BF16), fox_in=fox_in, fox_w_out=fox_w_out.astype(BF16),
        router_w=jnp.pad(router_w, ((0, 0), (0, LANES - N_EXPERTS))).astype(BF16), router_b=router_b,
        moe_w_gate=moe_w_gate.astype(BF16), moe_w_up=moe_w_up.astype(BF16), moe_w_down=moe_w_down.astype(BF16),
        ln_mix_g=ln_mix_g.astype(F32), ln_mix_b=ln_mix_b.astype(F32),
        ln_ffn_g=ln_ffn_g.astype(F32), ln_ffn_b=ln_ffn_b.astype(F32))
    past_len = cache_fox_k.shape[2]
    yp, op = _stream(x_prompt, mods_p, wts, None, None, None, None, None, 0, 256)
    ys, os_ = _stream(x_sample, mods_s, wts, state_ret, state_hgrn, cache_fox_k, cache_fox_v,
                      cache_fox_logf, past_len, 128)
    st = lambda xs: jnp.stack(xs).astype(dt)
    return (yp, ys, st(op['ret']), st(os_['ret']), st(op['hg']), st(os_['hg']),
            st(op['fk']), st(op['fv']), st(op['fl']), st(os_['fk']), st(os_['fv']), st(os_['fl']))
```

```python
import functools

import jax
import jax.numpy as jnp
from jax import lax
from jax.experimental import pallas as pl
from jax.experimental.pallas import tpu as pltpu

F32 = jnp.float32
BF16 = jnp.bfloat16

D_MODEL = 2048
DEPTH = 4
CHUNK = 64
N_MIXERS = 3
RET_HEADS = 8
RET_DK = D_MODEL // RET_HEADS
RET_DV = 2 * RET_DK
RET_QK = RET_HEADS * RET_DK
RET_V = RET_HEADS * RET_DV
RET_GROUP = 4
ROPE_BASE = 10000.0
HG_DK = 128
HG_HEADS = D_MODEL // HG_DK
HG_DV = D_MODEL // HG_HEADS
HG_BLOCK = 16
FOX_HEADS = 16
FOX_HD = D_MODEL // FOX_HEADS
N_EXPERTS = 16
N_GROUPS = 4
EXPERTS_PER_GROUP = N_EXPERTS // N_GROUPS
TOPK_GROUP = 1
TOP_K = 2
D_EXPERT = D_MODEL // 2
ALPHA = (2 * DEPTH) ** 0.25
LN_EPS = 1e-5
NORM_EPS = 1e-6
LOG2E = 1.4426950408889634

LANES = 128
VMEM_LIMIT = 56 * 1024 * 1024
DMA_UNROLL = 8


def _params(*sem):
    return pltpu.CompilerParams(dimension_semantics=sem, vmem_limit_bytes=VMEM_LIMIT)


def _tile(n, pref):
    t = min(n, pref)
    while n % t:
        t //= 2
    return t


def _mod_kernel(c_ref, w_ref, b_ref, o_ref):
    c = c_ref[...]
    a = (c * jax.nn.sigmoid(c)).astype(BF16)
    o_ref[0] = jnp.dot(a, w_ref[0].astype(BF16), preferred_element_type=F32) + b_ref[0]


def _modulation_all(c_all, ada_w, ada_b):
    nb = c_all.shape[0]
    depth, d, n = ada_w.shape
    tn = _tile(n, 1024)
    return pl.pallas_call(
        _mod_kernel,
        out_shape=jax.ShapeDtypeStruct((depth, nb, n), F32),
        grid=(depth, n // tn),
        in_specs=[pl.BlockSpec((nb, d), lambda l, j: (0, 0)),
                  pl.BlockSpec((1, d, tn), lambda l, j: (l, 0, j)),
                  pl.BlockSpec((1, 1, tn), lambda l, j: (l, 0, j))],
        out_specs=pl.BlockSpec((1, nb, tn), lambda l, j: (l, 0, j)),
        compiler_params=_params("parallel", "parallel"),
        name="modulation",
    )(c_all, ada_w, ada_b.reshape(depth, 1, n))


def _mod_spec(mod, bps):
    return pl.BlockSpec((1,) + mod.shape[1:], lambda i, *_: (i // bps, 0, 0))


def _inproj_kernel(x_ref, w_ref, *rest, modulated):
    if modulated:
        sc_ref, sh_ref, o_ref, xb_ref = rest
    else:
        o_ref, xb_ref = rest

    @pl.when(pl.program_id(1) == 0)
    def _():
        x = x_ref[...]
        if modulated:
            x = x * (1.0 + sc_ref[0]) + sh_ref[0]
        xb_ref[...] = x.astype(BF16)

    o_ref[...] = jnp.dot(xb_ref[...], w_ref[...], preferred_element_type=F32)


def _inproj(u, w, mod=None):
    t, d = u.shape
    n = w.shape[1]
    tm = _tile(t, 1024)
    tn = _tile(n, 1024)
    in_specs = [pl.BlockSpec((tm, d), lambda i, j: (i, 0)),
                pl.BlockSpec((d, tn), lambda i, j: (0, j))]
    args = [u, w]
    if mod is not None:
        sc, sh, seq_rows = mod
        if seq_rows is None:
            sc, sh, bps = sc.reshape(t // tm, tm, d), sh.reshape(t // tm, tm, d), 1
        else:
            tm = _tile(seq_rows, tm)
            bps = seq_rows // tm
            in_specs[0] = pl.BlockSpec((tm, d), lambda i, j: (i, 0))
        in_specs += [_mod_spec(sc, bps), _mod_spec(sh, bps)]
        args += [sc, sh]
    return pl.pallas_call(
        functools.partial(_inproj_kernel, modulated=mod is not None),
        out_shape=jax.ShapeDtypeStruct((t, n), F32),
        grid=(t // tm, n // tn),
        in_specs=in_specs,
        out_specs=pl.BlockSpec((tm, tn), lambda i, j: (i, j)),
        scratch_shapes=[pltpu.VMEM((tm, d), BF16)],
        compiler_params=_params("parallel", "arbitrary"),
        name="inproj",
    )(*args)


def _fox_inproj_kernel(x_ref, w_ref, *outs, tok, heads):
    acc = jnp.dot(x_ref[...].astype(BF16), w_ref[...], preferred_element_type=F32)
    n = 0
    if tok:
        outs[n][...] = acc
        n += 1
    if heads:
        for h in range(FOX_HEADS):
            outs[n][0, h] = acc[:, h * FOX_HD:(h + 1) * FOX_HD].astype(BF16)


def _fox_inproj(u, w, b, l, tm, tok, heads):
    t, d = u.shape
    bps = l // tm
    out_shape, out_specs = [], []
    if tok:
        out_shape.append(jax.ShapeDtypeStruct((t, d), F32))
        out_specs.append(pl.BlockSpec((tm, d), lambda i: (i, 0)))
    if heads:
        out_shape.append(jax.ShapeDtypeStruct((b, FOX_HEADS, l, FOX_HD), BF16))
        out_specs.append(pl.BlockSpec((1, FOX_HEADS, tm, FOX_HD), lambda i: (i // bps, 0, i % bps, 0)))
    return pl.pallas_call(
        functools.partial(_fox_inproj_kernel, tok=tok, heads=heads),
        out_shape=out_shape,
        grid=(t // tm,),
        in_specs=[pl.BlockSpec((tm, d), lambda i: (i, 0)),
                  pl.BlockSpec((d, d), lambda i: (0, 0))],
        out_specs=out_specs,
        compiler_params=_params("parallel"),
        name="fox_inproj",
    )(u, w)


def _split3(x):
    hi = x.astype(BF16)
    r1 = x - hi.astype(F32)
    mid = r1.astype(BF16)
    lo = (r1 - mid.astype(F32)).astype(BF16)
    return hi, mid, lo


def _tri_cumsum(tri, x):
    hi, mid, lo = _split3(x)
    return (jnp.dot(tri, hi, preferred_element_type=F32) + jnp.dot(tri, mid, preferred_element_type=F32)
            + jnp.dot(tri, lo, preferred_element_type=F32))


def _fox_gate_kernel(x_ref, w_ref, b_ref, tri_ref, o_ref, c_ref, carry_ref):
    @pl.when(pl.program_id(1) == 0)
    def _():
        carry_ref[...] = jnp.zeros_like(carry_ref)

    z = jnp.dot(x_ref[...].astype(BF16), w_ref[...], preferred_element_type=F32) + b_ref[...]
    logf = jnp.minimum(z, 0.0) - jnp.log1p(jnp.exp(-jnp.abs(z)))
    o_ref[...] = logf
    csum = _tri_cumsum(tri_ref[...], logf) + carry_ref[...]
    c_ref[...] = csum
    carry_ref[...] = csum[csum.shape[0] - 1:, :]


def _fox_gate(u, w, b, nb, l):
    t, d = u.shape
    tm = _tile(l, 256)
    bps = l // tm
    r = jnp.arange(tm)
    tri = (r[None, :] <= r[:, None]).astype(BF16)
    row = pl.BlockSpec((tm, LANES), lambda bi, li: (bi * bps + li, 0))
    return pl.pallas_call(
        _fox_gate_kernel,
        out_shape=[jax.ShapeDtypeStruct((t, LANES), F32), jax.ShapeDtypeStruct((t, LANES), F32)],
        grid=(nb, bps),
        in_specs=[pl.BlockSpec((tm, d), lambda bi, li: (bi * bps + li, 0)),
                  pl.BlockSpec((d, LANES), lambda bi, li: (0, 0)),
                  pl.BlockSpec((1, LANES), lambda bi, li: (0, 0)),
                  pl.BlockSpec((tm, tm), lambda bi, li: (0, 0))],
        out_specs=[row, row],
        scratch_shapes=[pltpu.VMEM((1, LANES), F32)],
        compiler_params=_params("parallel", "arbitrary"),
        name="fox_gate",
    )(u, w, b, tri)


def _retention_kernel(lg_ref, q_ref, k_ref, v_ref, g_ref, cos_ref, sin_ref, gn_ref, dec_ref, *rest,
                      lb, has_state):
    if has_state:
        s0_ref, y_ref, sout_ref, s_ref = rest
    else:
        y_ref, sout_ref, s_ref = rest
    hg = pl.program_id(1)
    li = pl.program_id(2)

    @pl.when(li == 0)
    def _():
        if has_state:
            s_ref[...] = s0_ref[0]
        else:
            s_ref[...] = jnp.zeros_like(s_ref)

    half = RET_DK // 2
    idx = lax.broadcasted_iota(jnp.int32, (lb, 1), 0).astype(F32)
    cos, sin = cos_ref[...], sin_ref[...]

    def rope(x):
        x1, x2 = x[:, :half], x[:, half:]
        return jnp.concatenate([x1 * cos - x2 * sin, x1 * sin + x2 * cos], axis=-1)

    for j in range(RET_GROUP):
        lg = lg_ref[hg * RET_GROUP + j]
        q_dec = jnp.exp((idx + 1.0) * lg)
        k_dec = jnp.exp((lb - 1.0 - idx) * lg)
        s_dec = jnp.exp(jnp.full((1, 1), lb, F32) * lg)
        qk_cols = slice(j * RET_DK, (j + 1) * RET_DK)
        v_cols = slice(j * RET_DV, (j + 1) * RET_DV)
        q = rope(q_ref[0, :, qk_cols])
        k = rope(k_ref[0, :, qk_cols]) * (RET_DK ** -0.5)
        vb = v_ref[0, :, v_cols].astype(BF16)
        scores = lax.dot_general(q.astype(BF16), k.astype(BF16), (((1,), (1,)), ((), ())),
                                 preferred_element_type=F32) * dec_ref[j]
        s = s_ref[j]
        o = (jnp.dot(scores.astype(BF16), vb, preferred_element_type=F32)
             + jnp.dot((q * q_dec).astype(BF16), s.astype(BF16), preferred_element_type=F32))
        kd = (k * k_dec).T.astype(BF16)
        s_ref[j] = s * s_dec + jnp.dot(kd, vb, preferred_element_type=F32)
        mu = jnp.mean(o, axis=-1, keepdims=True)
        oc = o - mu
        var = jnp.mean(oc * oc, axis=-1, keepdims=True)
        y = oc * lax.rsqrt(var + NORM_EPS) * gn_ref[:, v_cols]
        g = g_ref[0, :, v_cols]
        y_ref[0, :, v_cols] = (g * jax.nn.sigmoid(g) * y).astype(BF16)

    @pl.when(li == pl.num_programs(2) - 1)
    def _():
        sout_ref[0] = s_ref[...]


def _retention(proj, cos, sin, log_gamma, gn_w, s0, b, l):
    cl = min(l, CHUNK)
    lb = _tile(l, 4 * cl)
    p3 = proj.reshape(b, l, proj.shape[1])
    ng = RET_HEADS // RET_GROUP
    qw, vw = RET_GROUP * RET_DK, RET_GROUP * RET_DV
    has_state = s0 is not None
    in_specs = [pl.BlockSpec(memory_space=pltpu.SMEM),
                pl.BlockSpec((1, lb, qw), lambda bi, h, li: (bi, li, h)),
                pl.BlockSpec((1, lb, qw), lambda bi, h, li: (bi, li, ng + h)),
                pl.BlockSpec((1, lb, vw), lambda bi, h, li: (bi, li, ng + h)),
                pl.BlockSpec((1, lb, vw), lambda bi, h, li: (bi, li, 2 * ng + h)),
                pl.BlockSpec((lb, RET_DK // 2), lambda bi, h, li: (li, 0)),
                pl.BlockSpec((lb, RET_DK // 2), lambda bi, h, li: (li, 0)),
                pl.BlockSpec((1, vw), lambda bi, h, li: (0, h)),
                pl.BlockSpec((RET_GROUP, lb, lb), lambda bi, h, li: (h, 0, 0))]
    pos = jnp.arange(lb)
    dt = pos[:, None] - pos[None, :]
    same = (pos[:, None] // cl) == (pos[None, :] // cl)
    dist = jnp.where(same, jnp.abs(dt), dt).astype(F32)
    decay = jnp.where((same | (dt > 0))[None], jnp.exp(dist[None] * log_gamma[:, None, None]), 0.0)
    args = [log_gamma, p3, p3, p3, p3, cos, sin, gn_w.reshape(1, RET_V), decay]
    state_spec = pl.BlockSpec((1, RET_GROUP, RET_DK, RET_DV), lambda bi, h, li: (bi, h, 0, 0))
    if has_state:
        in_specs.append(state_spec)
        args.append(s0)
    y, s = pl.pallas_call(
        functools.partial(_retention_kernel, lb=lb, has_state=has_state),
        out_shape=[jax.ShapeDtypeStruct((b, l, RET_V), BF16),
                   jax.ShapeDtypeStruct((b, RET_HEADS, RET_DK, RET_DV), F32)],
        grid=(b, ng, l // lb),
        in_specs=in_specs,
        out_specs=[pl.BlockSpec((1, lb, vw), lambda bi, h, li: (bi, li, h)), state_spec],
        scratch_shapes=[pltpu.VMEM((RET_GROUP, RET_DK, RET_DV), F32)],
        compiler_params=_params("parallel", "parallel", "arbitrary"),
        name="retention",
    )(*args)
    return y.reshape(b * l, RET_V), s


def _hgrn_kernel(q_ref, fz_ref, v_ref, g_ref, bf_ref, lb_ref, nw_ref, tri_ref, *rest, lb_rows, has_state):
    if has_state:
        s0_ref, y_ref, sout_ref, st_ref, gc_ref, k_ref, o_ref = rest
    else:
        y_ref, sout_ref, st_ref, gc_ref, k_ref, o_ref = rest
    li = pl.program_id(1)
    hb = HG_BLOCK
    half = hb // 2

    @pl.when(li == 0)
    def _():
        for h in range(HG_HEADS):
            if has_state:
                st_ref[h] = s0_ref[0, h].T
            else:
                st_ref[h] = jnp.zeros((HG_DV, HG_DK), F32)

    lbv = lb_ref[...]
    f = lbv + (1.0 - lbv) * jax.nn.sigmoid(fz_ref[0] + bf_ref[...])
    logf = jnp.log(f)
    k_ref[...] = 1.0 - f
    gc_ref[...] = _tri_cumsum(tri_ref[...], logf) * LOG2E

    rt = lax.broadcasted_iota(jnp.int32, (half, 1), 0)
    contract_last = (((1,), (1,)), ((), ()))

    def block(bi, carry):
        r0 = pl.multiple_of(bi * hb, hb)
        for h in range(HG_HEADS):
            cs = slice(h * HG_DK, (h + 1) * HG_DK)
            gb = gc_ref[pl.ds(r0, hb), cs]
            qb = q_ref[0, pl.ds(r0, hb), cs]
            kb = k_ref[pl.ds(r0, hb), cs]
            vb = v_ref[0, pl.ds(r0, hb), cs]
            q_top, q_bot = qb[:half], qb[half:]
            g_top, g_bot = gb[:half], gb[half:]
            i_top = jnp.zeros((half, HG_DV), F32)
            i_bot = jnp.zeros((half, HG_DV), F32)
            for s in range(hb):
                gs, ks, vs = gb[s:s + 1], kb[s:s + 1], vb[s:s + 1]
                if s < half:
                    e = jnp.where(rt >= s, jnp.exp2(g_top - gs), 0.0)
                    a = jnp.sum(q_top * e * ks, axis=-1, keepdims=True)
                    i_top = i_top + a * vs
                    e = jnp.exp2(g_bot - gs)
                else:
                    e = jnp.where(rt + half >= s, jnp.exp2(g_bot - gs), 0.0)
                a = jnp.sum(q_bot * e * ks, axis=-1, keepdims=True)
                i_bot = i_bot + a * vs
            intra = jnp.concatenate([i_top, i_bot], axis=0)
            st = st_ref[h]
            qt = (qb * jnp.exp2(gb)).astype(BF16)
            inter = lax.dot_general(qt, st.astype(BF16), contract_last, preferred_element_type=F32)
            o_ref[pl.ds(r0, hb), cs] = intra + inter
            gl = gb[hb - 1:hb]
            kt = (kb * jnp.exp2(gl - gb)).astype(BF16)
            upd = jnp.dot(vb.T.astype(BF16), kt, preferred_element_type=F32)
            st_ref[h] = st * jnp.exp2(gl) + upd
        return carry

    lax.fori_loop(0, lb_rows // hb, block, 0)

    for h in range(HG_HEADS):
        cs = slice(h * HG_DK, (h + 1) * HG_DK)
        oh = o_ref[:, cs]
        on = oh * lax.rsqrt(jnp.mean(oh * oh, axis=-1, keepdims=True) + NORM_EPS)
        g = g_ref[0, :, cs]
        y_ref[0, :, cs] = (on * nw_ref[:, cs] * (g * jax.nn.sigmoid(g))).astype(BF16)

    @pl.when(li == pl.num_programs(1) - 1)
    def _():
        for h in range(HG_HEADS):
            sout_ref[0, h] = st_ref[h].T


def _hgrn(proj, b_f, lb, norm_w, s0, b, l):
    assert l % HG_BLOCK == 0
    lbr = _tile(l, 128)
    p3 = proj.reshape(b, l, 4 * D_MODEL)
    r = jnp.arange(lbr)
    tri = ((r[:, None] // HG_BLOCK == r[None, :] // HG_BLOCK) & (r[None, :] <= r[:, None])).astype(BF16)
    has_state = s0 is not None
    col = lambda j: pl.BlockSpec((1, lbr, D_MODEL), lambda bi, li: (bi, li, j))
    vec = pl.BlockSpec((1, D_MODEL), lambda bi, li: (0, 0))
    in_specs = [col(0), col(1), col(2), col(3), vec, vec, vec,
                pl.BlockSpec((lbr, lbr), lambda bi, li: (0, 0))]
    args = [p3, p3, p3, p3, b_f.reshape(1, D_MODEL), lb.reshape(1, D_MODEL), norm_w.reshape(1, D_MODEL), tri]
    state_spec = pl.BlockSpec((1, HG_HEADS, HG_DK, HG_DV), lambda bi, li: (bi, 0, 0, 0))
    if has_state:
        in_specs.append(state_spec)
        args.append(s0)
    y, s = pl.pallas_call(
        functools.partial(_hgrn_kernel, lb_rows=lbr, has_state=has_state),
        out_shape=[jax.ShapeDtypeStruct((b, l, D_MODEL), BF16),
                   jax.ShapeDtypeStruct((b, HG_HEADS, HG_DK, HG_DV), F32)],
        grid=(b, l // lbr),
        in_specs=in_specs,
        out_specs=[pl.BlockSpec((1, lbr, D_MODEL), lambda bi, li: (bi, li, 0)), state_spec],
        scratch_shapes=[pltpu.VMEM((HG_HEADS, HG_DV, HG_DK), F32),
                        pltpu.VMEM((lbr, D_MODEL), F32),
                        pltpu.VMEM((lbr, D_MODEL), F32),
                        pltpu.VMEM((lbr, D_MODEL), F32)],
        compiler_params=_params("parallel", "arbitrary"),
        name="hgrn2",
    )(*args)
    return y.reshape(b * l, D_MODEL), s


def _fox_prompt_kernel(q_ref, k_ref, v_ref, cq_ref, ck_ref, o_ref, *, l, tq, tk):
    scale = FOX_HD ** -0.5
    for qi in range(l // tq):
        q0 = qi * tq
        q = q_ref[0, 0, q0:q0 + tq, :]
        cq = cq_ref[0, 0, q0:q0 + tq, :]
        m = jnp.full((tq, 1), -jnp.inf, F32)
        den = jnp.zeros((tq, 1), F32)
        acc = jnp.zeros((tq, FOX_HD), F32)
        for kj in range(l // tk):
            k0 = kj * tk
            if k0 > q0 + tq - 1:
                continue
            s = lax.dot_general(q, k_ref[0, 0, k0:k0 + tk, :], (((1,), (1,)), ((), ())),
                                preferred_element_type=F32) * scale
            s = s + (cq - ck_ref[0, 0, :, k0:k0 + tk])
            if k0 + tk - 1 > q0:
                qpos = q0 + lax.broadcasted_iota(jnp.int32, (tq, tk), 0)
                kpos = k0 + lax.broadcasted_iota(jnp.int32, (tq, tk), 1)
                s = jnp.where(kpos <= qpos, s, -jnp.inf)
            m_new = jnp.maximum(m, jnp.max(s, axis=-1, keepdims=True))
            w = jnp.exp(m - m_new)
            p = jnp.exp(s - m_new)
            den = den * w + jnp.sum(p, axis=-1, keepdims=True)
            acc = acc * w + jnp.dot(p.astype(BF16), v_ref[0, 0, k0:k0 + tk, :], preferred_element_type=F32)
            m = m_new
        o_ref[0, q0:q0 + tq, :] = (acc / den).astype(BF16)


def _fox_prompt_attend(q, k, v, csum, b, l):
    tq = _tile(l, 256)
    tk = _tile(l, 512)
    cq = csum.transpose(0, 2, 1)[..., None]
    ck = csum.transpose(0, 2, 1)[:, :, None, :]
    head = pl.BlockSpec((1, 1, l, FOX_HD), lambda bi, h: (bi, h, 0, 0))
    o = pl.pallas_call(
        functools.partial(_fox_prompt_kernel, l=l, tq=tq, tk=tk),
        out_shape=jax.ShapeDtypeStruct((b, l, D_MODEL), BF16),
        grid=(b, FOX_HEADS),
        in_specs=[head, head, head,
                  pl.BlockSpec((1, 1, l, 1), lambda bi, h: (bi, h, 0, 0)),
                  pl.BlockSpec((1, 1, 1, l), lambda bi, h: (bi, h, 0, 0))],
        out_specs=pl.BlockSpec((1, l, FOX_HD), lambda bi, h: (bi, 0, h)),
        compiler_params=_params("parallel", "parallel"),
        name="fox_prompt_attention",
    )(q, k, v, cq, ck)
    return o.reshape(b * l, D_MODEL)


def _fox_sample_kernel(q_ref, kn_ref, vn_ref, kc_ref, vc_ref, cq_ref, ckc_ref, ckn_ref, o_ref, *, l, pc):
    nh = FOX_HEADS
    lshift = l.bit_length() - 1
    scale = FOX_HD ** -0.5
    contract_last = (((1,), (1,)), ((), ()))
    by_head = lambda ref: jnp.concatenate([ref[0, :, h * FOX_HD:(h + 1) * FOX_HD] for h in range(nh)],
                                          axis=0).astype(BF16)
    q = by_head(q_ref)
    cq = cq_ref[0]
    rows = nh * l
    qhead = jnp.right_shift(lax.broadcasted_iota(jnp.int32, (rows, 1), 0), lshift)
    kn, vn = by_head(kn_ref), by_head(vn_ref)
    col = lax.broadcasted_iota(jnp.int32, (1, rows), 1)
    qframe = jnp.bitwise_and(lax.broadcasted_iota(jnp.int32, (rows, 1), 0), l - 1)
    visible = (jnp.right_shift(col, lshift) == qhead) & (jnp.bitwise_and(col, l - 1) <= qframe)
    s = lax.dot_general(q, kn, contract_last, preferred_element_type=F32) * scale + (cq - ckn_ref[0])
    s = jnp.where(visible, s, -jnp.inf)
    m = jnp.max(s, axis=-1, keepdims=True)
    p = jnp.exp(s - m)
    den = jnp.sum(p, axis=-1, keepdims=True)
    acc = jnp.dot(p.astype(BF16), vn, preferred_element_type=F32)
    same_head = jnp.bitwise_and(lax.broadcasted_iota(jnp.int32, (1, pc * nh), 1), nh - 1) == qhead
    for c in range(kc_ref.shape[1] // (pc * nh)):
        ks = slice(c * pc * nh, (c + 1) * pc * nh)
        s = lax.dot_general(q, kc_ref[0, ks, :].astype(BF16), contract_last,
                            preferred_element_type=F32) * scale + (cq - ckc_ref[0, :, ks])
        s = jnp.where(same_head, s, -jnp.inf)
        m_new = jnp.maximum(m, jnp.max(s, axis=-1, keepdims=True))
        w = jnp.exp(m - m_new)
        p = jnp.exp(s - m_new)
        den = den * w + jnp.sum(p, axis=-1, keepdims=True)
        acc = acc * w + jnp.dot(p.astype(BF16), vc_ref[0, ks, :].astype(BF16), preferred_element_type=F32)
        m = m_new
    out = (acc / den).astype(BF16)
    for h in range(nh):
        o_ref[0, :, h * FOX_HD:(h + 1) * FOX_HD] = out[h * l:(h + 1) * l, :]


def _fox_sample_attend(q, kn, vn, cache_k, cache_v, csum, b, l):
    p = cache_k.shape[1]
    nh = FOX_HEADS
    assert nh & (nh - 1) == 0 and l & (l - 1) == 0
    pc = _tile(p, 256)
    c_new = csum[:, p:, :].transpose(0, 2, 1).reshape(b, nh * l)
    cq = c_new[:, :, None]
    ckn = c_new[:, None, :]
    ckc = csum[:, :p, :].reshape(b, 1, p * nh)
    tok = pl.BlockSpec((1, l, D_MODEL), lambda bi: (bi, 0, 0))
    cache = pl.BlockSpec((1, p * nh, FOX_HD), lambda bi: (bi, 0, 0))
    o = pl.pallas_call(
        functools.partial(_fox_sample_kernel, l=l, pc=pc),
        out_shape=jax.ShapeDtypeStruct((b, l, D_MODEL), BF16),
        grid=(b,),
        in_specs=[tok, tok, tok, cache, cache,
                  pl.BlockSpec((1, nh * l, 1), lambda bi: (bi, 0, 0)),
                  pl.BlockSpec((1, 1, p * nh), lambda bi: (bi, 0, 0)),
                  pl.BlockSpec((1, 1, nh * l), lambda bi: (bi, 0, 0))],
        out_specs=tok,
        compiler_params=_params("parallel"),
        name="fox_sample_attention",
    )(q.reshape(b, l, D_MODEL), kn.reshape(b, l, D_MODEL), vn.reshape(b, l, D_MODEL),
      cache_k.reshape(b, p * nh, FOX_HD), cache_v.reshape(b, p * nh, FOX_HD), cq, ckc, ckn)
    return o.reshape(b * l, D_MODEL)


def _layer_norm(z, g, b):
    mu = jnp.mean(z, axis=-1, keepdims=True)
    zc = z - mu
    var = jnp.mean(zc * zc, axis=-1, keepdims=True)
    return zc * lax.rsqrt(var + LN_EPS) * g + b


def _outproj_norm_kernel(y_ref, w_ref, x_ref, gate_ref, sc_ref, sh_ref, lng_ref, lnb_ref, rw_ref,
                         xn_ref, u_ref, s_ref):
    out = jnp.dot(y_ref[...], w_ref[...], preferred_element_type=F32)
    z = ALPHA * x_ref[...] + (1.0 + gate_ref[0]) * out
    xn = _layer_norm(z, lng_ref[...], lnb_ref[...])
    xn_ref[...] = xn
    u = xn * (1.0 + sc_ref[0]) + sh_ref[0]
    u_ref[...] = u
    s_ref[...] = jax.nn.sigmoid(jnp.dot(u.astype(BF16), rw_ref[...], preferred_element_type=F32))


def _outproj_norm(y, w, x, gate, sc, sh, ln_g, ln_b, rw, tm, bps):
    t, kdim = y.shape
    d = w.shape[1]
    row = pl.BlockSpec((tm, d), lambda i: (i, 0))
    vec = pl.BlockSpec((1, d), lambda i: (0, 0))
    once = pl.Buffered(1)
    return pl.pallas_call(
        _outproj_norm_kernel,
        out_shape=[jax.ShapeDtypeStruct((t, d), F32), jax.ShapeDtypeStruct((t, d), F32),
                   jax.ShapeDtypeStruct((t, LANES), F32)],
        grid=(t // tm,),
        in_specs=[pl.BlockSpec((tm, kdim), lambda i: (i, 0)),
                  pl.BlockSpec((kdim, d), lambda i: (0, 0), pipeline_mode=once),
                  row, _mod_spec(gate, bps), _mod_spec(sc, bps), _mod_spec(sh, bps), vec, vec,
                  pl.BlockSpec((d, LANES), lambda i: (0, 0), pipeline_mode=once)],
        out_specs=[row, row, pl.BlockSpec((tm, LANES), lambda i: (i, 0))],
        compiler_params=_params("parallel"),
        name="outproj_norm",
    )(y, w, x, gate, sc, sh, ln_g.reshape(1, d), ln_b.reshape(1, d), rw)


def _argmax_first(vals):
    best, idx = vals[0], jnp.zeros(vals[0].shape, jnp.int32)
    for j in range(1, len(vals)):
        gt = vals[j] > best
        best = jnp.where(gt, vals[j], best)
        idx = jnp.where(gt, j, idx)
    return best, idx


def _pick(rows, idx):
    out = rows[0]
    for j in range(1, len(rows)):
        out = jnp.where(idx == j, rows[j], out)
    return out


def _route_kernel(s_ref, b_ref, tri_ref, e_ref, r_ref, w_ref, cnt_ref):
    i = pl.program_id(0)

    @pl.when(i == 0)
    def _():
        cnt_ref[...] = jnp.zeros_like(cnt_ref)

    sc = s_ref[...].T[:N_EXPERTS, :]
    sel = sc + b_ref[...]
    row = lambda a, e: a[e:e + 1, :]
    grp = []
    for g in range(N_GROUPS):
        a, b, c, d = (row(sel, g * EXPERTS_PER_GROUP + j) for j in range(EXPERTS_PER_GROUP))
        hi1, lo1, hi2, lo2 = jnp.maximum(a, b), jnp.minimum(a, b), jnp.maximum(c, d), jnp.minimum(c, d)
        grp.append(jnp.maximum(hi1, hi2) + jnp.maximum(jnp.minimum(hi1, hi2), jnp.maximum(lo1, lo2)))
    _, gidx = _argmax_first(grp)
    member = lambda a: [_pick([row(a, g * EXPERTS_PER_GROUP + j) for g in range(N_GROUPS)], gidx)
                        for j in range(EXPERTS_PER_GROUP)]
    v, c = member(sel), member(sc)
    _, j0 = _argmax_first(v)
    _, j1 = _argmax_first([jnp.where(j0 == j, -jnp.inf, v[j]) for j in range(EXPERTS_PER_GROUP)])
    c0, c1 = _pick(c, j0), _pick(c, j1)
    den = c0 + c1
    e0 = gidx * EXPERTS_PER_GROUP + j0
    e1 = gidx * EXPERTS_PER_GROUP + j1
    eio = lax.broadcasted_iota(jnp.int32, sc.shape, 0)
    oh0, oh1 = eio == e0, eio == e1
    member_f = jnp.where(oh0 | oh1, 1.0, 0.0)
    before = jnp.dot(member_f.astype(BF16), tri_ref[...], preferred_element_type=F32) + cnt_ref[...]
    r0 = jnp.sum(jnp.where(oh0, before, 0.0), axis=0, keepdims=True)
    r1 = jnp.sum(jnp.where(oh1, before, 0.0), axis=0, keepdims=True)
    cnt_ref[...] += jnp.sum(member_f, axis=1, keepdims=True)
    e_ref[0:1, :] = e0
    e_ref[1:2, :] = e1
    r_ref[0:1, :] = r0.astype(jnp.int32)
    r_ref[1:2, :] = r1.astype(jnp.int32)
    w_ref[0:1, :] = c0 / den
    w_ref[1:2, :] = c1 / den


def _route(scores, router_b, tm_e):
    t = scores.shape[0]
    tm = _tile(t, 512)
    r = jnp.arange(tm)
    tri = (r[:, None] < r[None, :]).astype(BF16)
    slot = pl.BlockSpec((TOP_K, tm), lambda i: (0, i))
    eidx, rank, wts, counts = pl.pallas_call(
        _route_kernel,
        out_shape=[jax.ShapeDtypeStruct((TOP_K, t), jnp.int32), jax.ShapeDtypeStruct((TOP_K, t), jnp.int32),
                   jax.ShapeDtypeStruct((TOP_K, t), F32), jax.ShapeDtypeStruct((N_EXPERTS, 1), F32)],
        grid=(t // tm,),
        in_specs=[pl.BlockSpec((tm, LANES), lambda i: (i, 0)),
                  pl.BlockSpec((N_EXPERTS, 1), lambda i: (0, 0)),
                  pl.BlockSpec((tm, tm), lambda i: (0, 0))],
        out_specs=[slot, slot, slot, pl.BlockSpec((N_EXPERTS, 1), lambda i: (0, 0))],
        compiler_params=_params("arbitrary"),
        name="moe_route",
    )(scores, router_b.astype(F32).reshape(N_EXPERTS, 1), tri)
    counts = counts[:, 0].astype(jnp.int32)
    padded = (counts + tm_e - 1) // tm_e * tm_e
    pad_end = jnp.cumsum(padded)
    pad_start = pad_end - padded
    onehot = eidx[:, :, None] == jnp.arange(N_EXPERTS, dtype=jnp.int32)[None, None, :]
    dest = (jnp.sum(jnp.where(onehot, pad_start[None, None, :], 0), axis=-1) + rank).reshape(TOP_K * t)
    n_blocks = (t * TOP_K + N_EXPERTS * (tm_e - 1) + tm_e - 1) // tm_e
    blk_row = jnp.arange(n_blocks, dtype=jnp.int32) * tm_e
    blk_e = jnp.minimum(jnp.sum((pad_end[None, :] <= blk_row[:, None]).astype(jnp.int32), axis=1), N_EXPERTS - 1)
    n_used = (pad_end[-1] // tm_e).astype(jnp.int32).reshape(1)
    last_blk = jnp.concatenate([jnp.where(counts > 0, pad_end - tm_e, -1).astype(jnp.int32), n_used])
    return dest.astype(jnp.int32), wts.T, blk_e, n_used, n_blocks, last_blk


def _dispatch_kernel(dest_ref, last_ref, u_ref, xs_ref, zero_ref, sem, zsem, *, tb, t_total, tm_e, n_blocks):
    base = pl.program_id(0) * tb

    @pl.when(pl.program_id(0) == 0)
    def _():
        zero_ref[...] = jnp.zeros_like(zero_ref)

        def zero_copy(e):
            row = pl.multiple_of(jnp.maximum(last_ref[e], 0), tm_e)
            return pltpu.make_async_copy(zero_ref, xs_ref.at[pl.ds(row, tm_e), :], zsem)

        for e in range(N_EXPERTS):
            pl.when(last_ref[e] >= 0)(lambda e=e: zero_copy(e).start())
        for e in range(N_EXPERTS):
            pl.when(last_ref[e] >= 0)(lambda e=e: zero_copy(e).wait())

        def tail_copy(blk):
            return pltpu.make_async_copy(zero_ref, xs_ref.at[pl.ds(pl.multiple_of(blk * tm_e, tm_e), tm_e), :], zsem)

        n_used = last_ref[N_EXPERTS]
        lax.fori_loop(n_used, n_blocks, lambda blk, c: (tail_copy(blk).start(), c)[1], 0)
        lax.fori_loop(n_used, n_blocks, lambda blk, c: (tail_copy(blk).wait(), c)[1], 0)

    def row_copy(t, slot):
        d = dest_ref[slot * t_total + base + t]
        return pltpu.make_async_copy(u_ref.at[pl.ds(t, 1), :], xs_ref.at[pl.ds(d, 1), :], sem)

    def issue(t, carry):
        for slot in range(TOP_K):
            row_copy(t, slot).start()
        return carry

    lax.fori_loop(0, tb, issue, 0, unroll=DMA_UNROLL)
    for slot in range(TOP_K):
        pltpu.make_async_copy(u_ref, xs_ref.at[pl.ds(0, tb), :], sem).wait()


def _dispatch(u, dest, last_blk, n_rows, tb, tm_e):
    t, d = u.shape
    return pl.pallas_call(
        functools.partial(_dispatch_kernel, tb=tb, t_total=t, tm_e=tm_e, n_blocks=n_rows // tm_e),
        out_shape=jax.ShapeDtypeStruct((n_rows, d), F32),
        grid_spec=pltpu.PrefetchScalarGridSpec(
            num_scalar_prefetch=2,
            grid=(t // tb,),
            in_specs=[pl.BlockSpec((tb, d), lambda i, dest_ref, last_ref: (i, 0))],
            out_specs=pl.BlockSpec(memory_space=pl.ANY),
            scratch_shapes=[pltpu.VMEM((tm_e, d), F32), pltpu.SemaphoreType.DMA, pltpu.SemaphoreType.DMA],
        ),
        compiler_params=_params("arbitrary"),
        name="moe_dispatch",
    )(dest, last_blk, u)


def _expert_kernel(blk_e_ref, n_used_ref, xs_ref, wg_ref, wu_ref, wd_ref, ys_ref):
    del blk_e_ref
    i = pl.program_id(0)

    @pl.when(i < n_used_ref[0])
    def _():
        x = xs_ref[...].astype(BF16)
        g = jnp.dot(x, wg_ref[0], preferred_element_type=F32)
        up = jnp.dot(x, wu_ref[0], preferred_element_type=F32)
        hid = (g * jax.nn.sigmoid(g) * up).astype(BF16)
        ys_ref[...] = jnp.dot(hid, wd_ref[0], preferred_element_type=F32)

    @pl.when(i >= n_used_ref[0])
    def _():
        ys_ref[...] = jnp.zeros_like(ys_ref)


def _experts(xs, blk_e, n_used, wg, wu, wd, tm_e):
    r, d = xs.shape
    de = wg.shape[2]
    return pl.pallas_call(
        _expert_kernel,
        out_shape=jax.ShapeDtypeStruct((r, d), F32),
        grid_spec=pltpu.PrefetchScalarGridSpec(
            num_scalar_prefetch=2,
            grid=(r // tm_e,),
            in_specs=[pl.BlockSpec((tm_e, d), lambda i, be, nu: (i, 0)),
                      pl.BlockSpec((1, d, de), lambda i, be, nu: (be[i], 0, 0)),
                      pl.BlockSpec((1, d, de), lambda i, be, nu: (be[i], 0, 0)),
                      pl.BlockSpec((1, de, d), lambda i, be, nu: (be[i], 0, 0))],
            out_specs=pl.BlockSpec((tm_e, d), lambda i, be, nu: (i, 0)),
        ),
        compiler_params=_params("arbitrary"),
        name="moe_experts",
    )(blk_e, n_used, xs, wg, wu, wd)


def _combine_norm_kernel(dest_ref, ys_ref, wts_ref, x_ref, gate_ref, sc_ref, sh_ref, lng_ref, lnb_ref,
                         xn_ref, u_ref, y0_ref, y1_ref, sems, *, tb, t_total):
    i = pl.program_id(0)
    par = i % 2
    bufs = (y0_ref, y1_ref)

    def start_gathers(blk, p):
        base = blk * tb

        def issue(t, carry):
            for slot in range(TOP_K):
                d = dest_ref[slot * t_total + base + t]
                pltpu.make_async_copy(ys_ref.at[pl.ds(d, 1), :], bufs[slot].at[p, pl.ds(t, 1), :],
                                      sems.at[p]).start()
            return carry

        lax.fori_loop(0, tb, issue, 0, unroll=DMA_UNROLL)

    @pl.when(i == 0)
    def _():
        start_gathers(0, 0)

    @pl.when(i + 1 < pl.num_programs(0))
    def _():
        start_gathers(i + 1, 1 - par)

    for buf in bufs:
        pltpu.make_async_copy(ys_ref.at[pl.ds(0, tb), :], buf.at[par], sems.at[par]).wait()
    w = wts_ref[...]
    ffn = w[:, 0:1] * y0_ref[par] + w[:, 1:2] * y1_ref[par]
    z = ALPHA * x_ref[...] + (1.0 + gate_ref[0]) * ffn
    xn = _layer_norm(z, lng_ref[...], lnb_ref[...])
    xn_ref[...] = xn
    u_ref[...] = xn * (1.0 + sc_ref[0]) + sh_ref[0]


def _combine_norm(ys, dest, wts, x, gate, sc, sh, ln_g, ln_b, tb, bps):
    t, d = x.shape
    row = pl.BlockSpec((tb, d), lambda i, dr: (i, 0))
    vec = pl.BlockSpec((1, d), lambda i, dr: (0, 0))
    return pl.pallas_call(
        functools.partial(_combine_norm_kernel, tb=tb, t_total=t),
        out_shape=[jax.ShapeDtypeStruct((t, d), F32), jax.ShapeDtypeStruct((t, d), F32)],
        grid_spec=pltpu.PrefetchScalarGridSpec(
            num_scalar_prefetch=1,
            grid=(t // tb,),
            in_specs=[pl.BlockSpec(memory_space=pl.ANY),
                      pl.BlockSpec((tb, TOP_K), lambda i, dr: (i, 0)),
                      row, _mod_spec(gate, bps), _mod_spec(sc, bps), _mod_spec(sh, bps), vec, vec],
            out_specs=[row, row],
            scratch_shapes=[pltpu.VMEM((2, tb, d), F32), pltpu.VMEM((2, tb, d), F32),
                            pltpu.SemaphoreType.DMA((2,))],
        ),
        compiler_params=_params("arbitrary"),
        name="moe_combine_norm",
    )(dest, ys, wts, x, gate, sc, sh, ln_g.reshape(1, d), ln_b.reshape(1, d))


def _moe_block(x, u, scores, router_b, wg, wu, wd, gate, sc, sh, ln_g, ln_b, tm, bps, tm_e):
    dest, wts, blk_e, n_used, n_blocks, last_blk = _route(scores, router_b, tm_e)
    xs = _dispatch(u, dest, last_blk, n_blocks * tm_e, tm, tm_e)
    ys = _experts(xs, blk_e, n_used, wg, wu, wd, tm_e)
    tb = _tile(tm, 256) if gate.shape[1] == 1 else tm
    return _combine_norm(ys, dest, wts, x, gate, sc, sh, ln_g, ln_b, tb, bps * (tm // tb))


def _rope_tables(l, pos0):
    half = RET_DK // 2
    inv = ROPE_BASE ** (-jnp.arange(half, dtype=F32) / half)
    ang = (pos0 + jnp.arange(l)).astype(F32)[:, None] * inv[None, :]
    return jnp.cos(ang), jnp.sin(ang)


def _stream(x3, mods, wts, state_ret, state_hgrn, cache_k, cache_v, cache_logf, pos0, tm_e):
    b, l, d = x3.shape
    t = b * l
    fresh = state_ret is None
    if fresh:
        tm = _tile(l, 256)
        bps = l // tm
        expand = lambda m: m[:, None, :]
    else:
        tm = _tile(t, 256)
        bps = 1
        expand = lambda m: jnp.repeat(m, l, axis=0).reshape(t // tm, tm, d)
    x = x3.reshape(t, d)
    outs = dict(ret=[], hg=[], fk=[], fv=[], fl=[])
    log_gamma = jnp.log1p(-jnp.exp2(-5.0 - jnp.arange(RET_HEADS, dtype=F32)))
    cos, sin = _rope_tables(l, pos0)
    u = None
    for layer in range(DEPTH):
        m = [expand(a) for a in mods[layer]]
        kind, j = layer % N_MIXERS, layer // N_MIXERS
        if kind == 0:
            if layer == 0:
                proj = _inproj(x, wts['ret_w_in'][j], (m[1], m[0], l if fresh else None))
            else:
                proj = _inproj(u, wts['ret_w_in'][j])
            s0 = None if fresh else state_ret[j]
            y, s = _retention(proj, cos, sin, log_gamma, wts['ret_gn_w'][j], s0, b, l)
            outs['ret'].append(s)
            w_out = wts['ret_w_out'][j]
        elif kind == 1:
            proj = _inproj(u, wts['hg_w_in'][j])
            s0 = None if fresh else state_hgrn[j]
            y, s = _hgrn(proj, wts['hg_b_f'][j], wts['lbs'][layer], wts['hg_norm_w'][j], s0, b, l)
            outs['hg'].append(s)
            w_out = wts['hg_w_out'][j]
        else:
            wq, wk, wv, wf, bf = wts['fox_in'][j]
            logf, csum = (a[:, :FOX_HEADS].reshape(b, l, FOX_HEADS) for a in _fox_gate(u, wf, bf, b, l))
            if fresh:
                tf = _tile(l, 512)
                (qh,) = _fox_inproj(u, wq, b, l, tf, False, True)
                kt, kh = _fox_inproj(u, wk, b, l, tf, True, True)
                vt, vh = _fox_inproj(u, wv, b, l, tf, True, True)
                y = _fox_prompt_attend(qh, kh, vh, csum, b, l)
            else:
                qt = _inproj(u, wq)
                kt = _inproj(u, wk)
                vt = _inproj(u, wv)
                lf_all = jnp.concatenate([cache_logf[j].astype(F32), logf], axis=1)
                pos = jnp.arange(lf_all.shape[1])
                csum = jnp.einsum('ts,bsh->bth', (pos[None, :] <= pos[:, None]).astype(F32), lf_all,
                                  precision=lax.Precision.HIGHEST)
                y = _fox_sample_attend(qt, kt, vt, cache_k[j], cache_v[j], csum, b, l)
            outs['fk'].append(kt.reshape(b, l, FOX_HEADS, FOX_HD))
            outs['fv'].append(vt.reshape(b, l, FOX_HEADS, FOX_HD))
            outs['fl'].append(logf)
            w_out = wts['fox_w_out'][j]
        x, u, scores = _outproj_norm(y, w_out, x, m[2], m[4], m[3], wts['ln_mix_g'][layer],
                                     wts['ln_mix_b'][layer], wts['router_w'], tm, bps)
        nxt = [expand(a) for a in mods[min(layer + 1, DEPTH - 1)]]
        x, u = _moe_block(x, u, scores, wts['router_b'], wts['moe_w_gate'][layer], wts['moe_w_up'][layer],
                          wts['moe_w_down'][layer], m[5], nxt[1], nxt[0], wts['ln_ffn_g'][layer],
                          wts['ln_ffn_b'][layer], tm, bps, tm_e)
    return x.reshape(b, l, d), outs


def kernel(x_prompt, x_sample, state_ret, state_hgrn, cache_fox_k, cache_fox_v, cache_fox_logf, c_prompt, c_sample, ada_w, ada_b, ln_mix_g, ln_mix_b, ln_ffn_g, ln_ffn_b, ret_w_in, ret_gn_w, ret_w_out, hg_w_in, hg_b_f, hg_lower_bounds, hg_norm_w, hg_w_out, fox_w_in, fox_b_f, fox_w_out, router_w, router_b, moe_w_gate, moe_w_up, moe_w_down):
    dt = x_prompt.dtype
    d = D_MODEL
    nbp = c_prompt.shape[0]
    lbs = jnp.cumsum(jax.nn.softmax(hg_lower_bounds.astype(F32), axis=0), axis=0)
    lbs = lbs - lbs[0]
    mod_all = _modulation_all(jnp.concatenate([c_prompt, c_sample], axis=0).astype(F32), ada_w, ada_b)
    split6 = lambda m: [m[:, i * d:(i + 1) * d] for i in range(6)]
    mods_p = [split6(mod_all[layer, :nbp]) for layer in range(DEPTH)]
    mods_s = [split6(mod_all[layer, nbp:]) for layer in range(DEPTH)]
    pad = LANES - FOX_HEADS
    fox_in = [(fox_w_in[j, :, :d].astype(BF16), fox_w_in[j, :, d:2 * d].astype(BF16),
               fox_w_in[j, :, 2 * d:3 * d].astype(BF16),
               jnp.pad(fox_w_in[j, :, 3 * d:], ((0, 0), (0, pad))).astype(BF16),
               jnp.pad(fox_b_f[j].astype(F32), (0, pad)).reshape(1, LANES))
              for j in range(fox_w_in.shape[0])]
    wts = dict(
        ret_w_in=ret_w_in.astype(BF16), ret_gn_w=ret_gn_w.astype(F32), ret_w_out=ret_w_out.astype(BF16),
        hg_w_in=hg_w_in.astype(BF16), hg_b_f=hg_b_f.astype(F32), lbs=lbs, hg_norm_w=hg_norm_w.astype(F32),
        hg_w_out=hg_w_out.astype(BF16), fox_in=fox_in, fox_w_out=fox_w_out.astype(BF16),
        router_w=jnp.pad(router_w, ((0, 0), (0, LANES - N_EXPERTS))).astype(BF16), router_b=router_b,
        moe_w_gate=moe_w_gate.astype(BF16), moe_w_up=moe_w_up.astype(BF16), moe_w_down=moe_w_down.astype(BF16),
        ln_mix_g=ln_mix_g.astype(F32), ln_mix_b=ln_mix_b.astype(F32),
        ln_ffn_g=ln_ffn_g.astype(F32), ln_ffn_b=ln_ffn_b.astype(F32))
    past_len = cache_fox_k.shape[2]
    yp, op = _stream(x_prompt, mods_p, wts, None, None, None, None, None, 0, 256)
    ys, os_ = _stream(x_sample, mods_s, wts, state_ret, state_hgrn, cache_fox_k, cache_fox_v,
                      cache_fox_logf, past_len, 128)
    st = lambda xs: jnp.stack(xs).astype(dt)
    return (yp, ys, st(op['ret']), st(os_['ret']), st(op['hg']), st(os_['hg']),
            st(op['fk']), st(op['fv']), st(op['fl']), st(os_['fk']), st(os_['fv']), st(os_['fl']))
```

```python
import functools

import jax
import jax.numpy as jnp
from jax import lax
from jax.experimental import pallas as pl
from jax.experimental.pallas import tpu as pltpu

F32 = jnp.float32
BF16 = jnp.bfloat16

D_MODEL = 2048
DEPTH = 4
CHUNK = 64
N_MIXERS = 3
RET_HEADS = 8
RET_DK = D_MODEL // RET_HEADS
RET_DV = 2 * RET_DK
RET_QK = RET_HEADS * RET_DK
RET_V = RET_HEADS * RET_DV
RET_GROUP = 4
ROPE_BASE = 10000.0
HG_DK = 128
HG_HEADS = D_MODEL // HG_DK
HG_DV = D_MODEL // HG_HEADS
HG_BLOCK = 16
FOX_HEADS = 16
FOX_HD = D_MODEL // FOX_HEADS
N_EXPERTS = 16
N_GROUPS = 4
EXPERTS_PER_GROUP = N_EXPERTS // N_GROUPS
TOPK_GROUP = 1
TOP_K = 2
D_EXPERT = D_MODEL // 2
ALPHA = (2 * DEPTH) ** 0.25
LN_EPS = 1e-5
NORM_EPS = 1e-6
LOG2E = 1.4426950408889634

LANES = 128
VMEM_LIMIT = 56 * 1024 * 1024
DMA_UNROLL = 8


def _params(*sem):
    return pltpu.CompilerParams(dimension_semantics=sem, vmem_limit_bytes=VMEM_LIMIT)


def _tile(n, pref):
    t = min(n, pref)
    while n % t:
        t //= 2
    return t


def _mod_kernel(c_ref, w_ref, b_ref, o_ref):
    c = c_ref[...]
    a = (c * jax.nn.sigmoid(c)).astype(BF16)
    o_ref[0] = jnp.dot(a, w_ref[0].astype(BF16), preferred_element_type=F32) + b_ref[0]


def _modulation_all(c_all, ada_w, ada_b):
    nb = c_all.shape[0]
    depth, d, n = ada_w.shape
    tn = _tile(n, 1024)
    return pl.pallas_call(
        _mod_kernel,
        out_shape=jax.ShapeDtypeStruct((depth, nb, n), F32),
        grid=(depth, n // tn),
        in_specs=[pl.BlockSpec((nb, d), lambda l, j: (0, 0)),
                  pl.BlockSpec((1, d, tn), lambda l, j: (l, 0, j)),
                  pl.BlockSpec((1, 1, tn), lambda l, j: (l, 0, j))],
        out_specs=pl.BlockSpec((1, nb, tn), lambda l, j: (l, 0, j)),
        compiler_params=_params("parallel", "parallel"),
        name="modulation",
    )(c_all, ada_w, ada_b.reshape(depth, 1, n))


def _mod_spec(mod, bps):
    return pl.BlockSpec((1,) + mod.shape[1:], lambda i, *_: (i // bps, 0, 0))


def _inproj_kernel(x_ref, w_ref, *rest, modulated):
    if modulated:
        sc_ref, sh_ref, o_ref, xb_ref = rest
    else:
        o_ref, xb_ref = rest

    @pl.when(pl.program_id(1) == 0)
    def _():
        x = x_ref[...]
        if modulated:
            x = x * (1.0 + sc_ref[0]) + sh_ref[0]
        xb_ref[...] = x.astype(BF16)

    o_ref[...] = jnp.dot(xb_ref[...], w_ref[...], preferred_element_type=F32)


def _inproj(u, w, mod=None):
    t, d = u.shape
    n = w.shape[1]
    tm = _tile(t, 1024)
    tn = _tile(n, 1024)
    in_specs = [pl.BlockSpec((tm, d), lambda i, j: (i, 0)),
                pl.BlockSpec((d, tn), lambda i, j: (0, j))]
    args = [u, w]
    if mod is not None:
        sc, sh, seq_rows = mod
        if seq_rows is None:
            sc, sh, bps = sc.reshape(t // tm, tm, d), sh.reshape(t // tm, tm, d), 1
        else:
            tm = _tile(seq_rows, tm)
            bps = seq_rows // tm
            in_specs[0] = pl.BlockSpec((tm, d), lambda i, j: (i, 0))
        in_specs += [_mod_spec(sc, bps), _mod_spec(sh, bps)]
        args += [sc, sh]
    return pl.pallas_call(
        functools.partial(_inproj_kernel, modulated=mod is not None),
        out_shape=jax.ShapeDtypeStruct((t, n), F32),
        grid=(t // tm, n // tn),
        in_specs=in_specs,
        out_specs=pl.BlockSpec((tm, tn), lambda i, j: (i, j)),
        scratch_shapes=[pltpu.VMEM((tm, d), BF16)],
        compiler_params=_params("parallel", "arbitrary"),
        name="inproj",
    )(*args)


def _fox_inproj_kernel(x_ref, w_ref, *outs, tok, heads):
    acc = jnp.dot(x_ref[...].astype(BF16), w_ref[...], preferred_element_type=F32)
    n = 0
    if tok:
        outs[n][...] = acc
        n += 1
    if heads:
        for h in range(FOX_HEADS):
            outs[n][0, h] = acc[:, h * FOX_HD:(h + 1) * FOX_HD].astype(BF16)


def _fox_inproj(u, w, b, l, tm, tok, heads):
    t, d = u.shape
    bps = l // tm
    out_shape, out_specs = [], []
    if tok:
        out_shape.append(jax.ShapeDtypeStruct((t, d), F32))
        out_specs.append(pl.BlockSpec((tm, d), lambda i: (i, 0)))
    if heads:
        out_shape.append(jax.ShapeDtypeStruct((b, FOX_HEADS, l, FOX_HD), BF16))
        out_specs.append(pl.BlockSpec((1, FOX_HEADS, tm, FOX_HD), lambda i: (i // bps, 0, i % bps, 0)))
    return pl.pallas_call(
        functools.partial(_fox_inproj_kernel, tok=tok, heads=heads),
        out_shape=out_shape,
        grid=(t // tm,),
        in_specs=[pl.BlockSpec((tm, d), lambda i: (i, 0)),
                  pl.BlockSpec((d, d), lambda i: (0, 0))],
        out_specs=out_specs,
        compiler_params=_params("parallel"),
        name="fox_inproj",
    )(u, w)


def _split3(x):
    hi = x.astype(BF16)
    r1 = x - hi.astype(F32)
    mid = r1.astype(BF16)
    lo = (r1 - mid.astype(F32)).astype(BF16)
    return hi, mid, lo


def _tri_cumsum(tri, x):
    hi, mid, lo = _split3(x)
    return (jnp.dot(tri, hi, preferred_element_type=F32) + jnp.dot(tri, mid, preferred_element_type=F32)
            + jnp.dot(tri, lo, preferred_element_type=F32))


def _fox_gate_kernel(x_ref, w_ref, b_ref, tri_ref, o_ref, c_ref, carry_ref):
    @pl.when(pl.program_id(1) == 0)
    def _():
        carry_ref[...] = jnp.zeros_like(carry_ref)

    z = jnp.dot(x_ref[...].astype(BF16), w_ref[...], preferred_element_type=F32) + b_ref[...]
    logf = jnp.minimum(z, 0.0) - jnp.log1p(jnp.exp(-jnp.abs(z)))
    o_ref[...] = logf
    csum = _tri_cumsum(tri_ref[...], logf) + carry_ref[...]
    c_ref[...] = csum
    carry_ref[...] = csum[csum.shape[0] - 1:, :]


def _fox_gate(u, w, b, nb, l):
    t, d = u.shape
    tm = _tile(l, 256)
    bps = l // tm
    r = jnp.arange(tm)
    tri = (r[None, :] <= r[:, None]).astype(BF16)
    row = pl.BlockSpec((tm, LANES), lambda bi, li: (bi * bps + li, 0))
    return pl.pallas_call(
        _fox_gate_kernel,
        out_shape=[jax.ShapeDtypeStruct((t, LANES), F32), jax.ShapeDtypeStruct((t, LANES), F32)],
        grid=(nb, bps),
        in_specs=[pl.BlockSpec((tm, d), lambda bi, li: (bi * bps + li, 0)),
                  pl.BlockSpec((d, LANES), lambda bi, li: (0, 0)),
                  pl.BlockSpec((1, LANES), lambda bi, li: (0, 0)),
                  pl.BlockSpec((tm, tm), lambda bi, li: (0, 0))],
        out_specs=[row, row],
        scratch_shapes=[pltpu.VMEM((1, LANES), F32)],
        compiler_params=_params("parallel", "arbitrary"),
        name="fox_gate",
    )(u, w, b, tri)


def _retention_kernel(lg_ref, q_ref, k_ref, v_ref, g_ref, cos_ref, sin_ref, gn_ref, dec_ref, *rest,
                      lb, has_state):
    if has_state:
        s0_ref, y_ref, sout_ref, s_ref = rest
    else:
        y_ref, sout_ref, s_ref = rest
    hg = pl.program_id(1)
    li = pl.program_id(2)

    @pl.when(li == 0)
    def _():
        if has_state:
            s_ref[...] = s0_ref[0]
        else:
            s_ref[...] = jnp.zeros_like(s_ref)

    half = RET_DK // 2
    idx = lax.broadcasted_iota(jnp.int32, (lb, 1), 0).astype(F32)
    cos, sin = cos_ref[...], sin_ref[...]

    def rope(x):
        x1, x2 = x[:, :half], x[:, half:]
        return jnp.concatenate([x1 * cos - x2 * sin, x1 * sin + x2 * cos], axis=-1)

    for j in range(RET_GROUP):
        lg = lg_ref[hg * RET_GROUP + j]
        q_dec = jnp.exp((idx + 1.0) * lg)
        k_dec = jnp.exp((lb - 1.0 - idx) * lg)
        s_dec = jnp.exp(jnp.full((1, 1), lb, F32) * lg)
        qk_cols = slice(j * RET_DK, (j + 1) * RET_DK)
        v_cols = slice(j * RET_DV, (j + 1) * RET_DV)
        q = rope(q_ref[0, :, qk_cols])
        k = rope(k_ref[0, :, qk_cols]) * (RET_DK ** -0.5)
        vb = v_ref[0, :, v_cols].astype(BF16)
        scores = lax.dot_general(q.astype(BF16), k.astype(BF16), (((1,), (1,)), ((), ())),
                                 preferred_element_type=F32) * dec_ref[j]
        s = s_ref[j]
        o = (jnp.dot(scores.astype(BF16), vb, preferred_element_type=F32)
             + jnp.dot((q * q_dec).astype(BF16), s.astype(BF16), preferred_element_type=F32))
        kd = (k * k_dec).T.astype(BF16)
        s_ref[j] = s * s_dec + jnp.dot(kd, vb, preferred_element_type=F32)
        mu = jnp.mean(o, axis=-1, keepdims=True)
        oc = o - mu
        var = jnp.mean(oc * oc, axis=-1, keepdims=True)
        y = oc * lax.rsqrt(var + NORM_EPS) * gn_ref[:, v_cols]
        g = g_ref[0, :, v_cols]
        y_ref[0, :, v_cols] = (g * jax.nn.sigmoid(g) * y).astype(BF16)

    @pl.when(li == pl.num_programs(2) - 1)
    def _():
        sout_ref[0] = s_ref[...]


def _retention(proj, cos, sin, log_gamma, gn_w, s0, b, l):
    cl = min(l, CHUNK)
    lb = _tile(l, 4 * cl)
    p3 = proj.reshape(b, l, proj.shape[1])
    ng = RET_HEADS // RET_GROUP
    qw, vw = RET_GROUP * RET_DK, RET_GROUP * RET_DV
    has_state = s0 is not None
    in_specs = [pl.BlockSpec(memory_space=pltpu.SMEM),
                pl.BlockSpec((1, lb, qw), lambda bi, h, li: (bi, li, h)),
                pl.BlockSpec((1, lb, qw), lambda bi, h, li: (bi, li, ng + h)),
                pl.BlockSpec((1, lb, vw), lambda bi, h, li: (bi, li, ng + h)),
                pl.BlockSpec((1, lb, vw), lambda bi, h, li: (bi, li, 2 * ng + h)),
                pl.BlockSpec((lb, RET_DK // 2), lambda bi, h, li: (li, 0)),
                pl.BlockSpec((lb, RET_DK // 2), lambda bi, h, li: (li, 0)),
                pl.BlockSpec((1, vw), lambda bi, h, li: (0, h)),
                pl.BlockSpec((RET_GROUP, lb, lb), lambda bi, h, li: (h, 0, 0))]
    pos = jnp.arange(lb)
    dt = pos[:, None] - pos[None, :]
    same = (pos[:, None] // cl) == (pos[None, :] // cl)
    dist = jnp.where(same, jnp.abs(dt), dt).astype(F32)
    decay = jnp.where((same | (dt > 0))[None], jnp.exp(dist[None] * log_gamma[:, None, None]), 0.0)
    args = [log_gamma, p3, p3, p3, p3, cos, sin, gn_w.reshape(1, RET_V), decay]
    state_spec = pl.BlockSpec((1, RET_GROUP, RET_DK, RET_DV), lambda bi, h, li: (bi, h, 0, 0))
    if has_state:
        in_specs.append(state_spec)
        args.append(s0)
    y, s = pl.pallas_call(
        functools.partial(_retention_kernel, lb=lb, has_state=has_state),
        out_shape=[jax.ShapeDtypeStruct((b, l, RET_V), BF16),
                   jax.ShapeDtypeStruct((b, RET_HEADS, RET_DK, RET_DV), F32)],
        grid=(b, ng, l // lb),
        in_specs=in_specs,
        out_specs=[pl.BlockSpec((1, lb, vw), lambda bi, h, li: (bi, li, h)), state_spec],
        scratch_shapes=[pltpu.VMEM((RET_GROUP, RET_DK, RET_DV), F32)],
        compiler_params=_params("parallel", "parallel", "arbitrary"),
        name="retention",
    )(*args)
    return y.reshape(b * l, RET_V), s


def _hgrn_kernel(q_ref, fz_ref, v_ref, g_ref, bf_ref, lb_ref, nw_ref, tri_ref, *rest, lb_rows, has_state):
    if has_state:
        s0_ref, y_ref, sout_ref, st_ref, gc_ref, k_ref, o_ref = rest
    else:
        y_ref, sout_ref, st_ref, gc_ref, k_ref, o_ref = rest
    li = pl.program_id(1)
    hb = HG_BLOCK
    half = hb // 2

    @pl.when(li == 0)
    def _():
        for h in range(HG_HEADS):
            if has_state:
                st_ref[h] = s0_ref[0, h].T
            else:
                st_ref[h] = jnp.zeros((HG_DV, HG_DK), F32)

    lbv = lb_ref[...]
    f = lbv + (1.0 - lbv) * jax.nn.sigmoid(fz_ref[0] + bf_ref[...])
    logf = jnp.log(f)
    k_ref[...] = 1.0 - f
    gc_ref[...] = _tri_cumsum(tri_ref[...], logf) * LOG2E

    rt = lax.broadcasted_iota(jnp.int32, (half, 1), 0)
    contract_last = (((1,), (1,)), ((), ()))

    def block(bi, carry):
        r0 = pl.multiple_of(bi * hb, hb)
        for h in range(HG_HEADS):
            cs = slice(h * HG_DK, (h + 1) * HG_DK)
            gb = gc_ref[pl.ds(r0, hb), cs]
            qb = q_ref[0, pl.ds(r0, hb), cs]
            kb = k_ref[pl.ds(r0, hb), cs]
            vb = v_ref[0, pl.ds(r0, hb), cs]
            q_top, q_bot = qb[:half], qb[half:]
            g_top, g_bot = gb[:half], gb[half:]
            gk = gb - jnp.log2(kb)
            i_top = jnp.zeros((half, HG_DV), F32)
            i_bot = jnp.zeros((half, HG_DV), F32)
            for s in range(hb):
                gs, vs = gk[s:s + 1], vb[s:s + 1]
                if s < half:
                    e = jnp.where(rt >= s, jnp.exp2(g_top - gs), 0.0)
                    a = jnp.sum(q_top * e, axis=-1, keepdims=True)
                    i_top = i_top + a * vs
                    e = jnp.exp2(g_bot - gs)
                else:
                    e = jnp.where(rt + half >= s, jnp.exp2(g_bot - gs), 0.0)
                a = jnp.sum(q_bot * e, axis=-1, keepdims=True)
                i_bot = i_bot + a * vs
            intra = jnp.concatenate([i_top, i_bot], axis=0)
            st = st_ref[h]
            qt = (qb * jnp.exp2(gb)).astype(BF16)
            inter = lax.dot_general(qt, st.astype(BF16), contract_last, preferred_element_type=F32)
            o_ref[pl.ds(r0, hb), cs] = intra + inter
            gl = gb[hb - 1:hb]
            kt = (kb * jnp.exp2(gl - gb)).astype(BF16)
            upd = jnp.dot(vb.T.astype(BF16), kt, preferred_element_type=F32)
            st_ref[h] = st * jnp.exp2(gl) + upd
        return carry

    lax.fori_loop(0, lb_rows // hb, block, 0)

    for h in range(HG_HEADS):
        cs = slice(h * HG_DK, (h + 1) * HG_DK)
        oh = o_ref[:, cs]
        on = oh * lax.rsqrt(jnp.mean(oh * oh, axis=-1, keepdims=True) + NORM_EPS)
        g = g_ref[0, :, cs]
        y_ref[0, :, cs] = (on * nw_ref[:, cs] * (g * jax.nn.sigmoid(g))).astype(BF16)

    @pl.when(li == pl.num_programs(1) - 1)
    def _():
        for h in range(HG_HEADS):
            sout_ref[0, h] = st_ref[h].T


def _hgrn(proj, b_f, lb, norm_w, s0, b, l):
    assert l % HG_BLOCK == 0
    lbr = _tile(l, 256)
    p3 = proj.reshape(b, l, 4 * D_MODEL)
    r = jnp.arange(lbr)
    tri = ((r[:, None] // HG_BLOCK == r[None, :] // HG_BLOCK) & (r[None, :] <= r[:, None])).astype(BF16)
    has_state = s0 is not None
    col = lambda j: pl.BlockSpec((1, lbr, D_MODEL), lambda bi, li: (bi, li, j))
    vec = pl.BlockSpec((1, D_MODEL), lambda bi, li: (0, 0))
    in_specs = [col(0), col(1), col(2), col(3), vec, vec, vec,
                pl.BlockSpec((lbr, lbr), lambda bi, li: (0, 0))]
    args = [p3, p3, p3, p3, b_f.reshape(1, D_MODEL), lb.reshape(1, D_MODEL), norm_w.reshape(1, D_MODEL), tri]
    state_spec = pl.BlockSpec((1, HG_HEADS, HG_DK, HG_DV), lambda bi, li: (bi, 0, 0, 0))
    if has_state:
        in_specs.append(state_spec)
        args.append(s0)
    y, s = pl.pallas_call(
        functools.partial(_hgrn_kernel, lb_rows=lbr, has_state=has_state),
        out_shape=[jax.ShapeDtypeStruct((b, l, D_MODEL), BF16),
                   jax.ShapeDtypeStruct((b, HG_HEADS, HG_DK, HG_DV), F32)],
        grid=(b, l // lbr),
        in_specs=in_specs,
        out_specs=[pl.BlockSpec((1, lbr, D_MODEL), lambda bi, li: (bi, li, 0)), state_spec],
        scratch_shapes=[pltpu.VMEM((HG_HEADS, HG_DV, HG_DK), F32),
                        pltpu.VMEM((lbr, D_MODEL), F32),
                        pltpu.VMEM((lbr, D_MODEL), F32),
                        pltpu.VMEM((lbr, D_MODEL), F32)],
        compiler_params=_params("parallel", "arbitrary"),
        name="hgrn2",
    )(*args)
    return y.reshape(b * l, D_MODEL), s


def _fox_prompt_kernel(q_ref, k_ref, v_ref, cq_ref, ck_ref, o_ref, *, l, tq, tk):
    scale2 = FOX_HD ** -0.5 * LOG2E
    row = lax.broadcasted_iota(jnp.int32, (tq, tk), 0)
    col = lax.broadcasted_iota(jnp.int32, (tq, tk), 1)
    diag_bias = {}
    for qi in range(l // tq):
        for kj in range(l // tk):
            q0, k0 = qi * tq, kj * tk
            if k0 <= q0 + tq - 1 and k0 + tk - 1 > q0 and q0 - k0 not in diag_bias:
                diag_bias[q0 - k0] = jnp.where(col <= row + (q0 - k0), 0.0, -jnp.inf)
    for qi in range(l // tq):
        q0 = qi * tq
        q = q_ref[0, 0, q0:q0 + tq, :]
        cq = cq_ref[0, 0, q0:q0 + tq, :]
        m = jnp.full((tq, 1), -jnp.inf, F32)
        den = jnp.zeros((tq, 1), F32)
        acc = jnp.zeros((tq, FOX_HD), F32)
        for kj in range(l // tk):
            k0 = kj * tk
            if k0 > q0 + tq - 1:
                continue
            s = lax.dot_general(q, k_ref[0, 0, k0:k0 + tk, :], (((1,), (1,)), ((), ())),
                                preferred_element_type=F32) * scale2
            s = s + (cq - ck_ref[0, 0, :, k0:k0 + tk])
            if k0 + tk - 1 > q0:
                s = s + diag_bias[q0 - k0]
            m_new = jnp.maximum(m, jnp.max(s, axis=-1, keepdims=True))
            w = jnp.exp2(m - m_new)
            p = jnp.exp2(s - m_new)
            den = den * w + jnp.sum(p, axis=-1, keepdims=True)
            acc = acc * w + jnp.dot(p.astype(BF16), v_ref[0, 0, k0:k0 + tk, :], preferred_element_type=F32)
            m = m_new
        o_ref[0, q0:q0 + tq, :] = (acc / den).astype(BF16)


def _fox_prompt_attend(q, k, v, csum, b, l):
    tq = _tile(l, 256)
    tk = _tile(l, 512)
    csum = csum * LOG2E
    cq = csum.transpose(0, 2, 1)[..., None]
    ck = csum.transpose(0, 2, 1)[:, :, None, :]
    head = pl.BlockSpec((1, 1, l, FOX_HD), lambda bi, h: (bi, h, 0, 0))
    o = pl.pallas_call(
        functools.partial(_fox_prompt_kernel, l=l, tq=tq, tk=tk),
        out_shape=jax.ShapeDtypeStruct((b, l, D_MODEL), BF16),
        grid=(b, FOX_HEADS),
        in_specs=[head, head, head,
                  pl.BlockSpec((1, 1, l, 1), lambda bi, h: (bi, h, 0, 0)),
                  pl.BlockSpec((1, 1, 1, l), lambda bi, h: (bi, h, 0, 0))],
        out_specs=pl.BlockSpec((1, l, FOX_HD), lambda bi, h: (bi, 0, h)),
        compiler_params=_params("parallel", "parallel"),
        name="fox_prompt_attention",
    )(q, k, v, cq, ck)
    return o.reshape(b * l, D_MODEL)


def _fox_sample_kernel(q_ref, kn_ref, vn_ref, kc_ref, vc_ref, cq_ref, ckc_ref, ckn_ref, o_ref, *, l, pc):
    nh = FOX_HEADS
    lshift = l.bit_length() - 1
    scale = FOX_HD ** -0.5
    contract_last = (((1,), (1,)), ((), ()))
    by_head = lambda ref: jnp.concatenate([ref[0, :, h * FOX_HD:(h + 1) * FOX_HD] for h in range(nh)],
                                          axis=0).astype(BF16)
    q = by_head(q_ref)
    cq = cq_ref[0]
    rows = nh * l
    qhead = jnp.right_shift(lax.broadcasted_iota(jnp.int32, (rows, 1), 0), lshift)
    kn, vn = by_head(kn_ref), by_head(vn_ref)
    col = lax.broadcasted_iota(jnp.int32, (1, rows), 1)
    qframe = jnp.bitwise_and(lax.broadcasted_iota(jnp.int32, (rows, 1), 0), l - 1)
    visible = (jnp.right_shift(col, lshift) == qhead) & (jnp.bitwise_and(col, l - 1) <= qframe)
    s = lax.dot_general(q, kn, contract_last, preferred_element_type=F32) * scale + (cq - ckn_ref[0])
    s = jnp.where(visible, s, -jnp.inf)
    m = jnp.max(s, axis=-1, keepdims=True)
    p = jnp.exp(s - m)
    den = jnp.sum(p, axis=-1, keepdims=True)
    acc = jnp.dot(p.astype(BF16), vn, preferred_element_type=F32)
    same_head = jnp.bitwise_and(lax.broadcasted_iota(jnp.int32, (1, pc * nh), 1), nh - 1) == qhead
    for c in range(kc_ref.shape[1] // (pc * nh)):
        ks = slice(c * pc * nh, (c + 1) * pc * nh)
        s = lax.dot_general(q, kc_ref[0, ks, :].astype(BF16), contract_last,
                            preferred_element_type=F32) * scale + (cq - ckc_ref[0, :, ks])
        s = jnp.where(same_head, s, -jnp.inf)
        m_new = jnp.maximum(m, jnp.max(s, axis=-1, keepdims=True))
        w = jnp.exp(m - m_new)
        p = jnp.exp(s - m_new)
        den = den * w + jnp.sum(p, axis=-1, keepdims=True)
        acc = acc * w + jnp.dot(p.astype(BF16), vc_ref[0, ks, :].astype(BF16), preferred_element_type=F32)
        m = m_new
    out = (acc / den).astype(BF16)
    for h in range(nh):
        o_ref[0, :, h * FOX_HD:(h + 1) * FOX_HD] = out[h * l:(h + 1) * l, :]


def _fox_sample_attend(q, kn, vn, cache_k, cache_v, csum, b, l):
    p = cache_k.shape[1]
    nh = FOX_HEADS
    assert nh & (nh - 1) == 0 and l & (l - 1) == 0
    pc = _tile(p, 256)
    c_new = csum[:, p:, :].transpose(0, 2, 1).reshape(b, nh * l)
    cq = c_new[:, :, None]
    ckn = c_new[:, None, :]
    ckc = csum[:, :p, :].reshape(b, 1, p * nh)
    tok = pl.BlockSpec((1, l, D_MODEL), lambda bi: (bi, 0, 0))
    cache = pl.BlockSpec((1, p * nh, FOX_HD), lambda bi: (bi, 0, 0))
    o = pl.pallas_call(
        functools.partial(_fox_sample_kernel, l=l, pc=pc),
        out_shape=jax.ShapeDtypeStruct((b, l, D_MODEL), BF16),
        grid=(b,),
        in_specs=[tok, tok, tok, cache, cache,
                  pl.BlockSpec((1, nh * l, 1), lambda bi: (bi, 0, 0)),
                  pl.BlockSpec((1, 1, p * nh), lambda bi: (bi, 0, 0)),
                  pl.BlockSpec((1, 1, nh * l), lambda bi: (bi, 0, 0))],
        out_specs=tok,
        compiler_params=_params("parallel"),
        name="fox_sample_attention",
    )(q.reshape(b, l, D_MODEL), kn.reshape(b, l, D_MODEL), vn.reshape(b, l, D_MODEL),
      cache_k.reshape(b, p * nh, FOX_HD), cache_v.reshape(b, p * nh, FOX_HD), cq, ckc, ckn)
    return o.reshape(b * l, D_MODEL)


def _layer_norm(z, g, b):
    mu = jnp.mean(z, axis=-1, keepdims=True)
    zc = z - mu
    var = jnp.mean(zc * zc, axis=-1, keepdims=True)
    return zc * lax.rsqrt(var + LN_EPS) * g + b


def _outproj_norm_kernel(y_ref, w_ref, x_ref, gate_ref, sc_ref, sh_ref, lng_ref, lnb_ref, rw_ref,
                         xn_ref, u_ref, s_ref):
    out = jnp.dot(y_ref[...], w_ref[...], preferred_element_type=F32)
    z = ALPHA * x_ref[...] + (1.0 + gate_ref[0]) * out
    xn = _layer_norm(z, lng_ref[...], lnb_ref[...])
    xn_ref[...] = xn
    u = xn * (1.0 + sc_ref[0]) + sh_ref[0]
    u_ref[...] = u
    s_ref[...] = jax.nn.sigmoid(jnp.dot(u.astype(BF16), rw_ref[...], preferred_element_type=F32))


def _outproj_norm(y, w, x, gate, sc, sh, ln_g, ln_b, rw, tm, bps):
    t, kdim = y.shape
    d = w.shape[1]
    row = pl.BlockSpec((tm, d), lambda i: (i, 0))
    vec = pl.BlockSpec((1, d), lambda i: (0, 0))
    once = pl.Buffered(1)
    return pl.pallas_call(
        _outproj_norm_kernel,
        out_shape=[jax.ShapeDtypeStruct((t, d), F32), jax.ShapeDtypeStruct((t, d), F32),
                   jax.ShapeDtypeStruct((t, LANES), F32)],
        grid=(t // tm,),
        in_specs=[pl.BlockSpec((tm, kdim), lambda i: (i, 0)),
                  pl.BlockSpec((kdim, d), lambda i: (0, 0), pipeline_mode=once),
                  row, _mod_spec(gate, bps), _mod_spec(sc, bps), _mod_spec(sh, bps), vec, vec,
                  pl.BlockSpec((d, LANES), lambda i: (0, 0), pipeline_mode=once)],
        out_specs=[row, row, pl.BlockSpec((tm, LANES), lambda i: (i, 0))],
        compiler_params=_params("parallel"),
        name="outproj_norm",
    )(y, w, x, gate, sc, sh, ln_g.reshape(1, d), ln_b.reshape(1, d), rw)


def _argmax_first(vals):
    best, idx = vals[0], jnp.zeros(vals[0].shape, jnp.int32)
    for j in range(1, len(vals)):
        gt = vals[j] > best
        best = jnp.where(gt, vals[j], best)
        idx = jnp.where(gt, j, idx)
    return best, idx


def _pick(rows, idx):
    out = rows[0]
    for j in range(1, len(rows)):
        out = jnp.where(idx == j, rows[j], out)
    return out


def _route_kernel(s_ref, b_ref, tri_ref, e_ref, r_ref, w_ref, cnt_ref):
    i = pl.program_id(0)

    @pl.when(i == 0)
    def _():
        cnt_ref[...] = jnp.zeros_like(cnt_ref)

    sc = s_ref[...].T[:N_EXPERTS, :]
    sel = sc + b_ref[...]
    row = lambda a, e: a[e:e + 1, :]
    grp = []
    for g in range(N_GROUPS):
        a, b, c, d = (row(sel, g * EXPERTS_PER_GROUP + j) for j in range(EXPERTS_PER_GROUP))
        hi1, lo1, hi2, lo2 = jnp.maximum(a, b), jnp.minimum(a, b), jnp.maximum(c, d), jnp.minimum(c, d)
        grp.append(jnp.maximum(hi1, hi2) + jnp.maximum(jnp.minimum(hi1, hi2), jnp.maximum(lo1, lo2)))
    _, gidx = _argmax_first(grp)
    member = lambda a: [_pick([row(a, g * EXPERTS_PER_GROUP + j) for g in range(N_GROUPS)], gidx)
                        for j in range(EXPERTS_PER_GROUP)]
    v, c = member(sel), member(sc)
    _, j0 = _argmax_first(v)
    _, j1 = _argmax_first([jnp.where(j0 == j, -jnp.inf, v[j]) for j in range(EXPERTS_PER_GROUP)])
    c0, c1 = _pick(c, j0), _pick(c, j1)
    den = c0 + c1
    e0 = gidx * EXPERTS_PER_GROUP + j0
    e1 = gidx * EXPERTS_PER_GROUP + j1
    eio = lax.broadcasted_iota(jnp.int32, sc.shape, 0)
    oh0, oh1 = eio == e0, eio == e1
    member_f = jnp.where(oh0 | oh1, 1.0, 0.0)
    before = jnp.dot(member_f.astype(BF16), tri_ref[...], preferred_element_type=F32) + cnt_ref[...]
    r0 = jnp.sum(jnp.where(oh0, before, 0.0), axis=0, keepdims=True)
    r1 = jnp.sum(jnp.where(oh1, before, 0.0), axis=0, keepdims=True)
    cnt_ref[...] += jnp.sum(member_f, axis=1, keepdims=True)
    e_ref[0:1, :] = e0
    e_ref[1:2, :] = e1
    r_ref[0:1, :] = r0.astype(jnp.int32)
    r_ref[1:2, :] = r1.astype(jnp.int32)
    w_ref[0:1, :] = c0 / den
    w_ref[1:2, :] = c1 / den


def _route(scores, router_b, tm_e):
    t = scores.shape[0]
    tm = _tile(t, 512)
    r = jnp.arange(tm)
    tri = (r[:, None] < r[None, :]).astype(BF16)
    slot = pl.BlockSpec((TOP_K, tm), lambda i: (0, i))
    eidx, rank, wts, counts = pl.pallas_call(
        _route_kernel,
        out_shape=[jax.ShapeDtypeStruct((TOP_K, t), jnp.int32), jax.ShapeDtypeStruct((TOP_K, t), jnp.int32),
                   jax.ShapeDtypeStruct((TOP_K, t), F32), jax.ShapeDtypeStruct((N_EXPERTS, 1), F32)],
        grid=(t // tm,),
        in_specs=[pl.BlockSpec((tm, LANES), lambda i: (i, 0)),
                  pl.BlockSpec((N_EXPERTS, 1), lambda i: (0, 0)),
                  pl.BlockSpec((tm, tm), lambda i: (0, 0))],
        out_specs=[slot, slot, slot, pl.BlockSpec((N_EXPERTS, 1), lambda i: (0, 0))],
        compiler_params=_params("arbitrary"),
        name="moe_route",
    )(scores, router_b.astype(F32).reshape(N_EXPERTS, 1), tri)
    counts = counts[:, 0].astype(jnp.int32)
    padded = (counts + tm_e - 1) // tm_e * tm_e
    pad_end = jnp.cumsum(padded)
    pad_start = pad_end - padded
    onehot = eidx[:, :, None] == jnp.arange(N_EXPERTS, dtype=jnp.int32)[None, None, :]
    dest = (jnp.sum(jnp.where(onehot, pad_start[None, None, :], 0), axis=-1) + rank).reshape(TOP_K * t)
    n_blocks = (t * TOP_K + N_EXPERTS * (tm_e - 1) + tm_e - 1) // tm_e
    blk_row = jnp.arange(n_blocks, dtype=jnp.int32) * tm_e
    blk_e = jnp.minimum(jnp.sum((pad_end[None, :] <= blk_row[:, None]).astype(jnp.int32), axis=1), N_EXPERTS - 1)
    n_used = (pad_end[-1] // tm_e).astype(jnp.int32).reshape(1)
    last_blk = jnp.concatenate([jnp.where(counts > 0, pad_end - tm_e, -1).astype(jnp.int32), n_used])
    return dest.astype(jnp.int32), wts.T, blk_e, n_used, n_blocks, last_blk


def _dispatch_kernel(dest_ref, last_ref, u_ref, xs_ref, zero_ref, sem, zsem, *, tb, t_total, tm_e, n_blocks):
    base = pl.program_id(0) * tb

    @pl.when(pl.program_id(0) == 0)
    def _():
        zero_ref[...] = jnp.zeros_like(zero_ref)

        def zero_copy(e):
            row = pl.multiple_of(jnp.maximum(last_ref[e], 0), tm_e)
            return pltpu.make_async_copy(zero_ref, xs_ref.at[pl.ds(row, tm_e), :], zsem)

        for e in range(N_EXPERTS):
            pl.when(last_ref[e] >= 0)(lambda e=e: zero_copy(e).start())
        for e in range(N_EXPERTS):
            pl.when(last_ref[e] >= 0)(lambda e=e: zero_copy(e).wait())

        def tail_copy(blk):
            return pltpu.make_async_copy(zero_ref, xs_ref.at[pl.ds(pl.multiple_of(blk * tm_e, tm_e), tm_e), :], zsem)

        n_used = last_ref[N_EXPERTS]
        lax.fori_loop(n_used, n_blocks, lambda blk, c: (tail_copy(blk).start(), c)[1], 0)
        lax.fori_loop(n_used, n_blocks, lambda blk, c: (tail_copy(blk).wait(), c)[1], 0)

    def row_copy(t, slot):
        d = dest_ref[slot * t_total + base + t]
        return pltpu.make_async_copy(u_ref.at[pl.ds(t, 1), :], xs_ref.at[pl.ds(d, 1), :], sem)

    def issue(t, carry):
        for slot in range(TOP_K):
            row_copy(t, slot).start()
        return carry

    lax.fori_loop(0, tb, issue, 0, unroll=DMA_UNROLL)
    for slot in range(TOP_K):
        pltpu.make_async_copy(u_ref, xs_ref.at[pl.ds(0, tb), :], sem).wait()


def _dispatch(u, dest, last_blk, n_rows, tb, tm_e):
    t, d = u.shape
    return pl.pallas_call(
        functools.partial(_dispatch_kernel, tb=tb, t_total=t, tm_e=tm_e, n_blocks=n_rows // tm_e),
        out_shape=jax.ShapeDtypeStruct((n_rows, d), F32),
        grid_spec=pltpu.PrefetchScalarGridSpec(
            num_scalar_prefetch=2,
            grid=(t // tb,),
            in_specs=[pl.BlockSpec((tb, d), lambda i, dest_ref, last_ref: (i, 0))],
            out_specs=pl.BlockSpec(memory_space=pl.ANY),
            scratch_shapes=[pltpu.VMEM((tm_e, d), F32), pltpu.SemaphoreType.DMA, pltpu.SemaphoreType.DMA],
        ),
        compiler_params=_params("arbitrary"),
        name="moe_dispatch",
    )(dest, last_blk, u)


def _expert_kernel(blk_e_ref, n_used_ref, xs_ref, wg_ref, wu_ref, wd_ref, ys_ref):
    del blk_e_ref
    i = pl.program_id(0)

    @pl.when(i < n_used_ref[0])
    def _():
        x = xs_ref[...].astype(BF16)
        g = jnp.dot(x, wg_ref[0], preferred_element_type=F32)
        up = jnp.dot(x, wu_ref[0], preferred_element_type=F32)
        hid = (g * jax.nn.sigmoid(g) * up).astype(BF16)
        ys_ref[...] = jnp.dot(hid, wd_ref[0], preferred_element_type=F32)

    @pl.when(i >= n_used_ref[0])
    def _():
        ys_ref[...] = jnp.zeros_like(ys_ref)


def _experts(xs, blk_e, n_used, wg, wu, wd, tm_e):
    r, d = xs.shape
    de = wg.shape[2]
    return pl.pallas_call(
        _expert_kernel,
        out_shape=jax.ShapeDtypeStruct((r, d), F32),
        grid_spec=pltpu.PrefetchScalarGridSpec(
            num_scalar_prefetch=2,
            grid=(r // tm_e,),
            in_specs=[pl.BlockSpec((tm_e, d), lambda i, be, nu: (i, 0)),
                      pl.BlockSpec((1, d, de), lambda i, be, nu: (be[i], 0, 0)),
                      pl.BlockSpec((1, d, de), lambda i, be, nu: (be[i], 0, 0)),
                      pl.BlockSpec((1, de, d), lambda i, be, nu: (be[i], 0, 0))],
            out_specs=pl.BlockSpec((tm_e, d), lambda i, be, nu: (i, 0)),
        ),
        compiler_params=_params("arbitrary"),
        name="moe_experts",
    )(blk_e, n_used, xs, wg, wu, wd)


def _combine_norm_kernel(dest_ref, ys_ref, wts_ref, x_ref, gate_ref, lng_ref, lnb_ref, *rest,
                         tb, t_total, next_mod):
    if next_mod:
        sc_ref, sh_ref, xn_ref, u_ref, y0_ref, y1_ref, sems = rest
    else:
        xn_ref, y0_ref, y1_ref, sems = rest
    i = pl.program_id(0)
    par = i % 2
    bufs = (y0_ref, y1_ref)

    def start_gathers(blk, p):
        base = blk * tb

        def issue(t, carry):
            for slot in range(TOP_K):
                d = dest_ref[slot * t_total + base + t]
                pltpu.make_async_copy(ys_ref.at[pl.ds(d, 1), :], bufs[slot].at[p, pl.ds(t, 1), :],
                                      sems.at[p]).start()
            return carry

        lax.fori_loop(0, tb, issue, 0, unroll=DMA_UNROLL)

    @pl.when(i == 0)
    def _():
        start_gathers(0, 0)

    @pl.when(i + 1 < pl.num_programs(0))
    def _():
        start_gathers(i + 1, 1 - par)

    for buf in bufs:
        pltpu.make_async_copy(ys_ref.at[pl.ds(0, tb), :], buf.at[par], sems.at[par]).wait()
    w = wts_ref[...]
    ffn = w[:, 0:1] * y0_ref[par] + w[:, 1:2] * y1_ref[par]
    z = ALPHA * x_ref[...] + (1.0 + gate_ref[0]) * ffn
    xn = _layer_norm(z, lng_ref[...], lnb_ref[...])
    xn_ref[...] = xn
    if next_mod:
        u_ref[...] = xn * (1.0 + sc_ref[0]) + sh_ref[0]


def _combine_norm(ys, dest, wts, x, gate, next_mod, ln_g, ln_b, tb, bps):
    t, d = x.shape
    row = pl.BlockSpec((tb, d), lambda i, dr: (i, 0))
    vec = pl.BlockSpec((1, d), lambda i, dr: (0, 0))
    in_specs = [pl.BlockSpec(memory_space=pl.ANY),
                pl.BlockSpec((tb, TOP_K), lambda i, dr: (i, 0)),
                row, _mod_spec(gate, bps), vec, vec]
    args = [dest, ys, wts, x, gate, ln_g.reshape(1, d), ln_b.reshape(1, d)]
    out_shape = [jax.ShapeDtypeStruct((t, d), F32)]
    if next_mod is not None:
        in_specs += [_mod_spec(m, bps) for m in next_mod]
        args += list(next_mod)
        out_shape.append(jax.ShapeDtypeStruct((t, d), F32))
    outs = pl.pallas_call(
        functools.partial(_combine_norm_kernel, tb=tb, t_total=t, next_mod=next_mod is not None),
        out_shape=out_shape,
        grid_spec=pltpu.PrefetchScalarGridSpec(
            num_scalar_prefetch=1,
            grid=(t // tb,),
            in_specs=in_specs,
            out_specs=[row] * len(out_shape),
            scratch_shapes=[pltpu.VMEM((2, tb, d), F32), pltpu.VMEM((2, tb, d), F32),
                            pltpu.SemaphoreType.DMA((2,))],
        ),
        compiler_params=_params("arbitrary"),
        name="moe_combine_norm",
    )(*args)
    return outs if next_mod is not None else (outs[0], None)


def _moe_block(x, u, scores, router_b, wg, wu, wd, gate, next_mod, ln_g, ln_b, tm, bps, tm_e):
    dest, wts, blk_e, n_used, n_blocks, last_blk = _route(scores, router_b, tm_e)
    xs = _dispatch(u, dest, last_blk, n_blocks * tm_e, tm, tm_e)
    ys = _experts(xs, blk_e, n_used, wg, wu, wd, tm_e)
    tb = _tile(tm, 256) if gate.shape[1] == 1 else tm
    return _combine_norm(ys, dest, wts, x, gate, next_mod, ln_g, ln_b, tb, bps * (tm // tb))


def _rope_tables(l, pos0):
    half = RET_DK // 2
    inv = ROPE_BASE ** (-jnp.arange(half, dtype=F32) / half)
    ang = (pos0 + jnp.arange(l)).astype(F32)[:, None] * inv[None, :]
    return jnp.cos(ang), jnp.sin(ang)


def _stream(x3, mods, wts, state_ret, state_hgrn, cache_k, cache_v, cache_logf, pos0, tm_e):
    b, l, d = x3.shape
    t = b * l
    fresh = state_ret is None
    if fresh:
        tm = _tile(l, 256)
        bps = l // tm
        expand = lambda m: m[:, None, :]
    else:
        tm = _tile(t, 256)
        bps = 1
        expand = lambda m: jnp.repeat(m, l, axis=0).reshape(t // tm, tm, d)
    x = x3.reshape(t, d)
    outs = dict(ret=[], hg=[], fk=[], fv=[], fl=[])
    log_gamma = jnp.log1p(-jnp.exp2(-5.0 - jnp.arange(RET_HEADS, dtype=F32)))
    cos, sin = _rope_tables(l, pos0)
    u = None
    for layer in range(DEPTH):
        m = [expand(a) for a in mods[layer]]
        kind, j = layer % N_MIXERS, layer // N_MIXERS
        if kind == 0:
            if layer == 0:
                proj = _inproj(x, wts['ret_w_in'][j], (m[1], m[0], l if fresh else None))
            else:
                proj = _inproj(u, wts['ret_w_in'][j])
            s0 = None if fresh else state_ret[j]
            y, s = _retention(proj, cos, sin, log_gamma, wts['ret_gn_w'][j], s0, b, l)
            outs['ret'].append(s)
            w_out = wts['ret_w_out'][j]
        elif kind == 1:
            proj = _inproj(u, wts['hg_w_in'][j])
            s0 = None if fresh else state_hgrn[j]
            y, s = _hgrn(proj, wts['hg_b_f'][j], wts['lbs'][layer], wts['hg_norm_w'][j], s0, b, l)
            outs['hg'].append(s)
            w_out = wts['hg_w_out'][j]
        else:
            wq, wk, wv, wf, bf = wts['fox_in'][j]
            logf, csum = (a[:, :FOX_HEADS].reshape(b, l, FOX_HEADS) for a in _fox_gate(u, wf, bf, b, l))
            if fresh:
                tf = _tile(l, 512)
                (qh,) = _fox_inproj(u, wq, b, l, tf, False, True)
                kt, kh = _fox_inproj(u, wk, b, l, tf, True, True)
                vt, vh = _fox_inproj(u, wv, b, l, tf, True, True)
                y = _fox_prompt_attend(qh, kh, vh, csum, b, l)
            else:
                qt = _inproj(u, wq)
                kt = _inproj(u, wk)
                vt = _inproj(u, wv)
                lf_all = jnp.concatenate([cache_logf[j].astype(F32), logf], axis=1)
                pos = jnp.arange(lf_all.shape[1])
                csum = jnp.einsum('ts,bsh->bth', (pos[None, :] <= pos[:, None]).astype(F32), lf_all,
                                  precision=lax.Precision.HIGHEST)
                y = _fox_sample_attend(qt, kt, vt, cache_k[j], cache_v[j], csum, b, l)
            outs['fk'].append(kt.reshape(b, l, FOX_HEADS, FOX_HD))
            outs['fv'].append(vt.reshape(b, l, FOX_HEADS, FOX_HD))
            outs['fl'].append(logf)
            w_out = wts['fox_w_out'][j]
        x, u, scores = _outproj_norm(y, w_out, x, m[2], m[4], m[3], wts['ln_mix_g'][layer],
                                     wts['ln_mix_b'][layer], wts['router_w'], tm, bps)
        nxt = (expand(mods[layer + 1][1]), expand(mods[layer + 1][0])) if layer + 1 < DEPTH else None
        x, u = _moe_block(x, u, scores, wts['router_b'], wts['moe_w_gate'][layer], wts['moe_w_up'][layer],
                          wts['moe_w_down'][layer], m[5], nxt, wts['ln_ffn_g'][layer],
                          wts['ln_ffn_b'][layer], tm, bps, tm_e)
    return x.reshape(b, l, d), outs


def kernel(x_prompt, x_sample, state_ret, state_hgrn, cache_fox_k, cache_fox_v, cache_fox_logf, c_prompt, c_sample, ada_w, ada_b, ln_mix_g, ln_mix_b, ln_ffn_g, ln_ffn_b, ret_w_in, ret_gn_w, ret_w_out, hg_w_in, hg_b_f, hg_lower_bounds, hg_norm_w, hg_w_out, fox_w_in, fox_b_f, fox_w_out, router_w, router_b, moe_w_gate, moe_w_up, moe_w_down):
    dt = x_prompt.dtype
    d = D_MODEL
    nbp = c_prompt.shape[0]
    lbs = jnp.cumsum(jax.nn.softmax(hg_lower_bounds.astype(F32), axis=0), axis=0)
    lbs = lbs - lbs[0]
    mod_all = _modulation_all(jnp.concatenate([c_prompt, c_sample], axis=0).astype(F32), ada_w, ada_b)
    split6 = lambda m: [m[:, i * d:(i + 1) * d] for i in range(6)]
    mods_p = [split6(mod_all[layer, :nbp]) for layer in range(DEPTH)]
    mods_s = [split6(mod_all[layer, nbp:]) for layer in range(DEPTH)]
    pad = LANES - FOX_HEADS
    fox_in = [(fox_w_in[j, :, :d].astype(BF16), fox_w_in[j, :, d:2 * d].astype(BF16),
               fox_w_in[j, :, 2 * d:3 * d].astype(BF16),
               jnp.pad(fox_w_in[j, :, 3 * d:], ((0, 0), (0, pad))).astype(BF16),
               jnp.pad(fox_b_f[j].astype(F32), (0, pad)).reshape(1, LANES))
              for j in range(fox_w_in.shape[0])]
    wts = dict(
        ret_w_in=ret_w_in.astype(BF16), ret_gn_w=ret_gn_w.astype(F32), ret_w_out=ret_w_out.astype(BF16),
        hg_w_in=hg_w_in.astype(BF16), hg_b_f=hg_b_f.astype(F32), lbs=lbs, hg_norm_w=hg_norm_w.astype(F32),
        hg_w_out=hg_w_out.astype(BF16), fox_in=fox_in, fox_w_out=fox_w_out.astype(BF16),
        router_w=jnp.pad(router_w, ((0, 0), (0, LANES - N_EXPERTS))).astype(BF16), router_b=router_b,
        moe_w_gate=moe_w_gate.astype(BF16), moe_w_up=moe_w_up.astype(BF16), moe_w_down=moe_w_down.astype(BF16),
        ln_mix_g=ln_mix_g.astype(F32), ln_mix_b=ln_mix_b.astype(F32),
        ln_ffn_g=ln_ffn_g.astype(F32), ln_ffn_b=ln_ffn_b.astype(F32))
    past_len = cache_fox_k.shape[2]
    yp, op = _stream(x_prompt, mods_p, wts, None, None, None, None, None, 0, 256)
    ys, os_ = _stream(x_sample, mods_s, wts, state_ret, state_hgrn, cache_fox_k, cache_fox_v,
                      cache_fox_logf, past_len, 128)
    st = lambda xs: jnp.stack(xs).astype(dt)
    return (yp, ys, st(op['ret']), st(os_['ret']), st(op['hg']), st(os_['hg']),
            st(op['fk']), st(op['fv']), st(op['fl']), st(os_['fk']), st(os_['fv']), st(os_['fl']))
```

```python
import functools

import jax
import jax.numpy as jnp
from jax import lax
from jax.experimental import pallas as pl
from jax.experimental.pallas import tpu as pltpu

F32 = jnp.float32
BF16 = jnp.bfloat16

D_MODEL = 2048
DEPTH = 4
CHUNK = 64
N_MIXERS = 3
RET_HEADS = 8
RET_DK = D_MODEL // RET_HEADS
RET_DV = 2 * RET_DK
RET_QK = RET_HEADS * RET_DK
RET_V = RET_HEADS * RET_DV
RET_GROUP = 8
ROPE_BASE = 10000.0
HG_DK = 128
HG_HEADS = D_MODEL // HG_DK
HG_DV = D_MODEL // HG_HEADS
HG_BLOCK = 16
FOX_HEADS = 16
FOX_HD = D_MODEL // FOX_HEADS
N_EXPERTS = 16
N_GROUPS = 4
EXPERTS_PER_GROUP = N_EXPERTS // N_GROUPS
TOPK_GROUP = 1
TOP_K = 2
D_EXPERT = D_MODEL // 2
ALPHA = (2 * DEPTH) ** 0.25
LN_EPS = 1e-5
NORM_EPS = 1e-6
LOG2E = 1.4426950408889634

LANES = 128
VMEM_LIMIT = 56 * 1024 * 1024
DMA_UNROLL = 8


def _params(*sem):
    return pltpu.CompilerParams(dimension_semantics=sem, vmem_limit_bytes=VMEM_LIMIT)


def _tile(n, pref):
    t = min(n, pref)
    while n % t:
        t //= 2
    return t


def _mod_kernel(c_ref, w_ref, b_ref, o_ref):
    c = c_ref[...]
    a = (c * jax.nn.sigmoid(c)).astype(BF16)
    o_ref[0] = jnp.dot(a, w_ref[0].astype(BF16), preferred_element_type=F32) + b_ref[0]


def _modulation_all(c_all, ada_w, ada_b):
    nb = c_all.shape[0]
    depth, d, n = ada_w.shape
    tn = _tile(n, 1024)
    return pl.pallas_call(
        _mod_kernel,
        out_shape=jax.ShapeDtypeStruct((depth, nb, n), F32),
        grid=(depth, n // tn),
        in_specs=[pl.BlockSpec((nb, d), lambda l, j: (0, 0)),
                  pl.BlockSpec((1, d, tn), lambda l, j: (l, 0, j)),
                  pl.BlockSpec((1, 1, tn), lambda l, j: (l, 0, j))],
        out_specs=pl.BlockSpec((1, nb, tn), lambda l, j: (l, 0, j)),
        compiler_params=_params("parallel", "parallel"),
        name="modulation",
    )(c_all, ada_w, ada_b.reshape(depth, 1, n))


def _mod_spec(mod, bps):
    return pl.BlockSpec((1,) + mod.shape[1:], lambda i, *_: (i // bps, 0, 0))


def _inproj_kernel(x_ref, w_ref, *rest, modulated):
    if modulated:
        sc_ref, sh_ref, o_ref, xb_ref = rest
    else:
        o_ref, xb_ref = rest

    @pl.when(pl.program_id(1) == 0)
    def _():
        x = x_ref[...]
        if modulated:
            x = x * (1.0 + sc_ref[0]) + sh_ref[0]
        xb_ref[...] = x.astype(BF16)

    o_ref[...] = jnp.dot(xb_ref[...], w_ref[...], preferred_element_type=F32)


def _inproj(u, w, mod=None):
    t, d = u.shape
    n = w.shape[1]
    tm = _tile(t, 1024)
    tn = _tile(n, 1024)
    in_specs = [pl.BlockSpec((tm, d), lambda i, j: (i, 0)),
                pl.BlockSpec((d, tn), lambda i, j: (0, j))]
    args = [u, w]
    if mod is not None:
        sc, sh, seq_rows = mod
        if seq_rows is None:
            sc, sh, bps = sc.reshape(t // tm, tm, d), sh.reshape(t // tm, tm, d), 1
        else:
            tm = _tile(seq_rows, tm)
            bps = seq_rows // tm
            in_specs[0] = pl.BlockSpec((tm, d), lambda i, j: (i, 0))
        in_specs += [_mod_spec(sc, bps), _mod_spec(sh, bps)]
        args += [sc, sh]
    return pl.pallas_call(
        functools.partial(_inproj_kernel, modulated=mod is not None),
        out_shape=jax.ShapeDtypeStruct((t, n), F32),
        grid=(t // tm, n // tn),
        in_specs=in_specs,
        out_specs=pl.BlockSpec((tm, tn), lambda i, j: (i, j)),
        scratch_shapes=[pltpu.VMEM((tm, d), BF16)],
        compiler_params=_params("parallel", "arbitrary"),
        name="inproj",
    )(*args)


def _fox_inproj_kernel(x_ref, w_ref, *outs, tok, heads):
    acc = jnp.dot(x_ref[...].astype(BF16), w_ref[...], preferred_element_type=F32)
    n = 0
    if tok:
        outs[n][...] = acc
        n += 1
    if heads:
        for h in range(FOX_HEADS):
            outs[n][0, h] = acc[:, h * FOX_HD:(h + 1) * FOX_HD].astype(BF16)


def _fox_inproj(u, w, b, l, tm, tok, heads):
    t, d = u.shape
    bps = l // tm
    out_shape, out_specs = [], []
    if tok:
        out_shape.append(jax.ShapeDtypeStruct((t, d), F32))
        out_specs.append(pl.BlockSpec((tm, d), lambda i: (i, 0)))
    if heads:
        out_shape.append(jax.ShapeDtypeStruct((b, FOX_HEADS, l, FOX_HD), BF16))
        out_specs.append(pl.BlockSpec((1, FOX_HEADS, tm, FOX_HD), lambda i: (i // bps, 0, i % bps, 0)))
    return pl.pallas_call(
        functools.partial(_fox_inproj_kernel, tok=tok, heads=heads),
        out_shape=out_shape,
        grid=(t // tm,),
        in_specs=[pl.BlockSpec((tm, d), lambda i: (i, 0)),
                  pl.BlockSpec((d, d), lambda i: (0, 0))],
        out_specs=out_specs,
        compiler_params=_params("parallel"),
        name="fox_inproj",
    )(u, w)


def _split3(x):
    hi = x.astype(BF16)
    r1 = x - hi.astype(F32)
    mid = r1.astype(BF16)
    lo = (r1 - mid.astype(F32)).astype(BF16)
    return hi, mid, lo


def _tri_cumsum(tri, x):
    hi, mid, lo = _split3(x)
    return (jnp.dot(tri, hi, preferred_element_type=F32) + jnp.dot(tri, mid, preferred_element_type=F32)
            + jnp.dot(tri, lo, preferred_element_type=F32))


def _fox_gate_kernel(x_ref, w_ref, b_ref, tri_ref, o_ref, c_ref, carry_ref):
    @pl.when(pl.program_id(1) == 0)
    def _():
        carry_ref[...] = jnp.zeros_like(carry_ref)

    z = jnp.dot(x_ref[...].astype(BF16), w_ref[...], preferred_element_type=F32) + b_ref[...]
    logf = jnp.minimum(z, 0.0) - jnp.log1p(jnp.exp(-jnp.abs(z)))
    o_ref[...] = logf
    csum = _tri_cumsum(tri_ref[...], logf) + carry_ref[...]
    c_ref[...] = csum
    carry_ref[...] = csum[csum.shape[0] - 1:, :]


def _fox_gate(u, w, b, nb, l):
    t, d = u.shape
    tm = _tile(l, 256)
    bps = l // tm
    r = jnp.arange(tm)
    tri = (r[None, :] <= r[:, None]).astype(BF16)
    row = pl.BlockSpec((tm, LANES), lambda bi, li: (bi * bps + li, 0))
    return pl.pallas_call(
        _fox_gate_kernel,
        out_shape=[jax.ShapeDtypeStruct((t, LANES), F32), jax.ShapeDtypeStruct((t, LANES), F32)],
        grid=(nb, bps),
        in_specs=[pl.BlockSpec((tm, d), lambda bi, li: (bi * bps + li, 0)),
                  pl.BlockSpec((d, LANES), lambda bi, li: (0, 0)),
                  pl.BlockSpec((1, LANES), lambda bi, li: (0, 0)),
                  pl.BlockSpec((tm, tm), lambda bi, li: (0, 0))],
        out_specs=[row, row],
        scratch_shapes=[pltpu.VMEM((1, LANES), F32)],
        compiler_params=_params("parallel", "arbitrary"),
        name="fox_gate",
    )(u, w, b, tri)


def _retention_kernel(lg_ref, q_ref, k_ref, v_ref, g_ref, cos_ref, sin_ref, gn_ref, dec_ref, *rest,
                      lb, has_state):
    if has_state:
        s0_ref, y_ref, sout_ref, s_ref = rest
    else:
        y_ref, sout_ref, s_ref = rest
    hg = pl.program_id(1)
    li = pl.program_id(2)

    @pl.when(li == 0)
    def _():
        if has_state:
            s_ref[...] = s0_ref[0]
        else:
            s_ref[...] = jnp.zeros_like(s_ref)

    half = RET_DK // 2
    idx = lax.broadcasted_iota(jnp.int32, (lb, 1), 0).astype(F32)
    cos, sin = cos_ref[...], sin_ref[...]

    def rope(x):
        x1, x2 = x[:, :half], x[:, half:]
        return jnp.concatenate([x1 * cos - x2 * sin, x1 * sin + x2 * cos], axis=-1)

    for j in range(RET_GROUP):
        lg = lg_ref[hg * RET_GROUP + j]
        q_dec = jnp.exp((idx + 1.0) * lg)
        k_dec = jnp.exp((lb - 1.0 - idx) * lg)
        s_dec = jnp.exp(jnp.full((1, 1), lb, F32) * lg)
        qk_cols = slice(j * RET_DK, (j + 1) * RET_DK)
        v_cols = slice(j * RET_DV, (j + 1) * RET_DV)
        q = rope(q_ref[0, :, qk_cols])
        k = rope(k_ref[0, :, qk_cols]) * (RET_DK ** -0.5)
        vb = v_ref[0, :, v_cols].astype(BF16)
        scores = lax.dot_general(q.astype(BF16), k.astype(BF16), (((1,), (1,)), ((), ())),
                                 preferred_element_type=F32) * dec_ref[j]
        s = s_ref[j]
        o = (jnp.dot(scores.astype(BF16), vb, preferred_element_type=F32)
             + jnp.dot((q * q_dec).astype(BF16), s.astype(BF16), preferred_element_type=F32))
        kd = (k * k_dec).T.astype(BF16)
        s_ref[j] = s * s_dec + jnp.dot(kd, vb, preferred_element_type=F32)
        mu = jnp.mean(o, axis=-1, keepdims=True)
        oc = o - mu
        var = jnp.mean(oc * oc, axis=-1, keepdims=True)
        y = oc * lax.rsqrt(var + NORM_EPS) * gn_ref[:, v_cols]
        g = g_ref[0, :, v_cols]
        y_ref[0, :, v_cols] = (g * jax.nn.sigmoid(g) * y).astype(BF16)

    @pl.when(li == pl.num_programs(2) - 1)
    def _():
        sout_ref[0] = s_ref[...]


def _retention(proj, cos, sin, log_gamma, gn_w, s0, b, l):
    cl = min(l, CHUNK)
    lb = _tile(l, 4 * cl)
    p3 = proj.reshape(b, l, proj.shape[1])
    ng = RET_HEADS // RET_GROUP
    qw, vw = RET_GROUP * RET_DK, RET_GROUP * RET_DV
    has_state = s0 is not None
    in_specs = [pl.BlockSpec(memory_space=pltpu.SMEM),
                pl.BlockSpec((1, lb, qw), lambda bi, h, li: (bi, li, h)),
                pl.BlockSpec((1, lb, qw), lambda bi, h, li: (bi, li, ng + h)),
                pl.BlockSpec((1, lb, vw), lambda bi, h, li: (bi, li, ng + h)),
                pl.BlockSpec((1, lb, vw), lambda bi, h, li: (bi, li, 2 * ng + h)),
                pl.BlockSpec((lb, RET_DK // 2), lambda bi, h, li: (li, 0)),
                pl.BlockSpec((lb, RET_DK // 2), lambda bi, h, li: (li, 0)),
                pl.BlockSpec((1, vw), lambda bi, h, li: (0, h)),
                pl.BlockSpec((RET_GROUP, lb, lb), lambda bi, h, li: (h, 0, 0))]
    pos = jnp.arange(lb)
    dt = pos[:, None] - pos[None, :]
    same = (pos[:, None] // cl) == (pos[None, :] // cl)
    dist = jnp.where(same, jnp.abs(dt), dt).astype(F32)
    decay = jnp.where((same | (dt > 0))[None], jnp.exp(dist[None] * log_gamma[:, None, None]), 0.0)
    args = [log_gamma, p3, p3, p3, p3, cos, sin, gn_w.reshape(1, RET_V), decay]
    state_spec = pl.BlockSpec((1, RET_GROUP, RET_DK, RET_DV), lambda bi, h, li: (bi, h, 0, 0))
    if has_state:
        in_specs.append(state_spec)
        args.append(s0)
    y, s = pl.pallas_call(
        functools.partial(_retention_kernel, lb=lb, has_state=has_state),
        out_shape=[jax.ShapeDtypeStruct((b, l, RET_V), BF16),
                   jax.ShapeDtypeStruct((b, RET_HEADS, RET_DK, RET_DV), F32)],
        grid=(b, ng, l // lb),
        in_specs=in_specs,
        out_specs=[pl.BlockSpec((1, lb, vw), lambda bi, h, li: (bi, li, h)), state_spec],
        scratch_shapes=[pltpu.VMEM((RET_GROUP, RET_DK, RET_DV), F32)],
        compiler_params=_params("parallel", "parallel", "arbitrary"),
        name="retention",
    )(*args)
    return y.reshape(b * l, RET_V), s


def _hgrn_kernel(q_ref, fz_ref, v_ref, g_ref, bf_ref, lb_ref, nw_ref, tri_ref, *rest, lb_rows, has_state):
    if has_state:
        s0_ref, y_ref, sout_ref, st_ref, gc_ref, k_ref, o_ref = rest
    else:
        y_ref, sout_ref, st_ref, gc_ref, k_ref, o_ref = rest
    li = pl.program_id(1)
    hb = HG_BLOCK
    half = hb // 2

    @pl.when(li == 0)
    def _():
        for h in range(HG_HEADS):
            if has_state:
                st_ref[h] = s0_ref[0, h].T
            else:
                st_ref[h] = jnp.zeros((HG_DV, HG_DK), F32)

    lbv = lb_ref[...]
    f = lbv + (1.0 - lbv) * jax.nn.sigmoid(fz_ref[0] + bf_ref[...])
    logf = jnp.log(f)
    k_ref[...] = 1.0 - f
    gc_ref[...] = _tri_cumsum(tri_ref[...], logf) * LOG2E

    rt = lax.broadcasted_iota(jnp.int32, (half, 1), 0)
    contract_last = (((1,), (1,)), ((), ()))

    def block(bi, carry):
        r0 = pl.multiple_of(bi * hb, hb)
        for h in range(HG_HEADS):
            cs = slice(h * HG_DK, (h + 1) * HG_DK)
            gb = gc_ref[pl.ds(r0, hb), cs]
            qb = q_ref[0, pl.ds(r0, hb), cs]
            kb = k_ref[pl.ds(r0, hb), cs]
            vb = v_ref[0, pl.ds(r0, hb), cs]
            q_top, q_bot = qb[:half], qb[half:]
            g_top, g_bot = gb[:half], gb[half:]
            gk = gb - jnp.log2(kb)
            i_top = jnp.zeros((half, HG_DV), F32)
            i_bot = jnp.zeros((half, HG_DV), F32)
            for s in range(hb):
                gs, vs = gk[s:s + 1], vb[s:s + 1]
                if s < half:
                    e = jnp.where(rt >= s, jnp.exp2(g_top - gs), 0.0)
                    a = jnp.sum(q_top * e, axis=-1, keepdims=True)
                    i_top = i_top + a * vs
                    e = jnp.exp2(g_bot - gs)
                else:
                    e = jnp.where(rt + half >= s, jnp.exp2(g_bot - gs), 0.0)
                a = jnp.sum(q_bot * e, axis=-1, keepdims=True)
                i_bot = i_bot + a * vs
            intra = jnp.concatenate([i_top, i_bot], axis=0)
            st = st_ref[h]
            qt = (qb * jnp.exp2(gb)).astype(BF16)
            inter = lax.dot_general(qt, st.astype(BF16), contract_last, preferred_element_type=F32)
            o_ref[pl.ds(r0, hb), cs] = intra + inter
            gl = gb[hb - 1:hb]
            kt = (kb * jnp.exp2(gl - gb)).astype(BF16)
            upd = jnp.dot(vb.T.astype(BF16), kt, preferred_element_type=F32)
            st_ref[h] = st * jnp.exp2(gl) + upd
        return carry

    lax.fori_loop(0, lb_rows // hb, block, 0)

    for h in range(HG_HEADS):
        cs = slice(h * HG_DK, (h + 1) * HG_DK)
        oh = o_ref[:, cs]
        on = oh * lax.rsqrt(jnp.mean(oh * oh, axis=-1, keepdims=True) + NORM_EPS)
        g = g_ref[0, :, cs]
        y_ref[0, :, cs] = (on * nw_ref[:, cs] * (g * jax.nn.sigmoid(g))).astype(BF16)

    @pl.when(li == pl.num_programs(1) - 1)
    def _():
        for h in range(HG_HEADS):
            sout_ref[0, h] = st_ref[h].T


def _hgrn(proj, b_f, lb, norm_w, s0, b, l):
    assert l % HG_BLOCK == 0
    lbr = _tile(l, 256)
    p3 = proj.reshape(b, l, 4 * D_MODEL)
    r = jnp.arange(lbr)
    tri = ((r[:, None] // HG_BLOCK == r[None, :] // HG_BLOCK) & (r[None, :] <= r[:, None])).astype(BF16)
    has_state = s0 is not None
    col = lambda j: pl.BlockSpec((1, lbr, D_MODEL), lambda bi, li: (bi, li, j))
    vec = pl.BlockSpec((1, D_MODEL), lambda bi, li: (0, 0))
    in_specs = [col(0), col(1), col(2), col(3), vec, vec, vec,
                pl.BlockSpec((lbr, lbr), lambda bi, li: (0, 0))]
    args = [p3, p3, p3, p3, b_f.reshape(1, D_MODEL), lb.reshape(1, D_MODEL), norm_w.reshape(1, D_MODEL), tri]
    state_spec = pl.BlockSpec((1, HG_HEADS, HG_DK, HG_DV), lambda bi, li: (bi, 0, 0, 0))
    if has_state:
        in_specs.append(state_spec)
        args.append(s0)
    y, s = pl.pallas_call(
        functools.partial(_hgrn_kernel, lb_rows=lbr, has_state=has_state),
        out_shape=[jax.ShapeDtypeStruct((b, l, D_MODEL), BF16),
                   jax.ShapeDtypeStruct((b, HG_HEADS, HG_DK, HG_DV), F32)],
        grid=(b, l // lbr),
        in_specs=in_specs,
        out_specs=[pl.BlockSpec((1, lbr, D_MODEL), lambda bi, li: (bi, li, 0)), state_spec],
        scratch_shapes=[pltpu.VMEM((HG_HEADS, HG_DV, HG_DK), F32),
                        pltpu.VMEM((lbr, D_MODEL), F32),
                        pltpu.VMEM((lbr, D_MODEL), F32),
                        pltpu.VMEM((lbr, D_MODEL), F32)],
        compiler_params=_params("parallel", "arbitrary"),
        name="hgrn2",
    )(*args)
    return y.reshape(b * l, D_MODEL), s


def _fox_prompt_kernel(q_ref, k_ref, v_ref, cq_ref, ck_ref, o_ref, *, l, tq, tk):
    scale2 = FOX_HD ** -0.5 * LOG2E
    row = lax.broadcasted_iota(jnp.int32, (tq, tk), 0)
    col = lax.broadcasted_iota(jnp.int32, (tq, tk), 1)
    diag_bias = {}
    for qi in range(l // tq):
        for kj in range(l // tk):
            q0, k0 = qi * tq, kj * tk
            if k0 <= q0 + tq - 1 and k0 + tk - 1 > q0 and q0 - k0 not in diag_bias:
                diag_bias[q0 - k0] = jnp.where(col <= row + (q0 - k0), 0.0, -jnp.inf)
    for qi in range(l // tq):
        q0 = qi * tq
        q = q_ref[0, 0, q0:q0 + tq, :]
        cq = cq_ref[0, 0, q0:q0 + tq, :]
        m = jnp.full((tq, 1), -jnp.inf, F32)
        den = jnp.zeros((tq, 1), F32)
        acc = jnp.zeros((tq, FOX_HD), F32)
        for kj in range(l // tk):
            k0 = kj * tk
            if k0 > q0 + tq - 1:
                continue
            s = lax.dot_general(q, k_ref[0, 0, k0:k0 + tk, :], (((1,), (1,)), ((), ())),
                                preferred_element_type=F32) * scale2
            s = s + (cq - ck_ref[0, 0, :, k0:k0 + tk])
            if k0 + tk - 1 > q0:
                s = s + diag_bias[q0 - k0]
            m_new = jnp.maximum(m, jnp.max(s, axis=-1, keepdims=True))
            w = jnp.exp2(m - m_new)
            p = jnp.exp2(s - m_new)
            den = den * w + jnp.sum(p, axis=-1, keepdims=True)
            acc = acc * w + jnp.dot(p.astype(BF16), v_ref[0, 0, k0:k0 + tk, :], preferred_element_type=F32)
            m = m_new
        o_ref[0, q0:q0 + tq, :] = (acc / den).astype(BF16)


def _fox_prompt_attend(q, k, v, csum, b, l):
    tq = _tile(l, 256)
    tk = _tile(l, 512)
    csum = csum * LOG2E
    cq = csum.transpose(0, 2, 1)[..., None]
    ck = csum.transpose(0, 2, 1)[:, :, None, :]
    head = pl.BlockSpec((1, 1, l, FOX_HD), lambda bi, h: (bi, h, 0, 0))
    o = pl.pallas_call(
        functools.partial(_fox_prompt_kernel, l=l, tq=tq, tk=tk),
        out_shape=jax.ShapeDtypeStruct((b, l, D_MODEL), BF16),
        grid=(b, FOX_HEADS),
        in_specs=[head, head, head,
                  pl.BlockSpec((1, 1, l, 1), lambda bi, h: (bi, h, 0, 0)),
                  pl.BlockSpec((1, 1, 1, l), lambda bi, h: (bi, h, 0, 0))],
        out_specs=pl.BlockSpec((1, l, FOX_HD), lambda bi, h: (bi, 0, h)),
        compiler_params=_params("parallel", "parallel"),
        name="fox_prompt_attention",
    )(q, k, v, cq, ck)
    return o.reshape(b * l, D_MODEL)


def _fox_sample_kernel(q_ref, kn_ref, vn_ref, kc_ref, vc_ref, cq_ref, ckc_ref, ckn_ref, o_ref, *, l, pc):
    nh = FOX_HEADS
    lshift = l.bit_length() - 1
    scale = FOX_HD ** -0.5
    contract_last = (((1,), (1,)), ((), ()))
    by_head = lambda ref: jnp.concatenate([ref[0, :, h * FOX_HD:(h + 1) * FOX_HD] for h in range(nh)],
                                          axis=0).astype(BF16)
    q = by_head(q_ref)
    cq = cq_ref[0]
    rows = nh * l
    qhead = jnp.right_shift(lax.broadcasted_iota(jnp.int32, (rows, 1), 0), lshift)
    kn, vn = by_head(kn_ref), by_head(vn_ref)
    col = lax.broadcasted_iota(jnp.int32, (1, rows), 1)
    qframe = jnp.bitwise_and(lax.broadcasted_iota(jnp.int32, (rows, 1), 0), l - 1)
    visible = (jnp.right_shift(col, lshift) == qhead) & (jnp.bitwise_and(col, l - 1) <= qframe)
    s = lax.dot_general(q, kn, contract_last, preferred_element_type=F32) * scale + (cq - ckn_ref[0])
    s = jnp.where(visible, s, -jnp.inf)
    m = jnp.max(s, axis=-1, keepdims=True)
    p = jnp.exp(s - m)
    den = jnp.sum(p, axis=-1, keepdims=True)
    acc = jnp.dot(p.astype(BF16), vn, preferred_element_type=F32)
    same_head = jnp.bitwise_and(lax.broadcasted_iota(jnp.int32, (1, pc * nh), 1), nh - 1) == qhead
    for c in range(kc_ref.shape[1] // (pc * nh)):
        ks = slice(c * pc * nh, (c + 1) * pc * nh)
        s = lax.dot_general(q, kc_ref[0, ks, :].astype(BF16), contract_last,
                            preferred_element_type=F32) * scale + (cq - ckc_ref[0, :, ks])
        s = jnp.where(same_head, s, -jnp.inf)
        m_new = jnp.maximum(m, jnp.max(s, axis=-1, keepdims=True))
        w = jnp.exp(m - m_new)
        p = jnp.exp(s - m_new)
        den = den * w + jnp.sum(p, axis=-1, keepdims=True)
        acc = acc * w + jnp.dot(p.astype(BF16), vc_ref[0, ks, :].astype(BF16), preferred_element_type=F32)
        m = m_new
    out = (acc / den).astype(BF16)
    for h in range(nh):
        o_ref[0, :, h * FOX_HD:(h + 1) * FOX_HD] = out[h * l:(h + 1) * l, :]


def _fox_sample_attend(q, kn, vn, cache_k, cache_v, csum, b, l):
    p = cache_k.shape[1]
    nh = FOX_HEADS
    assert nh & (nh - 1) == 0 and l & (l - 1) == 0
    pc = _tile(p, 256)
    c_new = csum[:, p:, :].transpose(0, 2, 1).reshape(b, nh * l)
    cq = c_new[:, :, None]
    ckn = c_new[:, None, :]
    ckc = csum[:, :p, :].reshape(b, 1, p * nh)
    tok = pl.BlockSpec((1, l, D_MODEL), lambda bi: (bi, 0, 0))
    cache = pl.BlockSpec((1, p * nh, FOX_HD), lambda bi: (bi, 0, 0))
    o = pl.pallas_call(
        functools.partial(_fox_sample_kernel, l=l, pc=pc),
        out_shape=jax.ShapeDtypeStruct((b, l, D_MODEL), BF16),
        grid=(b,),
        in_specs=[tok, tok, tok, cache, cache,
                  pl.BlockSpec((1, nh * l, 1), lambda bi: (bi, 0, 0)),
                  pl.BlockSpec((1, 1, p * nh), lambda bi: (bi, 0, 0)),
                  pl.BlockSpec((1, 1, nh * l), lambda bi: (bi, 0, 0))],
        out_specs=tok,
        compiler_params=_params("parallel"),
        name="fox_sample_attention",
    )(q.reshape(b, l, D_MODEL), kn.reshape(b, l, D_MODEL), vn.reshape(b, l, D_MODEL),
      cache_k.reshape(b, p * nh, FOX_HD), cache_v.reshape(b, p * nh, FOX_HD), cq, ckc, ckn)
    return o.reshape(b * l, D_MODEL)


def _layer_norm(z, g, b):
    mu = jnp.mean(z, axis=-1, keepdims=True)
    zc = z - mu
    var = jnp.mean(zc * zc, axis=-1, keepdims=True)
    return zc * lax.rsqrt(var + LN_EPS) * g + b


def _outproj_norm_kernel(y_ref, w_ref, x_ref, gate_ref, sc_ref, sh_ref, lng_ref, lnb_ref, rw_ref,
                         xn_ref, u_ref, s_ref):
    out = jnp.dot(y_ref[...], w_ref[...], preferred_element_type=F32)
    z = ALPHA * x_ref[...] + (1.0 + gate_ref[0]) * out
    xn = _layer_norm(z, lng_ref[...], lnb_ref[...])
    xn_ref[...] = xn
    u = xn * (1.0 + sc_ref[0]) + sh_ref[0]
    u_ref[...] = u
    s_ref[...] = jax.nn.sigmoid(jnp.dot(u.astype(BF16), rw_ref[...], preferred_element_type=F32))


def _outproj_norm(y, w, x, gate, sc, sh, ln_g, ln_b, rw, tm, bps):
    t, kdim = y.shape
    d = w.shape[1]
    row = pl.BlockSpec((tm, d), lambda i: (i, 0))
    vec = pl.BlockSpec((1, d), lambda i: (0, 0))
    once = pl.Buffered(1)
    return pl.pallas_call(
        _outproj_norm_kernel,
        out_shape=[jax.ShapeDtypeStruct((t, d), F32), jax.ShapeDtypeStruct((t, d), F32),
                   jax.ShapeDtypeStruct((t, LANES), F32)],
        grid=(t // tm,),
        in_specs=[pl.BlockSpec((tm, kdim), lambda i: (i, 0)),
                  pl.BlockSpec((kdim, d), lambda i: (0, 0), pipeline_mode=once),
                  row, _mod_spec(gate, bps), _mod_spec(sc, bps), _mod_spec(sh, bps), vec, vec,
                  pl.BlockSpec((d, LANES), lambda i: (0, 0), pipeline_mode=once)],
        out_specs=[row, row, pl.BlockSpec((tm, LANES), lambda i: (i, 0))],
        compiler_params=_params("parallel"),
        name="outproj_norm",
    )(y, w, x, gate, sc, sh, ln_g.reshape(1, d), ln_b.reshape(1, d), rw)


def _argmax_first(vals):
    best, idx = vals[0], jnp.zeros(vals[0].shape, jnp.int32)
    for j in range(1, len(vals)):
        gt = vals[j] > best
        best = jnp.where(gt, vals[j], best)
        idx = jnp.where(gt, j, idx)
    return best, idx


def _pick(rows, idx):
    out = rows[0]
    for j in range(1, len(rows)):
        out = jnp.where(idx == j, rows[j], out)
    return out


def _route_kernel(s_ref, b_ref, tri_ref, e_ref, r_ref, w_ref, cnt_ref):
    i = pl.program_id(0)

    @pl.when(i == 0)
    def _():
        cnt_ref[...] = jnp.zeros_like(cnt_ref)

    sc = s_ref[...].T[:N_EXPERTS, :]
    sel = sc + b_ref[...]
    row = lambda a, e: a[e:e + 1, :]
    grp = []
    for g in range(N_GROUPS):
        a, b, c, d = (row(sel, g * EXPERTS_PER_GROUP + j) for j in range(EXPERTS_PER_GROUP))
        hi1, lo1, hi2, lo2 = jnp.maximum(a, b), jnp.minimum(a, b), jnp.maximum(c, d), jnp.minimum(c, d)
        grp.append(jnp.maximum(hi1, hi2) + jnp.maximum(jnp.minimum(hi1, hi2), jnp.maximum(lo1, lo2)))
    _, gidx = _argmax_first(grp)
    member = lambda a: [_pick([row(a, g * EXPERTS_PER_GROUP + j) for g in range(N_GROUPS)], gidx)
                        for j in range(EXPERTS_PER_GROUP)]
    v, c = member(sel), member(sc)
    _, j0 = _argmax_first(v)
    _, j1 = _argmax_first([jnp.where(j0 == j, -jnp.inf, v[j]) for j in range(EXPERTS_PER_GROUP)])
    c0, c1 = _pick(c, j0), _pick(c, j1)
    den = c0 + c1
    e0 = gidx * EXPERTS_PER_GROUP + j0
    e1 = gidx * EXPERTS_PER_GROUP + j1
    eio = lax.broadcasted_iota(jnp.int32, sc.shape, 0)
    oh0, oh1 = eio == e0, eio == e1
    member_f = jnp.where(oh0 | oh1, 1.0, 0.0)
    before = jnp.dot(member_f.astype(BF16), tri_ref[...], preferred_element_type=F32) + cnt_ref[...]
    r0 = jnp.sum(jnp.where(oh0, before, 0.0), axis=0, keepdims=True)
    r1 = jnp.sum(jnp.where(oh1, before, 0.0), axis=0, keepdims=True)
    cnt_ref[...] += jnp.sum(member_f, axis=1, keepdims=True)
    e_ref[0:1, :] = e0
    e_ref[1:2, :] = e1
    r_ref[0:1, :] = r0.astype(jnp.int32)
    r_ref[1:2, :] = r1.astype(jnp.int32)
    w_ref[0:1, :] = c0 / den
    w_ref[1:2, :] = c1 / den


def _route(scores, router_b, tm_e):
    t = scores.shape[0]
    tm = _tile(t, 512)
    r = jnp.arange(tm)
    tri = (r[:, None] < r[None, :]).astype(BF16)
    slot = pl.BlockSpec((TOP_K, tm), lambda i: (0, i))
    eidx, rank, wts, counts = pl.pallas_call(
        _route_kernel,
        out_shape=[jax.ShapeDtypeStruct((TOP_K, t), jnp.int32), jax.ShapeDtypeStruct((TOP_K, t), jnp.int32),
                   jax.ShapeDtypeStruct((TOP_K, t), F32), jax.ShapeDtypeStruct((N_EXPERTS, 1), F32)],
        grid=(t // tm,),
        in_specs=[pl.BlockSpec((tm, LANES), lambda i: (i, 0)),
                  pl.BlockSpec((N_EXPERTS, 1), lambda i: (0, 0)),
                  pl.BlockSpec((tm, tm), lambda i: (0, 0))],
        out_specs=[slot, slot, slot, pl.BlockSpec((N_EXPERTS, 1), lambda i: (0, 0))],
        compiler_params=_params("arbitrary"),
        name="moe_route",
    )(scores, router_b.astype(F32).reshape(N_EXPERTS, 1), tri)
    counts = counts[:, 0].astype(jnp.int32)
    padded = (counts + tm_e - 1) // tm_e * tm_e
    pad_end = jnp.cumsum(padded)
    pad_start = pad_end - padded
    onehot = eidx[:, :, None] == jnp.arange(N_EXPERTS, dtype=jnp.int32)[None, None, :]
    dest = (jnp.sum(jnp.where(onehot, pad_start[None, None, :], 0), axis=-1) + rank).reshape(TOP_K * t)
    n_blocks = (t * TOP_K + N_EXPERTS * (tm_e - 1) + tm_e - 1) // tm_e
    blk_row = jnp.arange(n_blocks, dtype=jnp.int32) * tm_e
    blk_e = jnp.minimum(jnp.sum((pad_end[None, :] <= blk_row[:, None]).astype(jnp.int32), axis=1), N_EXPERTS - 1)
    n_used = (pad_end[-1] // tm_e).astype(jnp.int32).reshape(1)
    last_blk = jnp.concatenate([jnp.where(counts > 0, pad_end - tm_e, -1).astype(jnp.int32), n_used])
    return dest.astype(jnp.int32), wts.T, blk_e, n_used, n_blocks, last_blk


def _dispatch_kernel(dest_ref, last_ref, u_ref, xs_ref, zero_ref, sem, zsem, *, tb, t_total, tm_e, n_blocks):
    base = pl.program_id(0) * tb

    @pl.when(pl.program_id(0) == 0)
    def _():
        zero_ref[...] = jnp.zeros_like(zero_ref)

        def zero_copy(e):
            row = pl.multiple_of(jnp.maximum(last_ref[e], 0), tm_e)
            return pltpu.make_async_copy(zero_ref, xs_ref.at[pl.ds(row, tm_e), :], zsem)

        for e in range(N_EXPERTS):
            pl.when(last_ref[e] >= 0)(lambda e=e: zero_copy(e).start())
        for e in range(N_EXPERTS):
            pl.when(last_ref[e] >= 0)(lambda e=e: zero_copy(e).wait())

        def tail_copy(blk):
            return pltpu.make_async_copy(zero_ref, xs_ref.at[pl.ds(pl.multiple_of(blk * tm_e, tm_e), tm_e), :], zsem)

        n_used = last_ref[N_EXPERTS]
        lax.fori_loop(n_used, n_blocks, lambda blk, c: (tail_copy(blk).start(), c)[1], 0)
        lax.fori_loop(n_used, n_blocks, lambda blk, c: (tail_copy(blk).wait(), c)[1], 0)

    def row_copy(t, slot):
        d = dest_ref[slot * t_total + base + t]
        return pltpu.make_async_copy(u_ref.at[pl.ds(t, 1), :], xs_ref.at[pl.ds(d, 1), :], sem)

    def issue(t, carry):
        for slot in range(TOP_K):
            row_copy(t, slot).start()
        return carry

    lax.fori_loop(0, tb, issue, 0, unroll=DMA_UNROLL)
    for slot in range(TOP_K):
        pltpu.make_async_copy(u_ref, xs_ref.at[pl.ds(0, tb), :], sem).wait()


def _dispatch(u, dest, last_blk, n_rows, tb, tm_e):
    t, d = u.shape
    return pl.pallas_call(
        functools.partial(_dispatch_kernel, tb=tb, t_total=t, tm_e=tm_e, n_blocks=n_rows // tm_e),
        out_shape=jax.ShapeDtypeStruct((n_rows, d), F32),
        grid_spec=pltpu.PrefetchScalarGridSpec(
            num_scalar_prefetch=2,
            grid=(t // tb,),
            in_specs=[pl.BlockSpec((tb, d), lambda i, dest_ref, last_ref: (i, 0))],
            out_specs=pl.BlockSpec(memory_space=pl.ANY),
            scratch_shapes=[pltpu.VMEM((tm_e, d), F32), pltpu.SemaphoreType.DMA, pltpu.SemaphoreType.DMA],
        ),
        compiler_params=_params("arbitrary"),
        name="moe_dispatch",
    )(dest, last_blk, u)


def _expert_kernel(blk_e_ref, n_used_ref, xs_ref, wg_ref, wu_ref, wd_ref, ys_ref):
    del blk_e_ref
    i = pl.program_id(0)

    @pl.when(i < n_used_ref[0])
    def _():
        x = xs_ref[...].astype(BF16)
        g = jnp.dot(x, wg_ref[0], preferred_element_type=F32)
        up = jnp.dot(x, wu_ref[0], preferred_element_type=F32)
        hid = (g * jax.nn.sigmoid(g) * up).astype(BF16)
        ys_ref[...] = jnp.dot(hid, wd_ref[0], preferred_element_type=F32)

    @pl.when(i >= n_used_ref[0])
    def _():
        ys_ref[...] = jnp.zeros_like(ys_ref)


def _experts(xs, blk_e, n_used, wg, wu, wd, tm_e):
    r, d = xs.shape
    de = wg.shape[2]
    return pl.pallas_call(
        _expert_kernel,
        out_shape=jax.ShapeDtypeStruct((r, d), F32),
        grid_spec=pltpu.PrefetchScalarGridSpec(
            num_scalar_prefetch=2,
            grid=(r // tm_e,),
            in_specs=[pl.BlockSpec((tm_e, d), lambda i, be, nu: (i, 0)),
                      pl.BlockSpec((1, d, de), lambda i, be, nu: (be[i], 0, 0)),
                      pl.BlockSpec((1, d, de), lambda i, be, nu: (be[i], 0, 0)),
                      pl.BlockSpec((1, de, d), lambda i, be, nu: (be[i], 0, 0))],
            out_specs=pl.BlockSpec((tm_e, d), lambda i, be, nu: (i, 0)),
        ),
        compiler_params=_params("arbitrary"),
        name="moe_experts",
    )(blk_e, n_used, xs, wg, wu, wd)


def _combine_norm_kernel(dest_ref, ys_ref, wts_ref, x_ref, gate_ref, lng_ref, lnb_ref, *rest,
                         tb, t_total, next_mod):
    if next_mod:
        sc_ref, sh_ref, xn_ref, u_ref, y0_ref, y1_ref, sems = rest
    else:
        xn_ref, y0_ref, y1_ref, sems = rest
    i = pl.program_id(0)
    par = i % 2
    bufs = (y0_ref, y1_ref)

    def start_gathers(blk, p):
        base = blk * tb

        def issue(t, carry):
            for slot in range(TOP_K):
                d = dest_ref[slot * t_total + base + t]
                pltpu.make_async_copy(ys_ref.at[pl.ds(d, 1), :], bufs[slot].at[p, pl.ds(t, 1), :],
                                      sems.at[p]).start()
            return carry

        lax.fori_loop(0, tb, issue, 0, unroll=DMA_UNROLL)

    @pl.when(i == 0)
    def _():
        start_gathers(0, 0)

    @pl.when(i + 1 < pl.num_programs(0))
    def _():
        start_gathers(i + 1, 1 - par)

    for buf in bufs:
        pltpu.make_async_copy(ys_ref.at[pl.ds(0, tb), :], buf.at[par], sems.at[par]).wait()
    w = wts_ref[...]
    ffn = w[:, 0:1] * y0_ref[par] + w[:, 1:2] * y1_ref[par]
    z = ALPHA * x_ref[...] + (1.0 + gate_ref[0]) * ffn
    xn = _layer_norm(z, lng_ref[...], lnb_ref[...])
    xn_ref[...] = xn
    if next_mod:
        u_ref[...] = xn * (1.0 + sc_ref[0]) + sh_ref[0]


def _combine_norm(ys, dest, wts, x, gate, next_mod, ln_g, ln_b, tb, bps):
    t, d = x.shape
    row = pl.BlockSpec((tb, d), lambda i, dr: (i, 0))
    vec = pl.BlockSpec((1, d), lambda i, dr: (0, 0))
    in_specs = [pl.BlockSpec(memory_space=pl.ANY),
                pl.BlockSpec((tb, TOP_K), lambda i, dr: (i, 0)),
                row, _mod_spec(gate, bps), vec, vec]
    args = [dest, ys, wts, x, gate, ln_g.reshape(1, d), ln_b.reshape(1, d)]
    out_shape = [jax.ShapeDtypeStruct((t, d), F32)]
    if next_mod is not None:
        in_specs += [_mod_spec(m, bps) for m in next_mod]
        args += list(next_mod)
        out_shape.append(jax.ShapeDtypeStruct((t, d), F32))
    outs = pl.pallas_call(
        functools.partial(_combine_norm_kernel, tb=tb, t_total=t, next_mod=next_mod is not None),
        out_shape=out_shape,
        grid_spec=pltpu.PrefetchScalarGridSpec(
            num_scalar_prefetch=1,
            grid=(t // tb,),
            in_specs=in_specs,
            out_specs=[row] * len(out_shape),
            scratch_shapes=[pltpu.VMEM((2, tb, d), F32), pltpu.VMEM((2, tb, d), F32),
                            pltpu.SemaphoreType.DMA((2,))],
        ),
        compiler_params=_params("arbitrary"),
        name="moe_combine_norm",
    )(*args)
    return outs if next_mod is not None else (outs[0], None)


def _moe_block(x, u, scores, router_b, wg, wu, wd, gate, next_mod, ln_g, ln_b, tm, bps, tm_e):
    dest, wts, blk_e, n_used, n_blocks, last_blk = _route(scores, router_b, tm_e)
    xs = _dispatch(u, dest, last_blk, n_blocks * tm_e, tm, tm_e)
    ys = _experts(xs, blk_e, n_used, wg, wu, wd, tm_e)
    tb = _tile(tm, 256) if gate.shape[1] == 1 else tm
    return _combine_norm(ys, dest, wts, x, gate, next_mod, ln_g, ln_b, tb, bps * (tm // tb))


def _rope_tables(l, pos0):
    half = RET_DK // 2
    inv = ROPE_BASE ** (-jnp.arange(half, dtype=F32) / half)
    ang = (pos0 + jnp.arange(l)).astype(F32)[:, None] * inv[None, :]
    return jnp.cos(ang), jnp.sin(ang)


def _stream(x3, mods, wts, state_ret, state_hgrn, cache_k, cache_v, cache_logf, pos0, tm_e):
    b, l, d = x3.shape
    t = b * l
    fresh = state_ret is None
    if fresh:
        tm = _tile(l, 256)
        bps = l // tm
        expand = lambda m: m[:, None, :]
    else:
        tm = _tile(t, 256)
        bps = 1
        expand = lambda m: jnp.repeat(m, l, axis=0).reshape(t // tm, tm, d)
    x = x3.reshape(t, d)
    outs = dict(ret=[], hg=[], fk=[], fv=[], fl=[])
    log_gamma = jnp.log1p(-jnp.exp2(-5.0 - jnp.arange(RET_HEADS, dtype=F32)))
    cos, sin = _rope_tables(l, pos0)
    u = None
    for layer in range(DEPTH):
        m = [expand(a) for a in mods[layer]]
        kind, j = layer % N_MIXERS, layer // N_MIXERS
        if kind == 0:
            if layer == 0:
                proj = _inproj(x, wts['ret_w_in'][j], (m[1], m[0], l if fresh else None))
            else:
                proj = _inproj(u, wts['ret_w_in'][j])
            s0 = None if fresh else state_ret[j]
            y, s = _retention(proj, cos, sin, log_gamma, wts['ret_gn_w'][j], s0, b, l)
            outs['ret'].append(s)
            w_out = wts['ret_w_out'][j]
        elif kind == 1:
            proj = _inproj(u, wts['hg_w_in'][j])
            s0 = None if fresh else state_hgrn[j]
            y, s = _hgrn(proj, wts['hg_b_f'][j], wts['lbs'][layer], wts['hg_norm_w'][j], s0, b, l)
            outs['hg'].append(s)
            w_out = wts['hg_w_out'][j]
        else:
            wq, wk, wv, wf, bf = wts['fox_in'][j]
            logf, csum = (a[:, :FOX_HEADS].reshape(b, l, FOX_HEADS) for a in _fox_gate(u, wf, bf, b, l))
            if fresh:
                tf = _tile(l, 512)
                (qh,) = _fox_inproj(u, wq, b, l, tf, False, True)
                kt, kh = _fox_inproj(u, wk, b, l, tf, True, True)
                vt, vh = _fox_inproj(u, wv, b, l, tf, True, True)
                y = _fox_prompt_attend(qh, kh, vh, csum, b, l)
            else:
                qt = _inproj(u, wq)
                kt = _inproj(u, wk)
                vt = _inproj(u, wv)
                lf_all = jnp.concatenate([cache_logf[j].astype(F32), logf], axis=1)
                pos = jnp.arange(lf_all.shape[1])
                csum = jnp.einsum('ts,bsh->bth', (pos[None, :] <= pos[:, None]).astype(F32), lf_all,
                                  precision=lax.Precision.HIGHEST)
                y = _fox_sample_attend(qt, kt, vt, cache_k[j], cache_v[j], csum, b, l)
            outs['fk'].append(kt.reshape(b, l, FOX_HEADS, FOX_HD))
            outs['fv'].append(vt.reshape(b, l, FOX_HEADS, FOX_HD))
            outs['fl'].append(logf)
            w_out = wts['fox_w_out'][j]
        x, u, scores = _outproj_norm(y, w_out, x, m[2], m[4], m[3], wts['ln_mix_g'][layer],
                                     wts['ln_mix_b'][layer], wts['router_w'], tm, bps)
        nxt = (expand(mods[layer + 1][1]), expand(mods[layer + 1][0])) if layer + 1 < DEPTH else None
        x, u = _moe_block(x, u, scores, wts['router_b'], wts['moe_w_gate'][layer], wts['moe_w_up'][layer],
                          wts['moe_w_down'][layer], m[5], nxt, wts['ln_ffn_g'][layer],
                          wts['ln_ffn_b'][layer], tm, bps, tm_e)
    return x.reshape(b, l, d), outs


def kernel(x_prompt, x_sample, state_ret, state_hgrn, cache_fox_k, cache_fox_v, cache_fox_logf, c_prompt, c_sample, ada_w, ada_b, ln_mix_g, ln_mix_b, ln_ffn_g, ln_ffn_b, ret_w_in, ret_gn_w, ret_w_out, hg_w_in, hg_b_f, hg_lower_bounds, hg_norm_w, hg_w_out, fox_w_in, fox_b_f, fox_w_out, router_w, router_b, moe_w_gate, moe_w_up, moe_w_down):
    dt = x_prompt.dtype
    d = D_MODEL
    nbp = c_prompt.shape[0]
    lbs = jnp.cumsum(jax.nn.softmax(hg_lower_bounds.astype(F32), axis=0), axis=0)
    lbs = lbs - lbs[0]
    mod_all = _modulation_all(jnp.concatenate([c_prompt, c_sample], axis=0).astype(F32), ada_w, ada_b)
    split6 = lambda m: [m[:, i * d:(i + 1) * d] for i in range(6)]
    mods_p = [split6(mod_all[layer, :nbp]) for layer in range(DEPTH)]
    mods_s = [split6(mod_all[layer, nbp:]) for layer in range(DEPTH)]
    pad = LANES - FOX_HEADS
    fox_in = [(fox_w_in[j, :, :d].astype(BF16), fox_w_in[j, :, d:2 * d].astype(BF16),
               fox_w_in[j, :, 2 * d:3 * d].astype(BF16),
               jnp.pad(fox_w_in[j, :, 3 * d:], ((0, 0), (0, pad))).astype(BF16),
               jnp.pad(fox_b_f[j].astype(F32), (0, pad)).reshape(1, LANES))
              for j in range(fox_w_in.shape[0])]
    wts = dict(
        ret_w_in=ret_w_in.astype(BF16), ret_gn_w=ret_gn_w.astype(F32), ret_w_out=ret_w_out.astype(BF16),
        hg_w_in=hg_w_in.astype(BF16), hg_b_f=hg_b_f.astype(F32), lbs=lbs, hg_norm_w=hg_norm_w.astype(F32),
        hg_w_out=hg_w_out.astype(BF16), fox_in=fox_in, fox_w_out=fox_w_out.astype(BF16),
        router_w=jnp.pad(router_w, ((0, 0), (0, LANES - N_EXPERTS))).astype(BF16), router_b=router_b,
        moe_w_gate=moe_w_gate.astype(BF16), moe_w_up=moe_w_up.astype(BF16), moe_w_down=moe_w_down.astype(BF16),
        ln_mix_g=ln_mix_g.astype(F32), ln_mix_b=ln_mix_b.astype(F32),
        ln_ffn_g=ln_ffn_g.astype(F32), ln_ffn_b=ln_ffn_b.astype(F32))
    past_len = cache_fox_k.shape[2]
    yp, op = _stream(x_prompt, mods_p, wts, None, None, None, None, None, 0, 256)
    ys, os_ = _stream(x_sample, mods_s, wts, state_ret, state_hgrn, cache_fox_k, cache_fox_v,
                      cache_fox_logf, past_len, 128)
    st = lambda xs: jnp.stack(xs).astype(dt)
    return (yp, ys, st(op['ret']), st(os_['ret']), st(op['hg']), st(os_['hg']),
            st(op['fk']), st(op['fv']), st(op['fl']), st(os_['fk']), st(os_['fv']), st(os_['fl']))
```

```python
import functools

import jax
import jax.numpy as jnp
from jax import lax
from jax.experimental import pallas as pl
from jax.experimental.pallas import tpu as pltpu

F32 = jnp.float32
BF16 = jnp.bfloat16

D_MODEL = 2048
DEPTH = 4
CHUNK = 64
N_MIXERS = 3
RET_HEADS = 8
RET_DK = D_MODEL // RET_HEADS
RET_DV = 2 * RET_DK
RET_QK = RET_HEADS * RET_DK
RET_V = RET_HEADS * RET_DV
RET_GROUP = 8
ROPE_BASE = 10000.0
HG_DK = 128
HG_HEADS = D_MODEL // HG_DK
HG_DV = D_MODEL // HG_HEADS
HG_BLOCK = 16
FOX_HEADS = 16
FOX_HD = D_MODEL // FOX_HEADS
N_EXPERTS = 16
N_GROUPS = 4
EXPERTS_PER_GROUP = N_EXPERTS // N_GROUPS
TOPK_GROUP = 1
TOP_K = 2
D_EXPERT = D_MODEL // 2
ALPHA = (2 * DEPTH) ** 0.25
LN_EPS = 1e-5
NORM_EPS = 1e-6
LOG2E = 1.4426950408889634

LANES = 128
VMEM_LIMIT = 56 * 1024 * 1024
DMA_UNROLL = 8


def _params(*sem):
    return pltpu.CompilerParams(dimension_semantics=sem, vmem_limit_bytes=VMEM_LIMIT)


def _tile(n, pref):
    t = min(n, pref)
    while n % t:
        t //= 2
    return t


def _mod_kernel(c_ref, w_ref, b_ref, o_ref):
    c = c_ref[...]
    a = (c * jax.nn.sigmoid(c)).astype(BF16)
    o_ref[0] = jnp.dot(a, w_ref[0].astype(BF16), preferred_element_type=F32) + b_ref[0]


def _modulation_all(c_all, ada_w, ada_b):
    nb = c_all.shape[0]
    depth, d, n = ada_w.shape
    tn = _tile(n, 1024)
    return pl.pallas_call(
        _mod_kernel,
        out_shape=jax.ShapeDtypeStruct((depth, nb, n), F32),
        grid=(depth, n // tn),
        in_specs=[pl.BlockSpec((nb, d), lambda l, j: (0, 0)),
                  pl.BlockSpec((1, d, tn), lambda l, j: (l, 0, j)),
                  pl.BlockSpec((1, 1, tn), lambda l, j: (l, 0, j))],
        out_specs=pl.BlockSpec((1, nb, tn), lambda l, j: (l, 0, j)),
        compiler_params=_params("parallel", "parallel"),
        name="modulation",
    )(c_all, ada_w, ada_b.reshape(depth, 1, n))


def _mod_spec(mod, bps):
    return pl.BlockSpec((1,) + mod.shape[1:], lambda i, *_: (i // bps, 0, 0))


def _inproj_kernel(x_ref, w_ref, *rest, modulated):
    if modulated:
        sc_ref, sh_ref, o_ref, xb_ref = rest
    else:
        o_ref, xb_ref = rest

    @pl.when(pl.program_id(1) == 0)
    def _():
        x = x_ref[...]
        if modulated:
            x = x * (1.0 + sc_ref[0]) + sh_ref[0]
        xb_ref[...] = x.astype(BF16)

    o_ref[...] = jnp.dot(xb_ref[...], w_ref[...], preferred_element_type=F32)


def _inproj(u, w, mod=None):
    t, d = u.shape
    n = w.shape[1]
    tm = _tile(t, 1024)
    tn = _tile(n, 1024)
    in_specs = [pl.BlockSpec((tm, d), lambda i, j: (i, 0)),
                pl.BlockSpec((d, tn), lambda i, j: (0, j))]
    args = [u, w]
    if mod is not None:
        sc, sh, seq_rows = mod
        if seq_rows is None:
            sc, sh, bps = sc.reshape(t // tm, tm, d), sh.reshape(t // tm, tm, d), 1
        else:
            tm = _tile(seq_rows, tm)
            bps = seq_rows // tm
            in_specs[0] = pl.BlockSpec((tm, d), lambda i, j: (i, 0))
        in_specs += [_mod_spec(sc, bps), _mod_spec(sh, bps)]
        args += [sc, sh]
    return pl.pallas_call(
        functools.partial(_inproj_kernel, modulated=mod is not None),
        out_shape=jax.ShapeDtypeStruct((t, n), F32),
        grid=(t // tm, n // tn),
        in_specs=in_specs,
        out_specs=pl.BlockSpec((tm, tn), lambda i, j: (i, j)),
        scratch_shapes=[pltpu.VMEM((tm, d), BF16)],
        compiler_params=_params("parallel", "arbitrary"),
        name="inproj",
    )(*args)


def _fox_inproj_kernel(x_ref, w_ref, *outs, tok, heads):
    acc = jnp.dot(x_ref[...].astype(BF16), w_ref[...], preferred_element_type=F32)
    n = 0
    if tok:
        outs[n][...] = acc
        n += 1
    if heads:
        for h in range(FOX_HEADS):
            outs[n][0, h] = acc[:, h * FOX_HD:(h + 1) * FOX_HD].astype(BF16)


def _fox_inproj(u, w, b, l, tm, tok, heads):
    t, d = u.shape
    bps = l // tm
    out_shape, out_specs = [], []
    if tok:
        out_shape.append(jax.ShapeDtypeStruct((t, d), F32))
        out_specs.append(pl.BlockSpec((tm, d), lambda i: (i, 0)))
    if heads:
        out_shape.append(jax.ShapeDtypeStruct((b, FOX_HEADS, l, FOX_HD), BF16))
        out_specs.append(pl.BlockSpec((1, FOX_HEADS, tm, FOX_HD), lambda i: (i // bps, 0, i % bps, 0)))
    return pl.pallas_call(
        functools.partial(_fox_inproj_kernel, tok=tok, heads=heads),
        out_shape=out_shape,
        grid=(t // tm,),
        in_specs=[pl.BlockSpec((tm, d), lambda i: (i, 0)),
                  pl.BlockSpec((d, d), lambda i: (0, 0))],
        out_specs=out_specs,
        compiler_params=_params("parallel"),
        name="fox_inproj",
    )(u, w)


def _split3(x):
    hi = x.astype(BF16)
    r1 = x - hi.astype(F32)
    mid = r1.astype(BF16)
    lo = (r1 - mid.astype(F32)).astype(BF16)
    return hi, mid, lo


def _tri_cumsum(tri, x):
    hi, mid, lo = _split3(x)
    return (jnp.dot(tri, hi, preferred_element_type=F32) + jnp.dot(tri, mid, preferred_element_type=F32)
            + jnp.dot(tri, lo, preferred_element_type=F32))


def _fox_gate_kernel(x_ref, w_ref, b_ref, tri_ref, o_ref, c_ref, carry_ref):
    @pl.when(pl.program_id(1) == 0)
    def _():
        carry_ref[...] = jnp.zeros_like(carry_ref)

    z = jnp.dot(x_ref[...].astype(BF16), w_ref[...], preferred_element_type=F32) + b_ref[...]
    logf = jnp.minimum(z, 0.0) - jnp.log1p(jnp.exp(-jnp.abs(z)))
    o_ref[...] = logf
    csum = _tri_cumsum(tri_ref[...], logf) + carry_ref[...]
    c_ref[...] = csum
    carry_ref[...] = csum[csum.shape[0] - 1:, :]


def _fox_gate(u, w, b, nb, l):
    t, d = u.shape
    tm = _tile(l, 256)
    bps = l // tm
    r = jnp.arange(tm)
    tri = (r[None, :] <= r[:, None]).astype(BF16)
    row = pl.BlockSpec((tm, LANES), lambda bi, li: (bi * bps + li, 0))
    return pl.pallas_call(
        _fox_gate_kernel,
        out_shape=[jax.ShapeDtypeStruct((t, LANES), F32), jax.ShapeDtypeStruct((t, LANES), F32)],
        grid=(nb, bps),
        in_specs=[pl.BlockSpec((tm, d), lambda bi, li: (bi * bps + li, 0)),
                  pl.BlockSpec((d, LANES), lambda bi, li: (0, 0)),
                  pl.BlockSpec((1, LANES), lambda bi, li: (0, 0)),
                  pl.BlockSpec((tm, tm), lambda bi, li: (0, 0))],
        out_specs=[row, row],
        scratch_shapes=[pltpu.VMEM((1, LANES), F32)],
        compiler_params=_params("parallel", "arbitrary"),
        name="fox_gate",
    )(u, w, b, tri)


def _retention_kernel(lg_ref, q_ref, k_ref, v_ref, g_ref, cos_ref, sin_ref, gn_ref, dec_ref, *rest,
                      lb, has_state):
    if has_state:
        s0_ref, y_ref, sout_ref, s_ref = rest
    else:
        y_ref, sout_ref, s_ref = rest
    hg = pl.program_id(1)
    li = pl.program_id(2)

    @pl.when(li == 0)
    def _():
        if has_state:
            s_ref[...] = s0_ref[0]
        else:
            s_ref[...] = jnp.zeros_like(s_ref)

    half = RET_DK // 2
    idx = lax.broadcasted_iota(jnp.int32, (lb, 1), 0).astype(F32)
    cos, sin = cos_ref[...], sin_ref[...]

    def rope(x):
        x1, x2 = x[:, :half], x[:, half:]
        return jnp.concatenate([x1 * cos - x2 * sin, x1 * sin + x2 * cos], axis=-1)

    for j in range(RET_GROUP):
        lg = lg_ref[hg * RET_GROUP + j]
        q_dec = jnp.exp((idx + 1.0) * lg)
        k_dec = jnp.exp((lb - 1.0 - idx) * lg)
        s_dec = jnp.exp(jnp.full((1, 1), lb, F32) * lg)
        qk_cols = slice(j * RET_DK, (j + 1) * RET_DK)
        v_cols = slice(j * RET_DV, (j + 1) * RET_DV)
        q = rope(q_ref[0, :, qk_cols])
        k = rope(k_ref[0, :, qk_cols]) * (RET_DK ** -0.5)
        vb = v_ref[0, :, v_cols].astype(BF16)
        scores = lax.dot_general(q.astype(BF16), k.astype(BF16), (((1,), (1,)), ((), ())),
                                 preferred_element_type=F32) * dec_ref[j]
        s = s_ref[j]
        o = (jnp.dot(scores.astype(BF16), vb, preferred_element_type=F32)
             + jnp.dot((q * q_dec).astype(BF16), s.astype(BF16), preferred_element_type=F32))
        kd = (k * k_dec).T.astype(BF16)
        s_ref[j] = s * s_dec + jnp.dot(kd, vb, preferred_element_type=F32)
        mu = jnp.mean(o, axis=-1, keepdims=True)
        oc = o - mu
        var = jnp.mean(oc * oc, axis=-1, keepdims=True)
        y = oc * lax.rsqrt(var + NORM_EPS) * gn_ref[:, v_cols]
        g = g_ref[0, :, v_cols]
        y_ref[0, :, v_cols] = (g * jax.nn.sigmoid(g) * y).astype(BF16)

    @pl.when(li == pl.num_programs(2) - 1)
    def _():
        sout_ref[0] = s_ref[...]


def _retention(proj, cos, sin, log_gamma, gn_w, s0, b, l):
    cl = min(l, CHUNK)
    lb = _tile(l, 4 * cl)
    p3 = proj.reshape(b, l, proj.shape[1])
    ng = RET_HEADS // RET_GROUP
    qw, vw = RET_GROUP * RET_DK, RET_GROUP * RET_DV
    has_state = s0 is not None
    in_specs = [pl.BlockSpec(memory_space=pltpu.SMEM),
                pl.BlockSpec((1, lb, qw), lambda bi, h, li: (bi, li, h)),
                pl.BlockSpec((1, lb, qw), lambda bi, h, li: (bi, li, ng + h)),
                pl.BlockSpec((1, lb, vw), lambda bi, h, li: (bi, li, ng + h)),
                pl.BlockSpec((1, lb, vw), lambda bi, h, li: (bi, li, 2 * ng + h)),
                pl.BlockSpec((lb, RET_DK // 2), lambda bi, h, li: (li, 0)),
                pl.BlockSpec((lb, RET_DK // 2), lambda bi, h, li: (li, 0)),
                pl.BlockSpec((1, vw), lambda bi, h, li: (0, h)),
                pl.BlockSpec((RET_GROUP, lb, lb), lambda bi, h, li: (h, 0, 0))]
    pos = jnp.arange(lb)
    dt = pos[:, None] - pos[None, :]
    same = (pos[:, None] // cl) == (pos[None, :] // cl)
    dist = jnp.where(same, jnp.abs(dt), dt).astype(F32)
    decay = jnp.where((same | (dt > 0))[None], jnp.exp(dist[None] * log_gamma[:, None, None]), 0.0)
    args = [log_gamma, p3, p3, p3, p3, cos, sin, gn_w.reshape(1, RET_V), decay]
    state_spec = pl.BlockSpec((1, RET_GROUP, RET_DK, RET_DV), lambda bi, h, li: (bi, h, 0, 0))
    if has_state:
        in_specs.append(state_spec)
        args.append(s0)
    y, s = pl.pallas_call(
        functools.partial(_retention_kernel, lb=lb, has_state=has_state),
        out_shape=[jax.ShapeDtypeStruct((b, l, RET_V), BF16),
                   jax.ShapeDtypeStruct((b, RET_HEADS, RET_DK, RET_DV), F32)],
        grid=(b, ng, l // lb),
        in_specs=in_specs,
        out_specs=[pl.BlockSpec((1, lb, vw), lambda bi, h, li: (bi, li, h)), state_spec],
        scratch_shapes=[pltpu.VMEM((RET_GROUP, RET_DK, RET_DV), F32)],
        compiler_params=_params("parallel", "parallel", "arbitrary"),
        name="retention",
    )(*args)
    return y.reshape(b * l, RET_V), s


def _hgrn_kernel(q_ref, fz_ref, v_ref, g_ref, bf_ref, lb_ref, nw_ref, tri_ref, *rest, lb_rows, has_state):
    if has_state:
        s0_ref, y_ref, sout_ref, st_ref, gc_ref, k_ref, o_ref = rest
    else:
        y_ref, sout_ref, st_ref, gc_ref, k_ref, o_ref = rest
    li = pl.program_id(1)
    hb = HG_BLOCK
    half = hb // 2

    @pl.when(li == 0)
    def _():
        for h in range(HG_HEADS):
            if has_state:
                st_ref[h] = s0_ref[0, h].T
            else:
                st_ref[h] = jnp.zeros((HG_DV, HG_DK), F32)

    lbv = lb_ref[...]
    f = lbv + (1.0 - lbv) * jax.nn.sigmoid(fz_ref[0] + bf_ref[...])
    logf = jnp.log(f)
    k_ref[...] = 1.0 - f
    gc_ref[...] = _tri_cumsum(tri_ref[...], logf) * LOG2E

    rt = lax.broadcasted_iota(jnp.int32, (half, 1), 0)
    contract_last = (((1,), (1,)), ((), ()))

    def block(bi, carry):
        r0 = pl.multiple_of(bi * hb, hb)
        for h in range(HG_HEADS):
            cs = slice(h * HG_DK, (h + 1) * HG_DK)
            gb = gc_ref[pl.ds(r0, hb), cs]
            qb = q_ref[0, pl.ds(r0, hb), cs]
            kb = k_ref[pl.ds(r0, hb), cs]
            vb = v_ref[0, pl.ds(r0, hb), cs]
            q_top, q_bot = qb[:half], qb[half:]
            g_top, g_bot = gb[:half], gb[half:]
            gk = gb - jnp.log2(kb)
            i_top = jnp.zeros((half, HG_DV), F32)
            i_bot = jnp.zeros((half, HG_DV), F32)
            for s in range(hb):
                gs, vs = gk[s:s + 1], vb[s:s + 1]
                if s < half:
                    e = jnp.where(rt >= s, jnp.exp2(g_top - gs), 0.0)
                    a = jnp.sum(q_top * e, axis=-1, keepdims=True)
                    i_top = i_top + a * vs
                    e = jnp.exp2(g_bot - gs)
                else:
                    e = jnp.where(rt + half >= s, jnp.exp2(g_bot - gs), 0.0)
                a = jnp.sum(q_bot * e, axis=-1, keepdims=True)
                i_bot = i_bot + a * vs
            intra = jnp.concatenate([i_top, i_bot], axis=0)
            st = st_ref[h]
            qt = (qb * jnp.exp2(gb)).astype(BF16)
            inter = lax.dot_general(qt, st.astype(BF16), contract_last, preferred_element_type=F32)
            o_ref[pl.ds(r0, hb), cs] = intra + inter
            gl = gb[hb - 1:hb]
            kt = (kb * jnp.exp2(gl - gb)).astype(BF16)
            upd = jnp.dot(vb.T.astype(BF16), kt, preferred_element_type=F32)
            st_ref[h] = st * jnp.exp2(gl) + upd
        return carry

    lax.fori_loop(0, lb_rows // hb, block, 0)

    for h in range(HG_HEADS):
        cs = slice(h * HG_DK, (h + 1) * HG_DK)
        oh = o_ref[:, cs]
        on = oh * lax.rsqrt(jnp.mean(oh * oh, axis=-1, keepdims=True) + NORM_EPS)
        g = g_ref[0, :, cs]
        y_ref[0, :, cs] = (on * nw_ref[:, cs] * (g * jax.nn.sigmoid(g))).astype(BF16)

    @pl.when(li == pl.num_programs(1) - 1)
    def _():
        for h in range(HG_HEADS):
            sout_ref[0, h] = st_ref[h].T


def _hgrn(proj, b_f, lb, norm_w, s0, b, l):
    assert l % HG_BLOCK == 0
    lbr = _tile(l, 256)
    p3 = proj.reshape(b, l, 4 * D_MODEL)
    r = jnp.arange(lbr)
    tri = ((r[:, None] // HG_BLOCK == r[None, :] // HG_BLOCK) & (r[None, :] <= r[:, None])).astype(BF16)
    has_state = s0 is not None
    col = lambda j: pl.BlockSpec((1, lbr, D_MODEL), lambda bi, li: (bi, li, j))
    vec = pl.BlockSpec((1, D_MODEL), lambda bi, li: (0, 0))
    in_specs = [col(0), col(1), col(2), col(3), vec, vec, vec,
                pl.BlockSpec((lbr, lbr), lambda bi, li: (0, 0))]
    args = [p3, p3, p3, p3, b_f.reshape(1, D_MODEL), lb.reshape(1, D_MODEL), norm_w.reshape(1, D_MODEL), tri]
    state_spec = pl.BlockSpec((1, HG_HEADS, HG_DK, HG_DV), lambda bi, li: (bi, 0, 0, 0))
    if has_state:
        in_specs.append(state_spec)
        args.append(s0)
    y, s = pl.pallas_call(
        functools.partial(_hgrn_kernel, lb_rows=lbr, has_state=has_state),
        out_shape=[jax.ShapeDtypeStruct((b, l, D_MODEL), BF16),
                   jax.ShapeDtypeStruct((b, HG_HEADS, HG_DK, HG_DV), F32)],
        grid=(b, l // lbr),
        in_specs=in_specs,
        out_specs=[pl.BlockSpec((1, lbr, D_MODEL), lambda bi, li: (bi, li, 0)), state_spec],
        scratch_shapes=[pltpu.VMEM((HG_HEADS, HG_DV, HG_DK), F32),
                        pltpu.VMEM((lbr, D_MODEL), F32),
                        pltpu.VMEM((lbr, D_MODEL), F32),
                        pltpu.VMEM((lbr, D_MODEL), F32)],
        compiler_params=_params("parallel", "arbitrary"),
        name="hgrn2",
    )(*args)
    return y.reshape(b * l, D_MODEL), s


def _fox_prompt_kernel(q_ref, k_ref, v_ref, cq_ref, ck_ref, o_ref, *, l, tq, tk):
    scale2 = FOX_HD ** -0.5 * LOG2E
    row = lax.broadcasted_iota(jnp.int32, (tq, tk), 0)
    col = lax.broadcasted_iota(jnp.int32, (tq, tk), 1)
    diag_bias = {}
    for qi in range(l // tq):
        for kj in range(l // tk):
            q0, k0 = qi * tq, kj * tk
            if k0 <= q0 + tq - 1 and k0 + tk - 1 > q0 and q0 - k0 not in diag_bias:
                diag_bias[q0 - k0] = jnp.where(col <= row + (q0 - k0), 0.0, -jnp.inf)
    for qi in range(l // tq):
        q0 = qi * tq
        q = q_ref[0, 0, q0:q0 + tq, :]
        cq = cq_ref[0, 0, q0:q0 + tq, :]
        m = jnp.full((tq, 1), -jnp.inf, F32)
        den = jnp.zeros((tq, 1), F32)
        acc = jnp.zeros((tq, FOX_HD), F32)
        for kj in range(l // tk):
            k0 = kj * tk
            if k0 > q0 + tq - 1:
                continue
            s = lax.dot_general(q, k_ref[0, 0, k0:k0 + tk, :], (((1,), (1,)), ((), ())),
                                preferred_element_type=F32) * scale2
            s = s + (cq - ck_ref[0, 0, :, k0:k0 + tk])
            if k0 + tk - 1 > q0:
                s = s + diag_bias[q0 - k0]
            m_new = jnp.maximum(m, jnp.max(s, axis=-1, keepdims=True))
            w = jnp.exp2(m - m_new)
            p = jnp.exp2(s - m_new)
            den = den * w + jnp.sum(p, axis=-1, keepdims=True)
            acc = acc * w + jnp.dot(p.astype(BF16), v_ref[0, 0, k0:k0 + tk, :], preferred_element_type=F32)
            m = m_new
        o_ref[0, q0:q0 + tq, :] = (acc / den).astype(BF16)


def _fox_prompt_attend(q, k, v, csum, b, l):
    tq = _tile(l, 256)
    tk = _tile(l, 512)
    csum = csum * LOG2E
    cq = csum.transpose(0, 2, 1)[..., None]
    ck = csum.transpose(0, 2, 1)[:, :, None, :]
    head = pl.BlockSpec((1, 1, l, FOX_HD), lambda bi, h: (bi, h, 0, 0))
    o = pl.pallas_call(
        functools.partial(_fox_prompt_kernel, l=l, tq=tq, tk=tk),
        out_shape=jax.ShapeDtypeStruct((b, l, D_MODEL), BF16),
        grid=(b, FOX_HEADS),
        in_specs=[head, head, head,
                  pl.BlockSpec((1, 1, l, 1), lambda bi, h: (bi, h, 0, 0)),
                  pl.BlockSpec((1, 1, 1, l), lambda bi, h: (bi, h, 0, 0))],
        out_specs=pl.BlockSpec((1, l, FOX_HD), lambda bi, h: (bi, 0, h)),
        compiler_params=_params("parallel", "parallel"),
        name="fox_prompt_attention",
    )(q, k, v, cq, ck)
    return o.reshape(b * l, D_MODEL)


def _fox_sample_kernel(q_ref, kn_ref, vn_ref, kc_ref, vc_ref, cq_ref, ckc_ref, ckn_ref, o_ref, *, l, pc):
    nh = FOX_HEADS
    lshift = l.bit_length() - 1
    scale = FOX_HD ** -0.5
    contract_last = (((1,), (1,)), ((), ()))
    by_head = lambda ref: jnp.concatenate([ref[0, :, h * FOX_HD:(h + 1) * FOX_HD] for h in range(nh)],
                                          axis=0).astype(BF16)
    q = by_head(q_ref)
    cq = cq_ref[0]
    rows = nh * l
    qhead = jnp.right_shift(lax.broadcasted_iota(jnp.int32, (rows, 1), 0), lshift)
    kn, vn = by_head(kn_ref), by_head(vn_ref)
    col = lax.broadcasted_iota(jnp.int32, (1, rows), 1)
    qframe = jnp.bitwise_and(lax.broadcasted_iota(jnp.int32, (rows, 1), 0), l - 1)
    visible = (jnp.right_shift(col, lshift) == qhead) & (jnp.bitwise_and(col, l - 1) <= qframe)
    s = lax.dot_general(q, kn, contract_last, preferred_element_type=F32) * scale + (cq - ckn_ref[0])
    s = jnp.where(visible, s, -jnp.inf)
    m = jnp.max(s, axis=-1, keepdims=True)
    p = jnp.exp(s - m)
    den = jnp.sum(p, axis=-1, keepdims=True)
    acc = jnp.dot(p.astype(BF16), vn, preferred_element_type=F32)
    same_head = jnp.bitwise_and(lax.broadcasted_iota(jnp.int32, (1, pc * nh), 1), nh - 1) == qhead
    for c in range(kc_ref.shape[1] // (pc * nh)):
        ks = slice(c * pc * nh, (c + 1) * pc * nh)
        s = lax.dot_general(q, kc_ref[0, ks, :].astype(BF16), contract_last,
                            preferred_element_type=F32) * scale + (cq - ckc_ref[0, :, ks])
        s = jnp.where(same_head, s, -jnp.inf)
        m_new = jnp.maximum(m, jnp.max(s, axis=-1, keepdims=True))
        w = jnp.exp(m - m_new)
        p = jnp.exp(s - m_new)
        den = den * w + jnp.sum(p, axis=-1, keepdims=True)
        acc = acc * w + jnp.dot(p.astype(BF16), vc_ref[0, ks, :].astype(BF16), preferred_element_type=F32)
        m = m_new
    out = (acc / den).astype(BF16)
    for h in range(nh):
        o_ref[0, :, h * FOX_HD:(h + 1) * FOX_HD] = out[h * l:(h + 1) * l, :]


def _fox_sample_attend(q, kn, vn, cache_k, cache_v, csum, b, l):
    p = cache_k.shape[1]
    nh = FOX_HEADS
    assert nh & (nh - 1) == 0 and l & (l - 1) == 0
    pc = _tile(p, 256)
    c_new = csum[:, p:, :].transpose(0, 2, 1).reshape(b, nh * l)
    cq = c_new[:, :, None]
    ckn = c_new[:, None, :]
    ckc = csum[:, :p, :].reshape(b, 1, p * nh)
    tok = pl.BlockSpec((1, l, D_MODEL), lambda bi: (bi, 0, 0))
    cache = pl.BlockSpec((1, p * nh, FOX_HD), lambda bi: (bi, 0, 0))
    o = pl.pallas_call(
        functools.partial(_fox_sample_kernel, l=l, pc=pc),
        out_shape=jax.ShapeDtypeStruct((b, l, D_MODEL), BF16),
        grid=(b,),
        in_specs=[tok, tok, tok, cache, cache,
                  pl.BlockSpec((1, nh * l, 1), lambda bi: (bi, 0, 0)),
                  pl.BlockSpec((1, 1, p * nh), lambda bi: (bi, 0, 0)),
                  pl.BlockSpec((1, 1, nh * l), lambda bi: (bi, 0, 0))],
        out_specs=tok,
        compiler_params=_params("parallel"),
        name="fox_sample_attention",
    )(q.reshape(b, l, D_MODEL), kn.reshape(b, l, D_MODEL), vn.reshape(b, l, D_MODEL),
      cache_k.reshape(b, p * nh, FOX_HD), cache_v.reshape(b, p * nh, FOX_HD), cq, ckc, ckn)
    return o.reshape(b * l, D_MODEL)


def _layer_norm(z, g, b):
    mu = jnp.mean(z, axis=-1, keepdims=True)
    zc = z - mu
    var = jnp.mean(zc * zc, axis=-1, keepdims=True)
    return zc * lax.rsqrt(var + LN_EPS) * g + b


def _outproj_norm_kernel(y_ref, w_ref, x_ref, gate_ref, sc_ref, sh_ref, lng_ref, lnb_ref, rw_ref,
                         xn_ref, u_ref, s_ref):
    out = jnp.dot(y_ref[...], w_ref[...], preferred_element_type=F32)
    z = ALPHA * x_ref[...] + (1.0 + gate_ref[0]) * out
    xn = _layer_norm(z, lng_ref[...], lnb_ref[...])
    xn_ref[...] = xn
    u = xn * (1.0 + sc_ref[0]) + sh_ref[0]
    u_ref[...] = u
    s_ref[...] = jax.nn.sigmoid(jnp.dot(u.astype(BF16), rw_ref[...], preferred_element_type=F32))


def _outproj_norm(y, w, x, gate, sc, sh, ln_g, ln_b, rw, tm, bps):
    t, kdim = y.shape
    d = w.shape[1]
    row = pl.BlockSpec((tm, d), lambda i: (i, 0))
    vec = pl.BlockSpec((1, d), lambda i: (0, 0))
    once = pl.Buffered(1)
    return pl.pallas_call(
        _outproj_norm_kernel,
        out_shape=[jax.ShapeDtypeStruct((t, d), F32), jax.ShapeDtypeStruct((t, d), F32),
                   jax.ShapeDtypeStruct((t, LANES), F32)],
        grid=(t // tm,),
        in_specs=[pl.BlockSpec((tm, kdim), lambda i: (i, 0)),
                  pl.BlockSpec((kdim, d), lambda i: (0, 0), pipeline_mode=once),
                  row, _mod_spec(gate, bps), _mod_spec(sc, bps), _mod_spec(sh, bps), vec, vec,
                  pl.BlockSpec((d, LANES), lambda i: (0, 0), pipeline_mode=once)],
        out_specs=[row, row, pl.BlockSpec((tm, LANES), lambda i: (i, 0))],
        compiler_params=_params("parallel"),
        name="outproj_norm",
    )(y, w, x, gate, sc, sh, ln_g.reshape(1, d), ln_b.reshape(1, d), rw)


def _argmax_first(vals):
    best, idx = vals[0], jnp.zeros(vals[0].shape, jnp.int32)
    for j in range(1, len(vals)):
        gt = vals[j] > best
        best = jnp.where(gt, vals[j], best)
        idx = jnp.where(gt, j, idx)
    return best, idx


def _pick(rows, idx):
    out = rows[0]
    for j in range(1, len(rows)):
        out = jnp.where(idx == j, rows[j], out)
    return out


def _route_kernel(s_ref, b_ref, tri_ref, e_ref, r_ref, w_ref, cnt_ref):
    i = pl.program_id(0)

    @pl.when(i == 0)
    def _():
        cnt_ref[...] = jnp.zeros_like(cnt_ref)

    sc = s_ref[...].T[:N_EXPERTS, :]
    sel = sc + b_ref[...]
    row = lambda a, e: a[e:e + 1, :]
    grp = []
    for g in range(N_GROUPS):
        a, b, c, d = (row(sel, g * EXPERTS_PER_GROUP + j) for j in range(EXPERTS_PER_GROUP))
        hi1, lo1, hi2, lo2 = jnp.maximum(a, b), jnp.minimum(a, b), jnp.maximum(c, d), jnp.minimum(c, d)
        grp.append(jnp.maximum(hi1, hi2) + jnp.maximum(jnp.minimum(hi1, hi2), jnp.maximum(lo1, lo2)))
    _, gidx = _argmax_first(grp)
    member = lambda a: [_pick([row(a, g * EXPERTS_PER_GROUP + j) for g in range(N_GROUPS)], gidx)
                        for j in range(EXPERTS_PER_GROUP)]
    v, c = member(sel), member(sc)
    _, j0 = _argmax_first(v)
    _, j1 = _argmax_first([jnp.where(j0 == j, -jnp.inf, v[j]) for j in range(EXPERTS_PER_GROUP)])
    c0, c1 = _pick(c, j0), _pick(c, j1)
    den = c0 + c1
    e0 = gidx * EXPERTS_PER_GROUP + j0
    e1 = gidx * EXPERTS_PER_GROUP + j1
    eio = lax.broadcasted_iota(jnp.int32, sc.shape, 0)
    oh0, oh1 = eio == e0, eio == e1
    member_f = jnp.where(oh0 | oh1, 1.0, 0.0)
    before = jnp.dot(member_f.astype(BF16), tri_ref[...], preferred_element_type=F32) + cnt_ref[...]
    r0 = jnp.sum(jnp.where(oh0, before, 0.0), axis=0, keepdims=True)
    r1 = jnp.sum(jnp.where(oh1, before, 0.0), axis=0, keepdims=True)
    cnt_ref[...] += jnp.sum(member_f, axis=1, keepdims=True)
    e_ref[0:1, :] = e0
    e_ref[1:2, :] = e1
    r_ref[0:1, :] = r0.astype(jnp.int32)
    r_ref[1:2, :] = r1.astype(jnp.int32)
    w_ref[0:1, :] = c0 / den
    w_ref[1:2, :] = c1 / den


def _route(scores, router_b, tm_e):
    t = scores.shape[0]
    tm = _tile(t, 512)
    r = jnp.arange(tm)
    tri = (r[:, None] < r[None, :]).astype(BF16)
    slot = pl.BlockSpec((TOP_K, tm), lambda i: (0, i))
    eidx, rank, wts, counts = pl.pallas_call(
        _route_kernel,
        out_shape=[jax.ShapeDtypeStruct((TOP_K, t), jnp.int32), jax.ShapeDtypeStruct((TOP_K, t), jnp.int32),
                   jax.ShapeDtypeStruct((TOP_K, t), F32), jax.ShapeDtypeStruct((N_EXPERTS, 1), F32)],
        grid=(t // tm,),
        in_specs=[pl.BlockSpec((tm, LANES), lambda i: (i, 0)),
                  pl.BlockSpec((N_EXPERTS, 1), lambda i: (0, 0)),
                  pl.BlockSpec((tm, tm), lambda i: (0, 0))],
        out_specs=[slot, slot, slot, pl.BlockSpec((N_EXPERTS, 1), lambda i: (0, 0))],
        compiler_params=_params("arbitrary"),
        name="moe_route",
    )(scores, router_b.astype(F32).reshape(N_EXPERTS, 1), tri)
    counts = counts[:, 0].astype(jnp.int32)
    padded = (counts + tm_e - 1) // tm_e * tm_e
    pad_end = jnp.cumsum(padded)
    pad_start = pad_end - padded
    onehot = eidx[:, :, None] == jnp.arange(N_EXPERTS, dtype=jnp.int32)[None, None, :]
    dest = (jnp.sum(jnp.where(onehot, pad_start[None, None, :], 0), axis=-1) + rank).reshape(TOP_K * t)
    n_blocks = (t * TOP_K + N_EXPERTS * (tm_e - 1) + tm_e - 1) // tm_e
    blk_row = jnp.arange(n_blocks, dtype=jnp.int32) * tm_e
    blk_e = jnp.minimum(jnp.sum((pad_end[None, :] <= blk_row[:, None]).astype(jnp.int32), axis=1), N_EXPERTS - 1)
    n_used = (pad_end[-1] // tm_e).astype(jnp.int32).reshape(1)
    last_blk = jnp.concatenate([jnp.where(counts > 0, pad_end - tm_e, -1).astype(jnp.int32), n_used])
    return dest.astype(jnp.int32), wts.T, blk_e, n_used, n_blocks, last_blk


def _dispatch_kernel(dest_ref, last_ref, u_ref, xs_ref, zero_ref, sem, zsem, *, tb, t_total, tm_e, n_blocks):
    base = pl.program_id(0) * tb

    @pl.when(pl.program_id(0) == 0)
    def _():
        zero_ref[...] = jnp.zeros_like(zero_ref)

        def zero_copy(e):
            row = pl.multiple_of(jnp.maximum(last_ref[e], 0), tm_e)
            return pltpu.make_async_copy(zero_ref, xs_ref.at[pl.ds(row, tm_e), :], zsem)

        for e in range(N_EXPERTS):
            pl.when(last_ref[e] >= 0)(lambda e=e: zero_copy(e).start())
        for e in range(N_EXPERTS):
            pl.when(last_ref[e] >= 0)(lambda e=e: zero_copy(e).wait())

        def tail_copy(blk):
            return pltpu.make_async_copy(zero_ref, xs_ref.at[pl.ds(pl.multiple_of(blk * tm_e, tm_e), tm_e), :], zsem)

        n_used = last_ref[N_EXPERTS]
        lax.fori_loop(n_used, n_blocks, lambda blk, c: (tail_copy(blk).start(), c)[1], 0)
        lax.fori_loop(n_used, n_blocks, lambda blk, c: (tail_copy(blk).wait(), c)[1], 0)

    def row_copy(t, slot):
        d = dest_ref[slot * t_total + base + t]
        return pltpu.make_async_copy(u_ref.at[pl.ds(t, 1), :], xs_ref.at[pl.ds(d, 1), :], sem)

    def issue(t, carry):
        for slot in range(TOP_K):
            row_copy(t, slot).start()
        return carry

    lax.fori_loop(0, tb, issue, 0, unroll=DMA_UNROLL)
    for slot in range(TOP_K):
        pltpu.make_async_copy(u_ref, xs_ref.at[pl.ds(0, tb), :], sem).wait()


def _dispatch(u, dest, last_blk, n_rows, tb, tm_e):
    t, d = u.shape
    return pl.pallas_call(
        functools.partial(_dispatch_kernel, tb=tb, t_total=t, tm_e=tm_e, n_blocks=n_rows // tm_e),
        out_shape=jax.ShapeDtypeStruct((n_rows, d), F32),
        grid_spec=pltpu.PrefetchScalarGridSpec(
            num_scalar_prefetch=2,
            grid=(t // tb,),
            in_specs=[pl.BlockSpec((tb, d), lambda i, dest_ref, last_ref: (i, 0))],
            out_specs=pl.BlockSpec(memory_space=pl.ANY),
            scratch_shapes=[pltpu.VMEM((tm_e, d), F32), pltpu.SemaphoreType.DMA, pltpu.SemaphoreType.DMA],
        ),
        compiler_params=_params("arbitrary"),
        name="moe_dispatch",
    )(dest, last_blk, u)


def _expert_kernel(blk_e_ref, n_used_ref, xs_ref, wg_ref, wu_ref, wd_ref, ys_ref):
    del blk_e_ref
    i = pl.program_id(0)

    @pl.when(i < n_used_ref[0])
    def _():
        x = xs_ref[...].astype(BF16)
        g = jnp.dot(x, wg_ref[0], preferred_element_type=F32)
        up = jnp.dot(x, wu_ref[0], preferred_element_type=F32)
        hid = (g * jax.nn.sigmoid(g) * up).astype(BF16)
        ys_ref[...] = jnp.dot(hid, wd_ref[0], preferred_element_type=F32)

    @pl.when(i >= n_used_ref[0])
    def _():
        ys_ref[...] = jnp.zeros_like(ys_ref)


def _experts(xs, blk_e, n_used, wg, wu, wd, tm_e):
    r, d = xs.shape
    de = wg.shape[2]
    return pl.pallas_call(
        _expert_kernel,
        out_shape=jax.ShapeDtypeStruct((r, d), F32),
        grid_spec=pltpu.PrefetchScalarGridSpec(
            num_scalar_prefetch=2,
            grid=(r // tm_e,),
            in_specs=[pl.BlockSpec((tm_e, d), lambda i, be, nu: (i, 0)),
                      pl.BlockSpec((1, d, de), lambda i, be, nu: (be[i], 0, 0)),
                      pl.BlockSpec((1, d, de), lambda i, be, nu: (be[i], 0, 0)),
                      pl.BlockSpec((1, de, d), lambda i, be, nu: (be[i], 0, 0))],
            out_specs=pl.BlockSpec((tm_e, d), lambda i, be, nu: (i, 0)),
        ),
        compiler_params=_params("arbitrary"),
        name="moe_experts",
    )(blk_e, n_used, xs, wg, wu, wd)


def _combine_norm_kernel(dest_ref, ys_ref, wts_ref, x_ref, gate_ref, lng_ref, lnb_ref, *rest,
                         tb, t_total, next_mod):
    if next_mod:
        sc_ref, sh_ref, xn_ref, u_ref, ya0, ya1, yb0, yb1, sem_a, sem_b = rest
    else:
        sc_ref = sh_ref = u_ref = None
        xn_ref, ya0, ya1, yb0, yb1, sem_a, sem_b = rest
    i = pl.program_id(0)
    n = pl.num_programs(0)
    buf_a, buf_b = (ya0, ya1), (yb0, yb1)

    def start_gathers(blk, bufs, sem):
        base = blk * tb
        for t in range(tb):
            for slot in range(TOP_K):
                d = dest_ref[slot * t_total + base + t]
                pltpu.make_async_copy(ys_ref.at[pl.ds(d, 1), :], bufs[slot].at[pl.ds(t, 1), :], sem).start()

    def wait_gathers(bufs, sem):
        for buf in bufs:
            pltpu.make_async_copy(ys_ref.at[pl.ds(0, tb), :], buf, sem).wait()

    def mod_rows(ref, rows):
        return ref[0] if ref.shape[1] == 1 else ref[0, rows, :]

    def finish(rows, bufs):
        w = wts_ref[rows, :]
        ffn = w[:, 0:1] * bufs[0][...] + w[:, 1:2] * bufs[1][...]
        z = ALPHA * x_ref[rows, :] + (1.0 + mod_rows(gate_ref, rows)) * ffn
        xn = _layer_norm(z, lng_ref[...], lnb_ref[...])
        xn_ref[rows, :] = xn
        if next_mod:
            u_ref[rows, :] = xn * (1.0 + mod_rows(sc_ref, rows)) + mod_rows(sh_ref, rows)

    @pl.when(i == 0)
    def _():
        start_gathers(0, buf_a, sem_a)

    start_gathers(2 * i + 1, buf_b, sem_b)
    wait_gathers(buf_a, sem_a)
    finish(slice(0, tb), buf_a)
    start_gathers(jnp.minimum(2 * i + 2, 2 * n - 2), buf_a, sem_a)
    wait_gathers(buf_b, sem_b)
    finish(slice(tb, 2 * tb), buf_b)

    @pl.when(i == n - 1)
    def _():
        wait_gathers(buf_a, sem_a)


def _combine_norm(ys, dest, wts, x, gate, next_mod, ln_g, ln_b, tb, bps):
    t, d = x.shape
    tb2 = 2 * tb
    assert t % tb2 == 0

    def mod2(m):
        if m.shape[1] == 1:
            assert bps % 2 == 0
            return m, bps // 2
        return m.reshape(t // tb2, tb2, d), 1

    row = pl.BlockSpec((tb2, d), lambda i, dr: (i, 0))
    vec = pl.BlockSpec((1, d), lambda i, dr: (0, 0))
    gate, gbps = mod2(gate)
    in_specs = [pl.BlockSpec(memory_space=pl.ANY),
                pl.BlockSpec((tb2, TOP_K), lambda i, dr: (i, 0)),
                row, _mod_spec(gate, gbps), vec, vec]
    args = [dest, ys, wts, x, gate, ln_g.reshape(1, d), ln_b.reshape(1, d)]
    out_shape = [jax.ShapeDtypeStruct((t, d), F32)]
    if next_mod is not None:
        for m in next_mod:
            m, mbps = mod2(m)
            in_specs.append(_mod_spec(m, mbps))
            args.append(m)
        out_shape.append(jax.ShapeDtypeStruct((t, d), F32))
    outs = pl.pallas_call(
        functools.partial(_combine_norm_kernel, tb=tb, t_total=t, next_mod=next_mod is not None),
        out_shape=out_shape,
        grid_spec=pltpu.PrefetchScalarGridSpec(
            num_scalar_prefetch=1,
            grid=(t // tb2,),
            in_specs=in_specs,
            out_specs=[row] * len(out_shape),
            scratch_shapes=[pltpu.VMEM((tb, d), F32)] * 4 + [pltpu.SemaphoreType.DMA] * 2,
        ),
        compiler_params=_params("arbitrary"),
        name="moe_combine_norm",
    )(*args)
    return outs if next_mod is not None else (outs[0], None)


def _moe_block(x, u, scores, router_b, wg, wu, wd, gate, next_mod, ln_g, ln_b, tm, bps, tm_e):
    dest, wts, blk_e, n_used, n_blocks, last_blk = _route(scores, router_b, tm_e)
    xs = _dispatch(u, dest, last_blk, n_blocks * tm_e, tm, tm_e)
    ys = _experts(xs, blk_e, n_used, wg, wu, wd, tm_e)
    tb = _tile(tm, 256) if gate.shape[1] == 1 else tm
    return _combine_norm(ys, dest, wts, x, gate, next_mod, ln_g, ln_b, tb, bps * (tm // tb))


def _rope_tables(l, pos0):
    half = RET_DK // 2
    inv = ROPE_BASE ** (-jnp.arange(half, dtype=F32) / half)
    ang = (pos0 + jnp.arange(l)).astype(F32)[:, None] * inv[None, :]
    return jnp.cos(ang), jnp.sin(ang)


def _stream(x3, mods, wts, state_ret, state_hgrn, cache_k, cache_v, cache_logf, pos0, tm_e):
    b, l, d = x3.shape
    t = b * l
    fresh = state_ret is None
    if fresh:
        tm = _tile(l, 256)
        bps = l // tm
        expand = lambda m: m[:, None, :]
    else:
        tm = _tile(t, 256)
        bps = 1
        expand = lambda m: jnp.repeat(m, l, axis=0).reshape(t // tm, tm, d)
    x = x3.reshape(t, d)
    outs = dict(ret=[], hg=[], fk=[], fv=[], fl=[])
    log_gamma = jnp.log1p(-jnp.exp2(-5.0 - jnp.arange(RET_HEADS, dtype=F32)))
    cos, sin = _rope_tables(l, pos0)
    u = None
    for layer in range(DEPTH):
        m = [expand(a) for a in mods[layer]]
        kind, j = layer % N_MIXERS, layer // N_MIXERS
        if kind == 0:
            if layer == 0:
                proj = _inproj(x, wts['ret_w_in'][j], (m[1], m[0], l if fresh else None))
            else:
                proj = _inproj(u, wts['ret_w_in'][j])
            s0 = None if fresh else state_ret[j]
            y, s = _retention(proj, cos, sin, log_gamma, wts['ret_gn_w'][j], s0, b, l)
            outs['ret'].append(s)
            w_out = wts['ret_w_out'][j]
        elif kind == 1:
            proj = _inproj(u, wts['hg_w_in'][j])
            s0 = None if fresh else state_hgrn[j]
            y, s = _hgrn(proj, wts['hg_b_f'][j], wts['lbs'][layer], wts['hg_norm_w'][j], s0, b, l)
            outs['hg'].append(s)
            w_out = wts['hg_w_out'][j]
        else:
            wq, wk, wv, wf, bf = wts['fox_in'][j]
            logf, csum = (a[:, :FOX_HEADS].reshape(b, l, FOX_HEADS) for a in _fox_gate(u, wf, bf, b, l))
            if fresh:
                tf = _tile(l, 512)
                (qh,) = _fox_inproj(u, wq, b, l, tf, False, True)
                kt, kh = _fox_inproj(u, wk, b, l, tf, True, True)
                vt, vh = _fox_inproj(u, wv, b, l, tf, True, True)
                y = _fox_prompt_attend(qh, kh, vh, csum, b, l)
            else:
                qt = _inproj(u, wq)
                kt = _inproj(u, wk)
                vt = _inproj(u, wv)
                lf_all = jnp.concatenate([cache_logf[j].astype(F32), logf], axis=1)
                pos = jnp.arange(lf_all.shape[1])
                csum = jnp.einsum('ts,bsh->bth', (pos[None, :] <= pos[:, None]).astype(F32), lf_all,
                                  precision=lax.Precision.HIGHEST)
                y = _fox_sample_attend(qt, kt, vt, cache_k[j], cache_v[j], csum, b, l)
            outs['fk'].append(kt.reshape(b, l, FOX_HEADS, FOX_HD))
            outs['fv'].append(vt.reshape(b, l, FOX_HEADS, FOX_HD))
            outs['fl'].append(logf)
            w_out = wts['fox_w_out'][j]
        x, u, scores = _outproj_norm(y, w_out, x, m[2], m[4], m[3], wts['ln_mix_g'][layer],
                                     wts['ln_mix_b'][layer], wts['router_w'], tm, bps)
        nxt = (expand(mods[layer + 1][1]), expand(mods[layer + 1][0])) if layer + 1 < DEPTH else None
        x, u = _moe_block(x, u, scores, wts['router_b'], wts['moe_w_gate'][layer], wts['moe_w_up'][layer],
                          wts['moe_w_down'][layer], m[5], nxt, wts['ln_ffn_g'][layer],
                          wts['ln_ffn_b'][layer], tm, bps, tm_e)
    return x.reshape(b, l, d), outs


def kernel(x_prompt, x_sample, state_ret, state_hgrn, cache_fox_k, cache_fox_v, cache_fox_logf, c_prompt, c_sample, ada_w, ada_b, ln_mix_g, ln_mix_b, ln_ffn_g, ln_ffn_b, ret_w_in, ret_gn_w, ret_w_out, hg_w_in, hg_b_f, hg_lower_bounds, hg_norm_w, hg_w_out, fox_w_in, fox_b_f, fox_w_out, router_w, router_b, moe_w_gate, moe_w_up, moe_w_down):
    dt = x_prompt.dtype
    d = D_MODEL
    nbp = c_prompt.shape[0]
    lbs = jnp.cumsum(jax.nn.softmax(hg_lower_bounds.astype(F32), axis=0), axis=0)
    lbs = lbs - lbs[0]
    mod_all = _modulation_all(jnp.concatenate([c_prompt, c_sample], axis=0).astype(F32), ada_w, ada_b)
    split6 = lambda m: [m[:, i * d:(i + 1) * d] for i in range(6)]
    mods_p = [split6(mod_all[layer, :nbp]) for layer in range(DEPTH)]
    mods_s = [split6(mod_all[layer, nbp:]) for layer in range(DEPTH)]
    pad = LANES - FOX_HEADS
    fox_in = [(fox_w_in[j, :, :d].astype(BF16), fox_w_in[j, :, d:2 * d].astype(BF16),
               fox_w_in[j, :, 2 * d:3 * d].astype(BF16),
               jnp.pad(fox_w_in[j, :, 3 * d:], ((0, 0), (0, pad))).astype(BF16),
               jnp.pad(fox_b_f[j].astype(F32), (0, pad)).reshape(1, LANES))
              for j in range(fox_w_in.shape[0])]
    wts = dict(
        ret_w_in=ret_w_in.astype(BF16), ret_gn_w=ret_gn_w.astype(F32), ret_w_out=ret_w_out.astype(BF16),
        hg_w_in=hg_w_in.astype(BF16), hg_b_f=hg_b_f.astype(F32), lbs=lbs, hg_norm_w=hg_norm_w.astype(F32),
        hg_w_out=hg_w_out.astype(BF16), fox_in=fox_in, fox_w_out=fox_w_out.astype(BF16),
        router_w=jnp.pad(router_w, ((0, 0), (0, LANES - N_EXPERTS))).astype(BF16), router_b=router_b,
        moe_w_gate=moe_w_gate.astype(BF16), moe_w_up=moe_w_up.astype(BF16), moe_w_down=moe_w_down.astype(BF16),
        ln_mix_g=ln_mix_g.astype(F32), ln_mix_b=ln_mix_b.astype(F32),
        ln_ffn_g=ln_ffn_g.astype(F32), ln_ffn_b=ln_ffn_b.astype(F32))
    past_len = cache_fox_k.shape[2]
    yp, op = _stream(x_prompt, mods_p, wts, None, None, None, None, None, 0, 256)
    ys, os_ = _stream(x_sample, mods_s, wts, state_ret, state_hgrn, cache_fox_k, cache_fox_v,
                      cache_fox_logf, past_len, 128)
    st = lambda xs: jnp.stack(xs).astype(dt)
    return (yp, ys, st(op['ret']), st(os_['ret']), st(op['hg']), st(os_['hg']),
            st(op['fk']), st(op['fv']), st(op['fl']), st(os_['fk']), st(os_['fv']), st(os_['fl']))
```

```python
import functools

import jax
import jax.numpy as jnp
from jax import lax
from jax.experimental import pallas as pl
from jax.experimental.pallas import tpu as pltpu

F32 = jnp.float32
BF16 = jnp.bfloat16

D_MODEL = 2048
DEPTH = 4
CHUNK = 64
N_MIXERS = 3
RET_HEADS = 8
RET_DK = D_MODEL // RET_HEADS
RET_DV = 2 * RET_DK
RET_QK = RET_HEADS * RET_DK
RET_V = RET_HEADS * RET_DV
RET_GROUP = 8
ROPE_BASE = 10000.0
HG_DK = 128
HG_HEADS = D_MODEL // HG_DK
HG_DV = D_MODEL // HG_HEADS
HG_BLOCK = 16
FOX_HEADS = 16
FOX_HD = D_MODEL // FOX_HEADS
N_EXPERTS = 16
N_GROUPS = 4
EXPERTS_PER_GROUP = N_EXPERTS // N_GROUPS
TOPK_GROUP = 1
TOP_K = 2
D_EXPERT = D_MODEL // 2
ALPHA = (2 * DEPTH) ** 0.25
LN_EPS = 1e-5
NORM_EPS = 1e-6
LOG2E = 1.4426950408889634

LANES = 128
VMEM_LIMIT = 56 * 1024 * 1024
DMA_UNROLL = 8


def _params(*sem):
    return pltpu.CompilerParams(dimension_semantics=sem, vmem_limit_bytes=VMEM_LIMIT)


def _tile(n, pref):
    t = min(n, pref)
    while n % t:
        t //= 2
    return t


def _mod_kernel(c_ref, w_ref, b_ref, o_ref):
    c = c_ref[...]
    a = (c * jax.nn.sigmoid(c)).astype(BF16)
    o_ref[0] = jnp.dot(a, w_ref[0].astype(BF16), preferred_element_type=F32) + b_ref[0]


def _modulation_all(c_all, ada_w, ada_b):
    nb = c_all.shape[0]
    depth, d, n = ada_w.shape
    tn = _tile(n, 1024)
    return pl.pallas_call(
        _mod_kernel,
        out_shape=jax.ShapeDtypeStruct((depth, nb, n), F32),
        grid=(depth, n // tn),
        in_specs=[pl.BlockSpec((nb, d), lambda l, j: (0, 0)),
                  pl.BlockSpec((1, d, tn), lambda l, j: (l, 0, j)),
                  pl.BlockSpec((1, 1, tn), lambda l, j: (l, 0, j))],
        out_specs=pl.BlockSpec((1, nb, tn), lambda l, j: (l, 0, j)),
        compiler_params=_params("parallel", "parallel"),
        name="modulation",
    )(c_all, ada_w, ada_b.reshape(depth, 1, n))


def _mod_spec(mod, bps):
    return pl.BlockSpec((1,) + mod.shape[1:], lambda i, *_: (i // bps, 0, 0))


def _inproj_kernel(x_ref, w_ref, *rest, modulated):
    if modulated:
        sc_ref, sh_ref, o_ref, xb_ref = rest
    else:
        o_ref, xb_ref = rest

    @pl.when(pl.program_id(1) == 0)
    def _():
        x = x_ref[...]
        if modulated:
            x = x * (1.0 + sc_ref[0]) + sh_ref[0]
        xb_ref[...] = x.astype(BF16)

    o_ref[...] = jnp.dot(xb_ref[...], w_ref[...], preferred_element_type=F32)


def _inproj(u, w, mod=None):
    t, d = u.shape
    n = w.shape[1]
    tm = _tile(t, 1024)
    tn = _tile(n, 1024)
    in_specs = [pl.BlockSpec((tm, d), lambda i, j: (i, 0)),
                pl.BlockSpec((d, tn), lambda i, j: (0, j))]
    args = [u, w]
    if mod is not None:
        sc, sh, seq_rows = mod
        if seq_rows is None:
            sc, sh, bps = sc.reshape(t // tm, tm, d), sh.reshape(t // tm, tm, d), 1
        else:
            tm = _tile(seq_rows, tm)
            bps = seq_rows // tm
            in_specs[0] = pl.BlockSpec((tm, d), lambda i, j: (i, 0))
        in_specs += [_mod_spec(sc, bps), _mod_spec(sh, bps)]
        args += [sc, sh]
    return pl.pallas_call(
        functools.partial(_inproj_kernel, modulated=mod is not None),
        out_shape=jax.ShapeDtypeStruct((t, n), F32),
        grid=(t // tm, n // tn),
        in_specs=in_specs,
        out_specs=pl.BlockSpec((tm, tn), lambda i, j: (i, j)),
        scratch_shapes=[pltpu.VMEM((tm, d), BF16)],
        compiler_params=_params("parallel", "arbitrary"),
        name="inproj",
    )(*args)


def _fox_inproj_kernel(x_ref, w_ref, *outs, tok, heads):
    acc = jnp.dot(x_ref[...].astype(BF16), w_ref[...], preferred_element_type=F32)
    n = 0
    if tok:
        outs[n][...] = acc
        n += 1
    if heads:
        for h in range(FOX_HEADS):
            outs[n][0, h] = acc[:, h * FOX_HD:(h + 1) * FOX_HD].astype(BF16)


def _fox_inproj(u, w, b, l, tm, tok, heads):
    t, d = u.shape
    bps = l // tm
    out_shape, out_specs = [], []
    if tok:
        out_shape.append(jax.ShapeDtypeStruct((t, d), F32))
        out_specs.append(pl.BlockSpec((tm, d), lambda i: (i, 0)))
    if heads:
        out_shape.append(jax.ShapeDtypeStruct((b, FOX_HEADS, l, FOX_HD), BF16))
        out_specs.append(pl.BlockSpec((1, FOX_HEADS, tm, FOX_HD), lambda i: (i // bps, 0, i % bps, 0)))
    return pl.pallas_call(
        functools.partial(_fox_inproj_kernel, tok=tok, heads=heads),
        out_shape=out_shape,
        grid=(t // tm,),
        in_specs=[pl.BlockSpec((tm, d), lambda i: (i, 0)),
                  pl.BlockSpec((d, d), lambda i: (0, 0))],
        out_specs=out_specs,
        compiler_params=_params("parallel"),
        name="fox_inproj",
    )(u, w)


def _split3(x):
    hi = x.astype(BF16)
    r1 = x - hi.astype(F32)
    mid = r1.astype(BF16)
    lo = (r1 - mid.astype(F32)).astype(BF16)
    return hi, mid, lo


def _tri_cumsum(tri, x):
    hi, mid, lo = _split3(x)
    return (jnp.dot(tri, hi, preferred_element_type=F32) + jnp.dot(tri, mid, preferred_element_type=F32)
            + jnp.dot(tri, lo, preferred_element_type=F32))


def _fox_gate_kernel(x_ref, w_ref, b_ref, tri_ref, o_ref, c_ref, carry_ref):
    @pl.when(pl.program_id(1) == 0)
    def _():
        carry_ref[...] = jnp.zeros_like(carry_ref)

    z = jnp.dot(x_ref[...].astype(BF16), w_ref[...], preferred_element_type=F32) + b_ref[...]
    logf = jnp.minimum(z, 0.0) - jnp.log1p(jnp.exp(-jnp.abs(z)))
    o_ref[...] = logf
    csum = _tri_cumsum(tri_ref[...], logf) + carry_ref[...]
    c_ref[...] = csum
    carry_ref[...] = csum[csum.shape[0] - 1:, :]


def _fox_gate(u, w, b, nb, l):
    t, d = u.shape
    tm = _tile(l, 256)
    bps = l // tm
    r = jnp.arange(tm)
    tri = (r[None, :] <= r[:, None]).astype(BF16)
    row = pl.BlockSpec((tm, LANES), lambda bi, li: (bi * bps + li, 0))
    return pl.pallas_call(
        _fox_gate_kernel,
        out_shape=[jax.ShapeDtypeStruct((t, LANES), F32), jax.ShapeDtypeStruct((t, LANES), F32)],
        grid=(nb, bps),
        in_specs=[pl.BlockSpec((tm, d), lambda bi, li: (bi * bps + li, 0)),
                  pl.BlockSpec((d, LANES), lambda bi, li: (0, 0)),
                  pl.BlockSpec((1, LANES), lambda bi, li: (0, 0)),
                  pl.BlockSpec((tm, tm), lambda bi, li: (0, 0))],
        out_specs=[row, row],
        scratch_shapes=[pltpu.VMEM((1, LANES), F32)],
        compiler_params=_params("parallel", "arbitrary"),
        name="fox_gate",
    )(u, w, b, tri)


def _retention_kernel(lg_ref, q_ref, k_ref, v_ref, g_ref, cos_ref, sin_ref, gn_ref, dec_ref, *rest,
                      lb, has_state):
    if has_state:
        s0_ref, y_ref, sout_ref, s_ref = rest
    else:
        y_ref, sout_ref, s_ref = rest
    hg = pl.program_id(1)
    li = pl.program_id(2)

    @pl.when(li == 0)
    def _():
        if has_state:
            s_ref[...] = s0_ref[0]
        else:
            s_ref[...] = jnp.zeros_like(s_ref)

    half = RET_DK // 2
    idx = lax.broadcasted_iota(jnp.int32, (lb, 1), 0).astype(F32)
    cos, sin = cos_ref[...], sin_ref[...]

    def rope(x):
        x1, x2 = x[:, :half], x[:, half:]
        return jnp.concatenate([x1 * cos - x2 * sin, x1 * sin + x2 * cos], axis=-1)

    for j in range(RET_GROUP):
        lg = lg_ref[hg * RET_GROUP + j]
        q_dec = jnp.exp((idx + 1.0) * lg)
        k_dec = jnp.exp((lb - 1.0 - idx) * lg)
        s_dec = jnp.exp(jnp.full((1, 1), lb, F32) * lg)
        qk_cols = slice(j * RET_DK, (j + 1) * RET_DK)
        v_cols = slice(j * RET_DV, (j + 1) * RET_DV)
        q = rope(q_ref[0, :, qk_cols])
        k = rope(k_ref[0, :, qk_cols]) * (RET_DK ** -0.5)
        vb = v_ref[0, :, v_cols].astype(BF16)
        scores = lax.dot_general(q.astype(BF16), k.astype(BF16), (((1,), (1,)), ((), ())),
                                 preferred_element_type=F32) * dec_ref[j]
        s = s_ref[j]
        o = (jnp.dot(scores.astype(BF16), vb, preferred_element_type=F32)
             + jnp.dot((q * q_dec).astype(BF16), s.astype(BF16), preferred_element_type=F32))
        kd = (k * k_dec).T.astype(BF16)
        s_ref[j] = s * s_dec + jnp.dot(kd, vb, preferred_element_type=F32)
        mu = jnp.mean(o, axis=-1, keepdims=True)
        oc = o - mu
        var = jnp.mean(oc * oc, axis=-1, keepdims=True)
        y = oc * lax.rsqrt(var + NORM_EPS) * gn_ref[:, v_cols]
        g = g_ref[0, :, v_cols]
        y_ref[0, :, v_cols] = (g * jax.nn.sigmoid(g) * y).astype(BF16)

    @pl.when(li == pl.num_programs(2) - 1)
    def _():
        sout_ref[0] = s_ref[...]


def _retention(proj, cos, sin, log_gamma, gn_w, s0, b, l):
    cl = min(l, CHUNK)
    lb = _tile(l, 4 * cl)
    p3 = proj.reshape(b, l, proj.shape[1])
    ng = RET_HEADS // RET_GROUP
    qw, vw = RET_GROUP * RET_DK, RET_GROUP * RET_DV
    has_state = s0 is not None
    in_specs = [pl.BlockSpec(memory_space=pltpu.SMEM),
                pl.BlockSpec((1, lb, qw), lambda bi, h, li: (bi, li, h)),
                pl.BlockSpec((1, lb, qw), lambda bi, h, li: (bi, li, ng + h)),
                pl.BlockSpec((1, lb, vw), lambda bi, h, li: (bi, li, ng + h)),
                pl.BlockSpec((1, lb, vw), lambda bi, h, li: (bi, li, 2 * ng + h)),
                pl.BlockSpec((lb, RET_DK // 2), lambda bi, h, li: (li, 0)),
                pl.BlockSpec((lb, RET_DK // 2), lambda bi, h, li: (li, 0)),
                pl.BlockSpec((1, vw), lambda bi, h, li: (0, h)),
                pl.BlockSpec((RET_GROUP, lb, lb), lambda bi, h, li: (h, 0, 0))]
    pos = jnp.arange(lb)
    dt = pos[:, None] - pos[None, :]
    same = (pos[:, None] // cl) == (pos[None, :] // cl)
    dist = jnp.where(same, jnp.abs(dt), dt).astype(F32)
    decay = jnp.where((same | (dt > 0))[None], jnp.exp(dist[None] * log_gamma[:, None, None]), 0.0)
    args = [log_gamma, p3, p3, p3, p3, cos, sin, gn_w.reshape(1, RET_V), decay]
    state_spec = pl.BlockSpec((1, RET_GROUP, RET_DK, RET_DV), lambda bi, h, li: (bi, h, 0, 0))
    if has_state:
        in_specs.append(state_spec)
        args.append(s0)
    y, s = pl.pallas_call(
        functools.partial(_retention_kernel, lb=lb, has_state=has_state),
        out_shape=[jax.ShapeDtypeStruct((b, l, RET_V), BF16),
                   jax.ShapeDtypeStruct((b, RET_HEADS, RET_DK, RET_DV), F32)],
        grid=(b, ng, l // lb),
        in_specs=in_specs,
        out_specs=[pl.BlockSpec((1, lb, vw), lambda bi, h, li: (bi, li, h)), state_spec],
        scratch_shapes=[pltpu.VMEM((RET_GROUP, RET_DK, RET_DV), F32)],
        compiler_params=_params("parallel", "parallel", "arbitrary"),
        name="retention",
    )(*args)
    return y.reshape(b * l, RET_V), s


def _hgrn_kernel(q_ref, fz_ref, v_ref, g_ref, bf_ref, lb_ref, nw_ref, tri_ref, *rest, lb_rows, has_state):
    if has_state:
        s0_ref, y_ref, sout_ref, st_ref, gc_ref, k_ref, o_ref = rest
    else:
        y_ref, sout_ref, st_ref, gc_ref, k_ref, o_ref = rest
    li = pl.program_id(1)
    hb = HG_BLOCK
    half = hb // 2

    @pl.when(li == 0)
    def _():
        for h in range(HG_HEADS):
            if has_state:
                st_ref[h] = s0_ref[0, h].T
            else:
                st_ref[h] = jnp.zeros((HG_DV, HG_DK), F32)

    lbv = lb_ref[...]
    f = lbv + (1.0 - lbv) * jax.nn.sigmoid(fz_ref[0] + bf_ref[...])
    logf = jnp.log(f)
    k_ref[...] = 1.0 - f
    gc_ref[...] = _tri_cumsum(tri_ref[...], logf) * LOG2E

    rt = lax.broadcasted_iota(jnp.int32, (half, 1), 0)
    contract_last = (((1,), (1,)), ((), ()))

    def block(bi, carry):
        r0 = pl.multiple_of(bi * hb, hb)
        for h in range(HG_HEADS):
            cs = slice(h * HG_DK, (h + 1) * HG_DK)
            gb = gc_ref[pl.ds(r0, hb), cs]
            qb = q_ref[0, pl.ds(r0, hb), cs]
            kb = k_ref[pl.ds(r0, hb), cs]
            vb = v_ref[0, pl.ds(r0, hb), cs]
            q_top, q_bot = qb[:half], qb[half:]
            g_top, g_bot = gb[:half], gb[half:]
            gk = gb - jnp.log2(kb)
            i_top = jnp.zeros((half, HG_DV), F32)
            i_bot = jnp.zeros((half, HG_DV), F32)
            for s in range(hb):
                gs, vs = gk[s:s + 1], vb[s:s + 1]
                if s < half:
                    e = jnp.where(rt >= s, jnp.exp2(g_top - gs), 0.0)
                    a = jnp.sum(q_top * e, axis=-1, keepdims=True)
                    i_top = i_top + a * vs
                    e = jnp.exp2(g_bot - gs)
                else:
                    e = jnp.where(rt + half >= s, jnp.exp2(g_bot - gs), 0.0)
                a = jnp.sum(q_bot * e, axis=-1, keepdims=True)
                i_bot = i_bot + a * vs
            intra = jnp.concatenate([i_top, i_bot], axis=0)
            st = st_ref[h]
            qt = (qb * jnp.exp2(gb)).astype(BF16)
            inter = lax.dot_general(qt, st.astype(BF16), contract_last, preferred_element_type=F32)
            o_ref[pl.ds(r0, hb), cs] = intra + inter
            gl = gb[hb - 1:hb]
            kt = (kb * jnp.exp2(gl - gb)).astype(BF16)
            upd = jnp.dot(vb.T.astype(BF16), kt, preferred_element_type=F32)
            st_ref[h] = st * jnp.exp2(gl) + upd
        return carry

    lax.fori_loop(0, lb_rows // hb, block, 0)

    for h in range(HG_HEADS):
        cs = slice(h * HG_DK, (h + 1) * HG_DK)
        oh = o_ref[:, cs]
        on = oh * lax.rsqrt(jnp.mean(oh * oh, axis=-1, keepdims=True) + NORM_EPS)
        g = g_ref[0, :, cs]
        y_ref[0, :, cs] = (on * nw_ref[:, cs] * (g * jax.nn.sigmoid(g))).astype(BF16)

    @pl.when(li == pl.num_programs(1) - 1)
    def _():
        for h in range(HG_HEADS):
            sout_ref[0, h] = st_ref[h].T


def _hgrn(proj, b_f, lb, norm_w, s0, b, l):
    assert l % HG_BLOCK == 0
    lbr = _tile(l, 256)
    p3 = proj.reshape(b, l, 4 * D_MODEL)
    r = jnp.arange(lbr)
    tri = ((r[:, None] // HG_BLOCK == r[None, :] // HG_BLOCK) & (r[None, :] <= r[:, None])).astype(BF16)
    has_state = s0 is not None
    col = lambda j: pl.BlockSpec((1, lbr, D_MODEL), lambda bi, li: (bi, li, j))
    vec = pl.BlockSpec((1, D_MODEL), lambda bi, li: (0, 0))
    in_specs = [col(0), col(1), col(2), col(3), vec, vec, vec,
                pl.BlockSpec((lbr, lbr), lambda bi, li: (0, 0))]
    args = [p3, p3, p3, p3, b_f.reshape(1, D_MODEL), lb.reshape(1, D_MODEL), norm_w.reshape(1, D_MODEL), tri]
    state_spec = pl.BlockSpec((1, HG_HEADS, HG_DK, HG_DV), lambda bi, li: (bi, 0, 0, 0))
    if has_state:
        in_specs.append(state_spec)
        args.append(s0)
    y, s = pl.pallas_call(
        functools.partial(_hgrn_kernel, lb_rows=lbr, has_state=has_state),
        out_shape=[jax.ShapeDtypeStruct((b, l, D_MODEL), BF16),
                   jax.ShapeDtypeStruct((b, HG_HEADS, HG_DK, HG_DV), F32)],
        grid=(b, l // lbr),
        in_specs=in_specs,
        out_specs=[pl.BlockSpec((1, lbr, D_MODEL), lambda bi, li: (bi, li, 0)), state_spec],
        scratch_shapes=[pltpu.VMEM((HG_HEADS, HG_DV, HG_DK), F32),
                        pltpu.VMEM((lbr, D_MODEL), F32),
                        pltpu.VMEM((lbr, D_MODEL), F32),
                        pltpu.VMEM((lbr, D_MODEL), F32)],
        compiler_params=_params("parallel", "arbitrary"),
        name="hgrn2",
    )(*args)
    return y.reshape(b * l, D_MODEL), s


def _fox_prompt_kernel(q_ref, k_ref, v_ref, cq_ref, ck_ref, o_ref, *, l, tq, tk):
    scale2 = FOX_HD ** -0.5 * LOG2E
    row = lax.broadcasted_iota(jnp.int32, (tq, tk), 0)
    col = lax.broadcasted_iota(jnp.int32, (tq, tk), 1)
    diag_bias = {}
    for qi in range(l // tq):
        for kj in range(l // tk):
            q0, k0 = qi * tq, kj * tk
            if k0 <= q0 + tq - 1 and k0 + tk - 1 > q0 and q0 - k0 not in diag_bias:
                diag_bias[q0 - k0] = jnp.where(col <= row + (q0 - k0), 0.0, -jnp.inf)
    for qi in range(l // tq):
        q0 = qi * tq
        q = q_ref[0, 0, q0:q0 + tq, :]
        cq = cq_ref[0, 0, q0:q0 + tq, :]
        m = jnp.full((tq, 1), -jnp.inf, F32)
        den = jnp.zeros((tq, 1), F32)
        acc = jnp.zeros((tq, FOX_HD), F32)
        for kj in range(l // tk):
            k0 = kj * tk
            if k0 > q0 + tq - 1:
                continue
            s = lax.dot_general(q, k_ref[0, 0, k0:k0 + tk, :], (((1,), (1,)), ((), ())),
                                preferred_element_type=F32) * scale2
            s = s + (cq - ck_ref[0, 0, :, k0:k0 + tk])
            if k0 + tk - 1 > q0:
                s = s + diag_bias[q0 - k0]
            m_new = jnp.maximum(m, jnp.max(s, axis=-1, keepdims=True))
            w = jnp.exp2(m - m_new)
            p = jnp.exp2(s - m_new)
            den = den * w + jnp.sum(p, axis=-1, keepdims=True)
            acc = acc * w + jnp.dot(p.astype(BF16), v_ref[0, 0, k0:k0 + tk, :], preferred_element_type=F32)
            m = m_new
        o_ref[0, q0:q0 + tq, :] = (acc / den).astype(BF16)


def _fox_prompt_attend(q, k, v, csum, b, l):
    tq = _tile(l, 256)
    tk = _tile(l, 512)
    csum = csum * LOG2E
    cq = csum.transpose(0, 2, 1)[..., None]
    ck = csum.transpose(0, 2, 1)[:, :, None, :]
    head = pl.BlockSpec((1, 1, l, FOX_HD), lambda bi, h: (bi, h, 0, 0))
    o = pl.pallas_call(
        functools.partial(_fox_prompt_kernel, l=l, tq=tq, tk=tk),
        out_shape=jax.ShapeDtypeStruct((b, l, D_MODEL), BF16),
        grid=(b, FOX_HEADS),
        in_specs=[head, head, head,
                  pl.BlockSpec((1, 1, l, 1), lambda bi, h: (bi, h, 0, 0)),
                  pl.BlockSpec((1, 1, 1, l), lambda bi, h: (bi, h, 0, 0))],
        out_specs=pl.BlockSpec((1, l, FOX_HD), lambda bi, h: (bi, 0, h)),
        compiler_params=_params("parallel", "parallel"),
        name="fox_prompt_attention",
    )(q, k, v, cq, ck)
    return o.reshape(b * l, D_MODEL)


def _fox_sample_kernel(q_ref, kn_ref, vn_ref, kc_ref, vc_ref, cq_ref, ckc_ref, ckn_ref, o_ref, *, l, pc):
    nh = FOX_HEADS
    lshift = l.bit_length() - 1
    scale = FOX_HD ** -0.5
    contract_last = (((1,), (1,)), ((), ()))
    by_head = lambda ref: jnp.concatenate([ref[0, :, h * FOX_HD:(h + 1) * FOX_HD] for h in range(nh)],
                                          axis=0).astype(BF16)
    q = by_head(q_ref)
    cq = cq_ref[0]
    rows = nh * l
    qhead = jnp.right_shift(lax.broadcasted_iota(jnp.int32, (rows, 1), 0), lshift)
    kn, vn = by_head(kn_ref), by_head(vn_ref)
    col = lax.broadcasted_iota(jnp.int32, (1, rows), 1)
    qframe = jnp.bitwise_and(lax.broadcasted_iota(jnp.int32, (rows, 1), 0), l - 1)
    visible = (jnp.right_shift(col, lshift) == qhead) & (jnp.bitwise_and(col, l - 1) <= qframe)
    s = lax.dot_general(q, kn, contract_last, preferred_element_type=F32) * scale + (cq - ckn_ref[0])
    s = jnp.where(visible, s, -jnp.inf)
    m = jnp.max(s, axis=-1, keepdims=True)
    p = jnp.exp(s - m)
    den = jnp.sum(p, axis=-1, keepdims=True)
    acc = jnp.dot(p.astype(BF16), vn, preferred_element_type=F32)
    same_head = jnp.bitwise_and(lax.broadcasted_iota(jnp.int32, (1, pc * nh), 1), nh - 1) == qhead
    for c in range(kc_ref.shape[1] // (pc * nh)):
        ks = slice(c * pc * nh, (c + 1) * pc * nh)
        s = lax.dot_general(q, kc_ref[0, ks, :].astype(BF16), contract_last,
                            preferred_element_type=F32) * scale + (cq - ckc_ref[0, :, ks])
        s = jnp.where(same_head, s, -jnp.inf)
        m_new = jnp.maximum(m, jnp.max(s, axis=-1, keepdims=True))
        w = jnp.exp(m - m_new)
        p = jnp.exp(s - m_new)
        den = den * w + jnp.sum(p, axis=-1, keepdims=True)
        acc = acc * w + jnp.dot(p.astype(BF16), vc_ref[0, ks, :].astype(BF16), preferred_element_type=F32)
        m = m_new
    out = (acc / den).astype(BF16)
    for h in range(nh):
        o_ref[0, :, h * FOX_HD:(h + 1) * FOX_HD] = out[h * l:(h + 1) * l, :]


def _fox_sample_attend(q, kn, vn, cache_k, cache_v, csum, b, l):
    p = cache_k.shape[1]
    nh = FOX_HEADS
    assert nh & (nh - 1) == 0 and l & (l - 1) == 0
    pc = _tile(p, 256)
    c_new = csum[:, p:, :].transpose(0, 2, 1).reshape(b, nh * l)
    cq = c_new[:, :, None]
    ckn = c_new[:, None, :]
    ckc = csum[:, :p, :].reshape(b, 1, p * nh)
    tok = pl.BlockSpec((1, l, D_MODEL), lambda bi: (bi, 0, 0))
    cache = pl.BlockSpec((1, p * nh, FOX_HD), lambda bi: (bi, 0, 0))
    o = pl.pallas_call(
        functools.partial(_fox_sample_kernel, l=l, pc=pc),
        out_shape=jax.ShapeDtypeStruct((b, l, D_MODEL), BF16),
        grid=(b,),
        in_specs=[tok, tok, tok, cache, cache,
                  pl.BlockSpec((1, nh * l, 1), lambda bi: (bi, 0, 0)),
                  pl.BlockSpec((1, 1, p * nh), lambda bi: (bi, 0, 0)),
                  pl.BlockSpec((1, 1, nh * l), lambda bi: (bi, 0, 0))],
        out_specs=tok,
        compiler_params=_params("parallel"),
        name="fox_sample_attention",
    )(q.reshape(b, l, D_MODEL), kn.reshape(b, l, D_MODEL), vn.reshape(b, l, D_MODEL),
      cache_k.reshape(b, p * nh, FOX_HD), cache_v.reshape(b, p * nh, FOX_HD), cq, ckc, ckn)
    return o.reshape(b * l, D_MODEL)


def _layer_norm(z, g, b):
    mu = jnp.mean(z, axis=-1, keepdims=True)
    zc = z - mu
    var = jnp.mean(zc * zc, axis=-1, keepdims=True)
    return zc * lax.rsqrt(var + LN_EPS) * g + b


def _outproj_norm_kernel(y_ref, w_ref, x_ref, gate_ref, sc_ref, sh_ref, lng_ref, lnb_ref, rw_ref,
                         xn_ref, u_ref, s_ref):
    out = jnp.dot(y_ref[...], w_ref[...], preferred_element_type=F32)
    z = ALPHA * x_ref[...] + (1.0 + gate_ref[0]) * out
    xn = _layer_norm(z, lng_ref[...], lnb_ref[...])
    xn_ref[...] = xn
    u = xn * (1.0 + sc_ref[0]) + sh_ref[0]
    u_ref[...] = u
    s_ref[...] = jax.nn.sigmoid(jnp.dot(u.astype(BF16), rw_ref[...], preferred_element_type=F32))


def _outproj_norm(y, w, x, gate, sc, sh, ln_g, ln_b, rw, tm, bps):
    t, kdim = y.shape
    d = w.shape[1]
    row = pl.BlockSpec((tm, d), lambda i: (i, 0))
    vec = pl.BlockSpec((1, d), lambda i: (0, 0))
    once = pl.Buffered(1)
    return pl.pallas_call(
        _outproj_norm_kernel,
        out_shape=[jax.ShapeDtypeStruct((t, d), F32), jax.ShapeDtypeStruct((t, d), F32),
                   jax.ShapeDtypeStruct((t, LANES), F32)],
        grid=(t // tm,),
        in_specs=[pl.BlockSpec((tm, kdim), lambda i: (i, 0)),
                  pl.BlockSpec((kdim, d), lambda i: (0, 0), pipeline_mode=once),
                  row, _mod_spec(gate, bps), _mod_spec(sc, bps), _mod_spec(sh, bps), vec, vec,
                  pl.BlockSpec((d, LANES), lambda i: (0, 0), pipeline_mode=once)],
        out_specs=[row, row, pl.BlockSpec((tm, LANES), lambda i: (i, 0))],
        compiler_params=_params("parallel"),
        name="outproj_norm",
    )(y, w, x, gate, sc, sh, ln_g.reshape(1, d), ln_b.reshape(1, d), rw)


def _argmax_first(vals):
    best, idx = vals[0], jnp.zeros(vals[0].shape, jnp.int32)
    for j in range(1, len(vals)):
        gt = vals[j] > best
        best = jnp.where(gt, vals[j], best)
        idx = jnp.where(gt, j, idx)
    return best, idx


def _pick(rows, idx):
    out = rows[0]
    for j in range(1, len(rows)):
        out = jnp.where(idx == j, rows[j], out)
    return out


def _route_kernel(s_ref, b_ref, tri_ref, e_ref, r_ref, w_ref, cnt_ref):
    i = pl.program_id(0)

    @pl.when(i == 0)
    def _():
        cnt_ref[...] = jnp.zeros_like(cnt_ref)

    sc = s_ref[...].T[:N_EXPERTS, :]
    sel = sc + b_ref[...]
    row = lambda a, e: a[e:e + 1, :]
    grp = []
    for g in range(N_GROUPS):
        a, b, c, d = (row(sel, g * EXPERTS_PER_GROUP + j) for j in range(EXPERTS_PER_GROUP))
        hi1, lo1, hi2, lo2 = jnp.maximum(a, b), jnp.minimum(a, b), jnp.maximum(c, d), jnp.minimum(c, d)
        grp.append(jnp.maximum(hi1, hi2) + jnp.maximum(jnp.minimum(hi1, hi2), jnp.maximum(lo1, lo2)))
    _, gidx = _argmax_first(grp)
    member = lambda a: [_pick([row(a, g * EXPERTS_PER_GROUP + j) for g in range(N_GROUPS)], gidx)
                        for j in range(EXPERTS_PER_GROUP)]
    v, c = member(sel), member(sc)
    _, j0 = _argmax_first(v)
    _, j1 = _argmax_first([jnp.where(j0 == j, -jnp.inf, v[j]) for j in range(EXPERTS_PER_GROUP)])
    c0, c1 = _pick(c, j0), _pick(c, j1)
    den = c0 + c1
    e0 = gidx * EXPERTS_PER_GROUP + j0
    e1 = gidx * EXPERTS_PER_GROUP + j1
    eio = lax.broadcasted_iota(jnp.int32, sc.shape, 0)
    oh0, oh1 = eio == e0, eio == e1
    member_f = jnp.where(oh0 | oh1, 1.0, 0.0)
    before = jnp.dot(member_f.astype(BF16), tri_ref[...], preferred_element_type=F32) + cnt_ref[...]
    r0 = jnp.sum(jnp.where(oh0, before, 0.0), axis=0, keepdims=True)
    r1 = jnp.sum(jnp.where(oh1, before, 0.0), axis=0, keepdims=True)
    cnt_ref[...] += jnp.sum(member_f, axis=1, keepdims=True)
    e_ref[0:1, :] = e0
    e_ref[1:2, :] = e1
    r_ref[0:1, :] = r0.astype(jnp.int32)
    r_ref[1:2, :] = r1.astype(jnp.int32)
    w_ref[0:1, :] = c0 / den
    w_ref[1:2, :] = c1 / den


def _route(scores, router_b, tm_e):
    t = scores.shape[0]
    tm = _tile(t, 512)
    r = jnp.arange(tm)
    tri = (r[:, None] < r[None, :]).astype(BF16)
    slot = pl.BlockSpec((TOP_K, tm), lambda i: (0, i))
    eidx, rank, wts, counts = pl.pallas_call(
        _route_kernel,
        out_shape=[jax.ShapeDtypeStruct((TOP_K, t), jnp.int32), jax.ShapeDtypeStruct((TOP_K, t), jnp.int32),
                   jax.ShapeDtypeStruct((TOP_K, t), F32), jax.ShapeDtypeStruct((N_EXPERTS, 1), F32)],
        grid=(t // tm,),
        in_specs=[pl.BlockSpec((tm, LANES), lambda i: (i, 0)),
                  pl.BlockSpec((N_EXPERTS, 1), lambda i: (0, 0)),
                  pl.BlockSpec((tm, tm), lambda i: (0, 0))],
        out_specs=[slot, slot, slot, pl.BlockSpec((N_EXPERTS, 1), lambda i: (0, 0))],
        compiler_params=_params("arbitrary"),
        name="moe_route",
    )(scores, router_b.astype(F32).reshape(N_EXPERTS, 1), tri)
    counts = counts[:, 0].astype(jnp.int32)
    padded = (counts + tm_e - 1) // tm_e * tm_e
    pad_end = jnp.cumsum(padded)
    pad_start = pad_end - padded
    onehot = eidx[:, :, None] == jnp.arange(N_EXPERTS, dtype=jnp.int32)[None, None, :]
    dest = (jnp.sum(jnp.where(onehot, pad_start[None, None, :], 0), axis=-1) + rank).reshape(TOP_K * t)
    n_blocks = (t * TOP_K + N_EXPERTS * (tm_e - 1) + tm_e - 1) // tm_e
    blk_row = jnp.arange(n_blocks, dtype=jnp.int32) * tm_e
    blk_e = jnp.minimum(jnp.sum((pad_end[None, :] <= blk_row[:, None]).astype(jnp.int32), axis=1), N_EXPERTS - 1)
    n_used = (pad_end[-1] // tm_e).astype(jnp.int32).reshape(1)
    last_blk = jnp.concatenate([jnp.where(counts > 0, pad_end - tm_e, -1).astype(jnp.int32), n_used])
    return dest.astype(jnp.int32), wts.T, blk_e, n_used, n_blocks, last_blk


def _dispatch_kernel(dest_ref, last_ref, u_ref, xs_ref, zero_ref, sem, zsem, *, tb, t_total, tm_e, n_blocks):
    base = pl.program_id(0) * tb

    @pl.when(pl.program_id(0) == 0)
    def _():
        zero_ref[...] = jnp.zeros_like(zero_ref)

        def zero_copy(e):
            row = pl.multiple_of(jnp.maximum(last_ref[e], 0), tm_e)
            return pltpu.make_async_copy(zero_ref, xs_ref.at[pl.ds(row, tm_e), :], zsem)

        for e in range(N_EXPERTS):
            pl.when(last_ref[e] >= 0)(lambda e=e: zero_copy(e).start())
        for e in range(N_EXPERTS):
            pl.when(last_ref[e] >= 0)(lambda e=e: zero_copy(e).wait())

        def tail_copy(blk):
            return pltpu.make_async_copy(zero_ref, xs_ref.at[pl.ds(pl.multiple_of(blk * tm_e, tm_e), tm_e), :], zsem)

        n_used = last_ref[N_EXPERTS]
        lax.fori_loop(n_used, n_blocks, lambda blk, c: (tail_copy(blk).start(), c)[1], 0)
        lax.fori_loop(n_used, n_blocks, lambda blk, c: (tail_copy(blk).wait(), c)[1], 0)

    def row_copy(t, slot):
        d = dest_ref[slot * t_total + base + t]
        return pltpu.make_async_copy(u_ref.at[pl.ds(t, 1), :], xs_ref.at[pl.ds(d, 1), :], sem)

    def issue(t, carry):
        for slot in range(TOP_K):
            row_copy(t, slot).start(priority=slot)
        return carry

    lax.fori_loop(0, tb, issue, 0, unroll=DMA_UNROLL)
    for slot in range(TOP_K):
        pltpu.make_async_copy(u_ref, xs_ref.at[pl.ds(0, tb), :], sem).wait()


def _dispatch(u, dest, last_blk, n_rows, tb, tm_e):
    t, d = u.shape
    return pl.pallas_call(
        functools.partial(_dispatch_kernel, tb=tb, t_total=t, tm_e=tm_e, n_blocks=n_rows // tm_e),
        out_shape=jax.ShapeDtypeStruct((n_rows, d), F32),
        grid_spec=pltpu.PrefetchScalarGridSpec(
            num_scalar_prefetch=2,
            grid=(t // tb,),
            in_specs=[pl.BlockSpec((tb, d), lambda i, dest_ref, last_ref: (i, 0))],
            out_specs=pl.BlockSpec(memory_space=pl.ANY),
            scratch_shapes=[pltpu.VMEM((tm_e, d), F32), pltpu.SemaphoreType.DMA, pltpu.SemaphoreType.DMA],
        ),
        compiler_params=_params("arbitrary"),
        name="moe_dispatch",
    )(dest, last_blk, u)


def _expert_kernel(blk_e_ref, n_used_ref, xs_ref, wg_ref, wu_ref, wd_ref, ys_ref):
    del blk_e_ref
    i = pl.program_id(0)

    @pl.when(i < n_used_ref[0])
    def _():
        x = xs_ref[...].astype(BF16)
        g = jnp.dot(x, wg_ref[0], preferred_element_type=F32)
        up = jnp.dot(x, wu_ref[0], preferred_element_type=F32)
        hid = (g * jax.nn.sigmoid(g) * up).astype(BF16)
        ys_ref[...] = jnp.dot(hid, wd_ref[0], preferred_element_type=F32)

    @pl.when(i >= n_used_ref[0])
    def _():
        ys_ref[...] = jnp.zeros_like(ys_ref)


def _experts(xs, blk_e, n_used, wg, wu, wd, tm_e):
    r, d = xs.shape
    de = wg.shape[2]
    return pl.pallas_call(
        _expert_kernel,
        out_shape=jax.ShapeDtypeStruct((r, d), F32),
        grid_spec=pltpu.PrefetchScalarGridSpec(
            num_scalar_prefetch=2,
            grid=(r // tm_e,),
            in_specs=[pl.BlockSpec((tm_e, d), lambda i, be, nu: (i, 0)),
                      pl.BlockSpec((1, d, de), lambda i, be, nu: (be[i], 0, 0)),
                      pl.BlockSpec((1, d, de), lambda i, be, nu: (be[i], 0, 0)),
                      pl.BlockSpec((1, de, d), lambda i, be, nu: (be[i], 0, 0))],
            out_specs=pl.BlockSpec((tm_e, d), lambda i, be, nu: (i, 0)),
        ),
        compiler_params=_params("arbitrary"),
        name="moe_experts",
    )(blk_e, n_used, xs, wg, wu, wd)


def _combine_norm_kernel(dest_ref, ys_ref, wts_ref, x_ref, gate_ref, lng_ref, lnb_ref, *rest,
                         tb, t_total, next_mod):
    if next_mod:
        sc_ref, sh_ref, xn_ref, u_ref, ya0, ya1, yb0, yb1, sem_a, sem_b = rest
    else:
        sc_ref = sh_ref = u_ref = None
        xn_ref, ya0, ya1, yb0, yb1, sem_a, sem_b = rest
    i = pl.program_id(0)
    n = pl.num_programs(0)
    buf_a, buf_b = (ya0, ya1), (yb0, yb1)

    def start_gathers(blk, bufs, sem):
        base = blk * tb
        for t in range(tb):
            for slot in range(TOP_K):
                d = dest_ref[slot * t_total + base + t]
                pltpu.make_async_copy(ys_ref.at[pl.ds(d, 1), :], bufs[slot].at[pl.ds(t, 1), :],
                                      sem).start(priority=slot)

    def wait_gathers(bufs, sem):
        for buf in bufs:
            pltpu.make_async_copy(ys_ref.at[pl.ds(0, tb), :], buf, sem).wait()

    def mod_rows(ref, rows):
        return ref[0] if ref.shape[1] == 1 else ref[0, rows, :]

    def finish(rows, bufs):
        w = wts_ref[rows, :]
        ffn = w[:, 0:1] * bufs[0][...] + w[:, 1:2] * bufs[1][...]
        z = ALPHA * x_ref[rows, :] + (1.0 + mod_rows(gate_ref, rows)) * ffn
        xn = _layer_norm(z, lng_ref[...], lnb_ref[...])
        xn_ref[rows, :] = xn
        if next_mod:
            u_ref[rows, :] = xn * (1.0 + mod_rows(sc_ref, rows)) + mod_rows(sh_ref, rows)

    @pl.when(i == 0)
    def _():
        start_gathers(0, buf_a, sem_a)

    start_gathers(2 * i + 1, buf_b, sem_b)
    wait_gathers(buf_a, sem_a)
    finish(slice(0, tb), buf_a)
    start_gathers(jnp.minimum(2 * i + 2, 2 * n - 2), buf_a, sem_a)
    wait_gathers(buf_b, sem_b)
    finish(slice(tb, 2 * tb), buf_b)

    @pl.when(i == n - 1)
    def _():
        wait_gathers(buf_a, sem_a)


def _combine_norm(ys, dest, wts, x, gate, next_mod, ln_g, ln_b, tb, bps):
    t, d = x.shape
    tb2 = 2 * tb
    assert t % tb2 == 0

    def mod2(m):
        if m.shape[1] == 1:
            assert bps % 2 == 0
            return m, bps // 2
        return m.reshape(t // tb2, tb2, d), 1

    row = pl.BlockSpec((tb2, d), lambda i, dr: (i, 0))
    vec = pl.BlockSpec((1, d), lambda i, dr: (0, 0))
    gate, gbps = mod2(gate)
    in_specs = [pl.BlockSpec(memory_space=pl.ANY),
                pl.BlockSpec((tb2, TOP_K), lambda i, dr: (i, 0)),
                row, _mod_spec(gate, gbps), vec, vec]
    args = [dest, ys, wts, x, gate, ln_g.reshape(1, d), ln_b.reshape(1, d)]
    out_shape = [jax.ShapeDtypeStruct((t, d), F32)]
    if next_mod is not None:
        for m in next_mod:
            m, mbps = mod2(m)
            in_specs.append(_mod_spec(m, mbps))
            args.append(m)
        out_shape.append(jax.ShapeDtypeStruct((t, d), F32))
    outs = pl.pallas_call(
        functools.partial(_combine_norm_kernel, tb=tb, t_total=t, next_mod=next_mod is not None),
        out_shape=out_shape,
        grid_spec=pltpu.PrefetchScalarGridSpec(
            num_scalar_prefetch=1,
            grid=(t // tb2,),
            in_specs=in_specs,
            out_specs=[row] * len(out_shape),
            scratch_shapes=[pltpu.VMEM((tb, d), F32)] * 4 + [pltpu.SemaphoreType.DMA] * 2,
        ),
        compiler_params=_params("arbitrary"),
        name="moe_combine_norm",
    )(*args)
    return outs if next_mod is not None else (outs[0], None)


def _moe_block(x, u, scores, router_b, wg, wu, wd, gate, next_mod, ln_g, ln_b, tm, bps, tm_e):
    dest, wts, blk_e, n_used, n_blocks, last_blk = _route(scores, router_b, tm_e)
    xs = _dispatch(u, dest, last_blk, n_blocks * tm_e, tm, tm_e)
    ys = _experts(xs, blk_e, n_used, wg, wu, wd, tm_e)
    tb = _tile(tm, 256) if gate.shape[1] == 1 else tm
    return _combine_norm(ys, dest, wts, x, gate, next_mod, ln_g, ln_b, tb, bps * (tm // tb))


def _rope_tables(l, pos0):
    half = RET_DK // 2
    inv = ROPE_BASE ** (-jnp.arange(half, dtype=F32) / half)
    ang = (pos0 + jnp.arange(l)).astype(F32)[:, None] * inv[None, :]
    return jnp.cos(ang), jnp.sin(ang)


def _stream(x3, mods, wts, state_ret, state_hgrn, cache_k, cache_v, cache_logf, pos0, tm_e):
    b, l, d = x3.shape
    t = b * l
    fresh = state_ret is None
    if fresh:
        tm = _tile(l, 256)
        bps = l // tm
        expand = lambda m: m[:, None, :]
    else:
        tm = _tile(t, 256)
        bps = 1
        expand = lambda m: jnp.repeat(m, l, axis=0).reshape(t // tm, tm, d)
    x = x3.reshape(t, d)
    outs = dict(ret=[], hg=[], fk=[], fv=[], fl=[])
    log_gamma = jnp.log1p(-jnp.exp2(-5.0 - jnp.arange(RET_HEADS, dtype=F32)))
    cos, sin = _rope_tables(l, pos0)
    u = None
    for layer in range(DEPTH):
        m = [expand(a) for a in mods[layer]]
        kind, j = layer % N_MIXERS, layer // N_MIXERS
        if kind == 0:
            if layer == 0:
                proj = _inproj(x, wts['ret_w_in'][j], (m[1], m[0], l if fresh else None))
            else:
                proj = _inproj(u, wts['ret_w_in'][j])
            s0 = None if fresh else state_ret[j]
            y, s = _retention(proj, cos, sin, log_gamma, wts['ret_gn_w'][j], s0, b, l)
            outs['ret'].append(s)
            w_out = wts['ret_w_out'][j]
        elif kind == 1:
            proj = _inproj(u, wts['hg_w_in'][j])
            s0 = None if fresh else state_hgrn[j]
            y, s = _hgrn(proj, wts['hg_b_f'][j], wts['lbs'][layer], wts['hg_norm_w'][j], s0, b, l)
            outs['hg'].append(s)
            w_out = wts['hg_w_out'][j]
        else:
            wq, wk, wv, wf, bf = wts['fox_in'][j]
            logf, csum = (a[:, :FOX_HEADS].reshape(b, l, FOX_HEADS) for a in _fox_gate(u, wf, bf, b, l))
            if fresh:
                tf = _tile(l, 512)
                (qh,) = _fox_inproj(u, wq, b, l, tf, False, True)
                kt, kh = _fox_inproj(u, wk, b, l, tf, True, True)
                vt, vh = _fox_inproj(u, wv, b, l, tf, True, True)
                y = _fox_prompt_attend(qh, kh, vh, csum, b, l)
            else:
                qt = _inproj(u, wq)
                kt = _inproj(u, wk)
                vt = _inproj(u, wv)
                lf_all = jnp.concatenate([cache_logf[j].astype(F32), logf], axis=1)
                pos = jnp.arange(lf_all.shape[1])
                csum = jnp.einsum('ts,bsh->bth', (pos[None, :] <= pos[:, None]).astype(F32), lf_all,
                                  precision=lax.Precision.HIGHEST)
                y = _fox_sample_attend(qt, kt, vt, cache_k[j], cache_v[j], csum, b, l)
            outs['fk'].append(kt.reshape(b, l, FOX_HEADS, FOX_HD))
            outs['fv'].append(vt.reshape(b, l, FOX_HEADS, FOX_HD))
            outs['fl'].append(logf)
            w_out = wts['fox_w_out'][j]
        x, u, scores = _outproj_norm(y, w_out, x, m[2], m[4], m[3], wts['ln_mix_g'][layer],
                                     wts['ln_mix_b'][layer], wts['router_w'], tm, bps)
        nxt = (expand(mods[layer + 1][1]), expand(mods[layer + 1][0])) if layer + 1 < DEPTH else None
        x, u = _moe_block(x, u, scores, wts['router_b'], wts['moe_w_gate'][layer], wts['moe_w_up'][layer],
                          wts['moe_w_down'][layer], m[5], nxt, wts['ln_ffn_g'][layer],
                          wts['ln_ffn_b'][layer], tm, bps, tm_e)
    return x.reshape(b, l, d), outs


def kernel(x_prompt, x_sample, state_ret, state_hgrn, cache_fox_k, cache_fox_v, cache_fox_logf, c_prompt, c_sample, ada_w, ada_b, ln_mix_g, ln_mix_b, ln_ffn_g, ln_ffn_b, ret_w_in, ret_gn_w, ret_w_out, hg_w_in, hg_b_f, hg_lower_bounds, hg_norm_w, hg_w_out, fox_w_in, fox_b_f, fox_w_out, router_w, router_b, moe_w_gate, moe_w_up, moe_w_down):
    dt = x_prompt.dtype
    d = D_MODEL
    nbp = c_prompt.shape[0]
    lbs = jnp.cumsum(jax.nn.softmax(hg_lower_bounds.astype(F32), axis=0), axis=0)
    lbs = lbs - lbs[0]
    mod_all = _modulation_all(jnp.concatenate([c_prompt, c_sample], axis=0).astype(F32), ada_w, ada_b)
    split6 = lambda m: [m[:, i * d:(i + 1) * d] for i in range(6)]
    mods_p = [split6(mod_all[layer, :nbp]) for layer in range(DEPTH)]
    mods_s = [split6(mod_all[layer, nbp:]) for layer in range(DEPTH)]
    pad = LANES - FOX_HEADS
    fox_in = [(fox_w_in[j, :, :d].astype(BF16), fox_w_in[j, :, d:2 * d].astype(BF16),
               fox_w_in[j, :, 2 * d:3 * d].astype(BF16),
               jnp.pad(fox_w_in[j, :, 3 * d:], ((0, 0), (0, pad))).astype(BF16),
               jnp.pad(fox_b_f[j].astype(F32), (0, pad)).reshape(1, LANES))
              for j in range(fox_w_in.shape[0])]
    wts = dict(
        ret_w_in=ret_w_in.astype(BF16), ret_gn_w=ret_gn_w.astype(F32), ret_w_out=ret_w_out.astype(BF16),
        hg_w_in=hg_w_in.astype(BF16), hg_b_f=hg_b_f.astype(F32), lbs=lbs, hg_norm_w=hg_norm_w.astype(F32),
        hg_w_out=hg_w_out.astype(BF16), fox_in=fox_in, fox_w_out=fox_w_out.astype(BF16),
        router_w=jnp.pad(router_w, ((0, 0), (0, LANES - N_EXPERTS))).astype(BF16), router_b=router_b,
        moe_w_gate=moe_w_gate.astype(BF16), moe_w_up=moe_w_up.astype(BF16), moe_w_down=moe_w_down.astype(BF16),
        ln_mix_g=ln_mix_g.astype(F32), ln_mix_b=ln_mix_b.astype(F32),
        ln_ffn_g=ln_ffn_g.astype(F32), ln_ffn_b=ln_ffn_b.astype(F32))
    past_len = cache_fox_k.shape[2]
    yp, op = _stream(x_prompt, mods_p, wts, None, None, None, None, None, 0, 256)
    ys, os_ = _stream(x_sample, mods_s, wts, state_ret, state_hgrn, cache_fox_k, cache_fox_v,
                      cache_fox_logf, past_len, 128)
    st = lambda xs: jnp.stack(xs).astype(dt)
    return (yp, ys, st(op['ret']), st(os_['ret']), st(op['hg']), st(os_['hg']),
            st(op['fk']), st(op['fv']), st(op['fl']), st(os_['fk']), st(os_['fv']), st(os_['fl']))
```
